```python
import jax, jax.numpy as jnp
from jax import lax
import numpy as np

D_MODEL = 1024
BATCH = 2
SEQ = 8192
DEPTH = 1

A_HEADS = 4
A_DK = 128
A_DV = 128
A_QK = A_HEADS * A_DK
A_WIDTH = A_HEADS * A_DV
CHUNK = 64
B_HEADS = 8
B_KV_HEADS = 2
B_HEAD_DIM = 64
B_GROUP = B_HEADS // B_KV_HEADS
B_WIDTH = B_HEADS * B_HEAD_DIM
B_KV = B_KV_HEADS * B_HEAD_DIM
WINDOW = 128
BLOCK = 128
MIX_WIDTH = A_WIDTH + B_WIDTH
D_FF = -(-8 * D_MODEL // (3 * 256)) * 256
EPS = 1e-6
IN_SIZES = (A_QK, A_QK, A_WIDTH, A_WIDTH, B_WIDTH, B_KV, B_KV)
IN_COLS = sum(IN_SIZES)
SPLITS = tuple(int(s) for s in np.cumsum(IN_SIZES)[:-1])

kernel_name = 'hymba_hgrn2_swa_sink_alibi_block'


def rmsnorm(x, g):
    xf = x.astype(jnp.float32)
    y = xf * lax.rsqrt(jnp.mean(xf * xf, axis=-1, keepdims=True) + EPS)
    return (y * g.astype(jnp.float32)).astype(x.dtype)


def hgrn2_mixer(q, f_logit, i, g, lb, norm_g):
    bsz, s, _ = q.shape
    n = s // CHUNK
    f32 = jnp.float32
    qf = jax.nn.silu(q.astype(f32))
    forget = lb + (1.0 - lb) * jax.nn.sigmoid(f_logit.astype(f32))
    key = 1.0 - forget
    log_f = jnp.log(forget)

    def chunks(t, d):
        return t.reshape(bsz, n, CHUNK, A_HEADS, d).transpose(0, 3, 1, 2, 4)

    q_c = chunks(qf, A_DK)
    k_c = chunks(key, A_DK)
    v_c = chunks(i.astype(f32), A_DV)
    b = jnp.cumsum(chunks(log_f, A_DK), axis=3)
    b_last = b[:, :, :, -1:, :]
    q_dec = q_c * jnp.exp(b)
    k_intra = k_c * jnp.exp(-b)
    k_state = k_c * jnp.exp(b_last - b)
    causal = jnp.tril(jnp.ones((CHUNK, CHUNK), dtype=bool))
    a = jnp.where(causal, jnp.einsum('bhnck,bhnsk->bhncs', q_dec, k_intra), 0.0)
    o_intra = jnp.einsum('bhncs,bhnsv->bhncv', a, v_c)
    upd = jnp.einsum('bhnsk,bhnsv->bhnkv', k_state, v_c)
    decay = jnp.exp(b_last[:, :, :, 0, :])

    def step(state, inp):
        dec, u = inp
        return dec[..., None] * state + u, state

    s0 = jnp.zeros((bsz, A_HEADS, A_DK, A_DV), f32)
    _, s_prev = lax.scan(step, s0, (jnp.moveaxis(decay, 2, 0), jnp.moveaxis(upd, 2, 0)))
    s_prev = jnp.moveaxis(s_prev, 0, 2)
    o = o_intra + jnp.einsum('bhnck,bhnkv->bhncv', q_dec, s_prev)
    o = o.transpose(0, 2, 3, 1, 4).reshape(bsz, s, A_HEADS, A_DV)
    o = o * lax.rsqrt(jnp.mean(o * o, axis=-1, keepdims=True) + EPS)
    o = o.reshape(bsz, s, A_WIDTH) * norm_g.astype(f32) * jax.nn.silu(g.astype(f32))
    return o.astype(q.dtype)


def alibi_slopes():
    return jnp.exp2(-(8.0 / B_HEADS) * jnp.arange(1, B_HEADS + 1, dtype=jnp.float32))


def swa_sink_attention(q, k, v, sinks, norm_g):
    bsz, s, _ = q.shape
    nb = s // BLOCK
    f32 = jnp.float32
    qb = q.reshape(bsz, nb, BLOCK, B_KV_HEADS, B_GROUP, B_HEAD_DIM)
    kb = k.reshape(bsz, nb, BLOCK, B_KV_HEADS, B_HEAD_DIM)
    vb = v.reshape(bsz, nb, BLOCK, B_KV_HEADS, B_HEAD_DIM)

    def with_prev(t):
        prev = jnp.pad(t, ((0, 0), (1, 0), (0, 0), (0, 0), (0, 0)))[:, :-1]
        return jnp.concatenate([prev, t], axis=2)

    kk = with_prev(kb)
    vv = with_prev(vb)
    scores = jnp.einsum('bnqhgd,bnshd->bnhgqs', qb, kk).astype(f32) * (B_HEAD_DIM ** -0.5)
    qi = jnp.arange(BLOCK)[:, None]
    kj = jnp.arange(2 * BLOCK)[None, :]
    dist = qi + BLOCK - kj
    key_pos = jnp.arange(nb)[:, None] * BLOCK - BLOCK + kj
    valid = ((dist >= 0) & (dist < WINDOW))[None] & (key_pos >= 0)[:, None, :]
    slopes = alibi_slopes().reshape(B_KV_HEADS, B_GROUP)
    scores = scores - slopes[:, :, None, None] * dist.astype(f32)
    scores = jnp.where(valid[None, :, None, None], scores, -jnp.inf)
    sink = sinks.astype(f32).reshape(B_KV_HEADS, B_GROUP)[None, None, :, :, None, None]
    m = jnp.maximum(jnp.max(scores, axis=-1, keepdims=True), sink)
    p = jnp.exp(scores - m)
    p = p / (jnp.sum(p, axis=-1, keepdims=True) + jnp.exp(sink - m))
    o = jnp.einsum('bnhgqs,bnshd->bnqhgd', p, vv.astype(f32))
    o = o.reshape(bsz, s, B_WIDTH)
    return rmsnorm(o, norm_g).astype(q.dtype)


def setup_inputs(seed: int = 0) -> dict:
    key = jax.random.key(seed)
    ks = jax.random.split(key, 13)
    f32 = jnp.float32
    nrm = lambda k, shape, scale: jax.random.normal(k, shape, f32) * scale
    return {
        'x': jax.random.normal(ks[0], (BATCH, SEQ, D_MODEL), f32),
        'ln1_g': 1.0 + nrm(ks[1], (DEPTH, D_MODEL), 0.02),
        'w_in': nrm(ks[2], (DEPTH, D_MODEL, IN_COLS), D_MODEL ** -0.5),
        'lb_logits': nrm(ks[3], (DEPTH + 1, A_QK), 0.1),
        'hgrn_norm_g': 1.0 + nrm(ks[4], (DEPTH, A_WIDTH), 0.02),
        'attn_sinks': nrm(ks[5], (DEPTH, B_HEADS), 0.5),
        'attn_norm_g': 1.0 + nrm(ks[6], (DEPTH, B_WIDTH), 0.02),
        'w_out': nrm(ks[7], (DEPTH, MIX_WIDTH, D_MODEL), MIX_WIDTH ** -0.5),
        'ln2_g': 1.0 + nrm(ks[8], (DEPTH, D_MODEL), 0.02),
        'w_gate': nrm(ks[9], (DEPTH, D_MODEL, D_FF), D_MODEL ** -0.5),
        'w_up': nrm(ks[10], (DEPTH, D_MODEL, D_FF), D_MODEL ** -0.5),
        'w_down': nrm(ks[11], (DEPTH, D_FF, D_MODEL), D_FF ** -0.5),
        'final_g': 1.0 + nrm(ks[12], (D_MODEL,), 0.02),
    }


def reference(x, ln1_g, w_in, lb_logits, hgrn_norm_g, attn_sinks, attn_norm_g, w_out, ln2_g, w_gate, w_up, w_down, final_g):
    lower_bounds = jnp.cumsum(jax.nn.softmax(lb_logits.astype(jnp.float32), axis=0), axis=0)
    h = x
    for l in range(DEPTH):
        u = rmsnorm(h, ln1_g[l])
        proj = u @ w_in[l]
        a_q, a_f, a_i, a_g, b_q, b_k, b_v = jnp.split(proj, SPLITS, axis=-1)
        y_a = hgrn2_mixer(a_q, a_f, a_i, a_g, lower_bounds[l], hgrn_norm_g[l])
        y_b = swa_sink_attention(b_q, b_k, b_v, attn_sinks[l], attn_norm_g[l])
        h = h + jnp.concatenate([y_a, y_b], axis=-1) @ w_out[l]
        u = rmsnorm(h, ln2_g[l])
        h = h + (jax.nn.silu(u @ w_gate[l]) * (u @ w_up[l])) @ w_down[l]
    return rmsnorm(h, final_g)
```

```python
import functools

import jax
import jax.numpy as jnp
from jax import lax
from jax.experimental import pallas as pl
from jax.experimental.pallas import tpu as pltpu

D_MODEL = 1024
A_HEADS = 4
A_DK = 128
A_DV = 128
A_QK = A_HEADS * A_DK
A_WIDTH = A_HEADS * A_DV
CHUNK = 64
B_HEADS = 8
B_KV_HEADS = 2
B_HEAD_DIM = 64
B_GROUP = B_HEADS // B_KV_HEADS
B_WIDTH = B_HEADS * B_HEAD_DIM
B_KV = B_KV_HEADS * B_HEAD_DIM
WINDOW = 128
BLOCK = 128
MIX_WIDTH = A_WIDTH + B_WIDTH
D_FF = 2816
IN_COLS = 4 * A_QK + B_WIDTH + 2 * B_KV
EPS = 1e-6
NEG_BIG = -1e30

F32 = jnp.float32
BF16 = jnp.bfloat16

V7X_VMEM_LIMIT_BYTES = 56 * 1024 * 1024

TM_PROJ = 512
TS_MIX = 512
TM_FFN = 512
FF_CHUNK = 256


def _rms_scale(v):
    return lax.rsqrt(jnp.mean(v * v, axis=-1, keepdims=True) + EPS)


def _silu(v):
    return v * (1.0 / (1.0 + jnp.exp(-v)))


def _sigmoid(v):
    return 1.0 / (1.0 + jnp.exp(-v))


def _in_proj_kernel(x_ref, g_ref, w_ref, o_ref):
    x = x_ref[...]
    u = (x * _rms_scale(x) * g_ref[...]).astype(BF16)
    o_ref[...] = jnp.dot(u, w_ref[...], preferred_element_type=F32)


def _in_proj(x2d, ln1_g, w_in_bf16):
    t = x2d.shape[0]
    return pl.pallas_call(
        _in_proj_kernel,
        grid=(t // TM_PROJ,),
        in_specs=[
            pl.BlockSpec((TM_PROJ, D_MODEL), lambda i: (i, 0)),
            pl.BlockSpec((1, D_MODEL), lambda i: (0, 0)),
            pl.BlockSpec((D_MODEL, IN_COLS), lambda i: (0, 0), pipeline_mode=pl.Buffered(1)),
        ],
        out_specs=pl.BlockSpec((TM_PROJ, IN_COLS), lambda i: (i, 0)),
        out_shape=jax.ShapeDtypeStruct((t, IN_COLS), F32),
        compiler_params=pltpu.CompilerParams(
            dimension_semantics=("arbitrary",), vmem_limit_bytes=V7X_VMEM_LIMIT_BYTES),
        name="in_proj",
    )(x2d, ln1_g, w_in_bf16)


def _mixer_kernel(sinks_ref, hq_ref, hf_ref, hi_ref, hg_ref, bq_ref, bk_ref, bv_ref,
                  lbl_ref, hng_ref, ang_ref, y_ref,
                  state_ref, kbuf_ref, vbuf_ref, bias_ref):
    b_idx = pl.program_id(0)
    s_idx = pl.program_id(1)
    ts = hq_ref.shape[1]

    @pl.when((b_idx == 0) & (s_idx == 0))
    def _():
        qi = lax.broadcasted_iota(jnp.int32, (BLOCK, 2 * BLOCK), 0)
        kj = lax.broadcasted_iota(jnp.int32, (BLOCK, 2 * BLOCK), 1)
        dist = qi + BLOCK - kj
        valid = (dist >= 0) & (dist < WINDOW)
        valid_first = valid & (kj >= BLOCK)
        distf = dist.astype(F32)
        for h in range(B_HEADS):
            slope = 2.0 ** (-(8.0 / B_HEADS) * (h + 1))
            sc = -slope * distf
            bias_ref[0, h] = jnp.where(valid, sc, NEG_BIG)
            bias_ref[1, h] = jnp.where(valid_first, sc, NEG_BIG)

    @pl.when(s_idx == 0)
    def _():
        state_ref[...] = jnp.zeros_like(state_ref)
        kbuf_ref[0:BLOCK, :] = jnp.zeros((BLOCK, B_KV), F32)
        vbuf_ref[0:BLOCK, :] = jnp.zeros((BLOCK, B_KV), F32)

    lbl = lbl_ref[...]
    lmax = jnp.max(lbl, axis=0, keepdims=True)
    le = jnp.exp(lbl - lmax)
    lb = le[0:1, :] / jnp.sum(le, axis=0, keepdims=True)

    r = lax.broadcasted_iota(jnp.int32, (CHUNK, CHUNK), 0)
    c = lax.broadcasted_iota(jnp.int32, (CHUNK, CHUNK), 1)
    causal = r >= c
    tri = jnp.where(causal, 1.0, 0.0).astype(BF16)

    def hgrn_chunk(ci, carry):
        rows = pl.ds(pl.multiple_of(ci * CHUNK, CHUNK), CHUNK)
        fl = hf_ref[0, rows, :]
        forget = lb + (1.0 - lb) * _sigmoid(fl)
        key = 1.0 - forget
        logf = jnp.log(forget)
        p0 = logf.astype(BF16)
        r1 = logf - p0.astype(F32)
        p1 = r1.astype(BF16)
        p2 = (r1 - p1.astype(F32)).astype(BF16)
        bcum = (jnp.dot(tri, p0, preferred_element_type=F32)
                + jnp.dot(tri, p1, preferred_element_type=F32)
                + jnp.dot(tri, p2, preferred_element_type=F32))
        b_last = bcum[CHUNK - 1:CHUNK, :]
        eb = jnp.exp(bcum)
        q_dec = _silu(hq_ref[0, rows, :]) * eb
        k_intra = key * jnp.exp(-bcum)
        k_state = key * jnp.exp(b_last - bcum)
        decay = jnp.exp(b_last)
        vv = hi_ref[0, rows, :]
        gate = _silu(hg_ref[0, rows, :])
        hng = hng_ref[...]
        for h in range(A_HEADS):
            cs = slice(h * A_DK, (h + 1) * A_DK)
            qd = q_dec[:, cs].astype(BF16)
            ki = k_intra[:, cs].astype(BF16)
            ks = k_state[:, cs]
            v = vv[:, cs]
            vb = v.astype(BF16)
            a = lax.dot_general(qd, ki, (((1,), (1,)), ((), ())), preferred_element_type=F32)
            a = jnp.where(causal, a, 0.0).astype(BF16)
            st = state_ref[h]
            o = (jnp.dot(a, vb, preferred_element_type=F32)
                 + lax.dot_general(qd, st.astype(BF16), (((1,), (1,)), ((), ())),
                                   preferred_element_type=F32))
            upd_t = lax.dot_general(vb, ks.astype(BF16), (((0,), (0,)), ((), ())),
                                    preferred_element_type=F32)
            state_ref[h] = st * decay[:, cs] + upd_t
            o = o * _rms_scale(o)
            y_ref[0, rows, cs] = (o * hng[:, cs] * gate[:, cs]).astype(y_ref.dtype)
        return carry

    lax.fori_loop(0, ts // CHUNK, hgrn_chunk, 0)

    kbuf_ref[BLOCK:, :] = bk_ref[0]
    vbuf_ref[BLOCK:, :] = bv_ref[0]
    ang = ang_ref[...]

    def swa_block(j, carry):
        rows = pl.ds(pl.multiple_of(j * BLOCK, BLOCK), BLOCK)
        krows = pl.ds(pl.multiple_of(j * BLOCK, BLOCK), 2 * BLOCK)
        first = jnp.where((s_idx == 0) & (j == 0), 1, 0)
        q = bq_ref[0, rows, :] * (B_HEAD_DIM ** -0.5)
        kk = kbuf_ref[krows, :].astype(BF16)
        vv = vbuf_ref[krows, :].astype(BF16)
        outs = []
        ssq = jnp.zeros((BLOCK, 1), F32)
        for h in range(B_HEADS):
            kvh = h // B_GROUP
            qh = q[:, h * B_HEAD_DIM:(h + 1) * B_HEAD_DIM].astype(BF16)
            kh = kk[:, kvh * B_HEAD_DIM:(kvh + 1) * B_HEAD_DIM]
            vh = vv[:, kvh * B_HEAD_DIM:(kvh + 1) * B_HEAD_DIM]
            s = lax.dot_general(qh, kh, (((1,), (1,)), ((), ())), preferred_element_type=F32)
            s = s + bias_ref[first, h]
            sink = sinks_ref[h]
            m = jnp.maximum(jnp.max(s, axis=-1, keepdims=True), sink)
            p = jnp.exp(s - m)
            denom = jnp.sum(p, axis=-1, keepdims=True) + jnp.exp(sink - m)
            o = jnp.dot(p.astype(BF16), vh, preferred_element_type=F32) * (1.0 / denom)
            ssq = ssq + jnp.sum(o * o, axis=-1, keepdims=True)
            outs.append(o)
        scale = lax.rsqrt(ssq * (1.0 / B_WIDTH) + EPS)
        o_all = jnp.concatenate(outs, axis=-1)
        y_ref[0, rows, A_WIDTH:] = (o_all * scale * ang).astype(y_ref.dtype)
        return carry

    lax.fori_loop(0, ts // BLOCK, swa_block, 0)

    kbuf_ref[0:BLOCK, :] = kbuf_ref[ts:ts + BLOCK, :]
    vbuf_ref[0:BLOCK, :] = vbuf_ref[ts:ts + BLOCK, :]


def _mixer(proj3d, sinks, lb_logits, hgrn_norm_g, attn_norm_g):
    bsz, seq, _ = proj3d.shape
    ts = TS_MIX
    wide = lambda col: pl.BlockSpec((1, ts, A_QK), lambda b, s, col=col: (b, s, col))
    kv = lambda col: pl.BlockSpec((1, ts, B_KV), lambda b, s, col=col: (b, s, col))
    kv_col0 = (4 * A_QK + B_WIDTH) // B_KV
    full = lambda shape: pl.BlockSpec(shape, lambda b, s: (0,) * len(shape))
    return pl.pallas_call(
        _mixer_kernel,
        grid=(bsz, seq // ts),
        in_specs=[
            pl.BlockSpec(memory_space=pltpu.SMEM),
            wide(0), wide(1), wide(2), wide(3), wide(4), kv(kv_col0), kv(kv_col0 + 1),
            full((2, A_QK)), full((1, A_WIDTH)), full((1, B_WIDTH)),
        ],
        out_specs=pl.BlockSpec((1, ts, MIX_WIDTH), lambda b, s: (b, s, 0)),
        out_shape=jax.ShapeDtypeStruct((bsz, seq, MIX_WIDTH), BF16),
        scratch_shapes=[
            pltpu.VMEM((A_HEADS, A_DV, A_DK), F32),
            pltpu.VMEM((ts + BLOCK, B_KV), F32),
            pltpu.VMEM((ts + BLOCK, B_KV), F32),
            pltpu.VMEM((2, B_HEADS, BLOCK, 2 * BLOCK), F32),
        ],
        compiler_params=pltpu.CompilerParams(
            dimension_semantics=("arbitrary", "arbitrary"),
            vmem_limit_bytes=V7X_VMEM_LIMIT_BYTES),
        name="mixer",
    )(sinks, proj3d, proj3d, proj3d, proj3d, proj3d, proj3d, proj3d,
      lb_logits, hgrn_norm_g, attn_norm_g)


def _ffn_kernel(x_ref, y_ref, wo_ref, g2_ref, wg_ref, wu_ref, wd_ref, gf_ref, o_ref, acc_ref):
    h = x_ref[...] + jnp.dot(y_ref[...], wo_ref[...], preferred_element_type=F32)
    u = (h * _rms_scale(h) * g2_ref[...]).astype(BF16)
    acc_ref[...] = h
    for c0 in range(0, D_FF, FF_CHUNK):
        cs = slice(c0, c0 + FF_CHUNK)
        gt = jnp.dot(u, wg_ref[:, cs], preferred_element_type=F32)
        up = jnp.dot(u, wu_ref[:, cs], preferred_element_type=F32)
        act = (_silu(gt) * up).astype(BF16)
        acc_ref[...] += jnp.dot(act, wd_ref[cs, :], preferred_element_type=F32)
    h2 = acc_ref[...]
    o_ref[...] = h2 * _rms_scale(h2) * gf_ref[...]


def _ffn(x2d, y2d, w_out, ln2_g, w_gate, w_up, w_down, final_g):
    t = x2d.shape[0]
    tm = TM_FFN
    const = lambda shape: pl.BlockSpec(shape, lambda i: (0, 0), pipeline_mode=pl.Buffered(1))
    return pl.pallas_call(
        _ffn_kernel,
        grid=(t // tm,),
        in_specs=[
            pl.BlockSpec((tm, D_MODEL), lambda i: (i, 0)),
            pl.BlockSpec((tm, MIX_WIDTH), lambda i: (i, 0)),
            const((MIX_WIDTH, D_MODEL)),
            const((1, D_MODEL)),
            const((D_MODEL, D_FF)),
            const((D_MODEL, D_FF)),
            const((D_FF, D_MODEL)),
            const((1, D_MODEL)),
        ],
        out_specs=pl.BlockSpec((tm, D_MODEL), lambda i: (i, 0)),
        out_shape=jax.ShapeDtypeStruct((t, D_MODEL), F32),
        scratch_shapes=[pltpu.VMEM((tm, D_MODEL), F32)],
        compiler_params=pltpu.CompilerParams(
            dimension_semantics=("arbitrary",), vmem_limit_bytes=V7X_VMEM_LIMIT_BYTES),
        name="ffn",
    )(x2d, y2d, w_out, ln2_g, w_gate, w_up, w_down, final_g)


def kernel(x, ln1_g, w_in, lb_logits, hgrn_norm_g, attn_sinks, attn_norm_g, w_out, ln2_g,
           w_gate, w_up, w_down, final_g):
    bsz, seq, d = x.shape
    x2d = x.reshape(bsz * seq, d)
    proj = _in_proj(x2d, ln1_g[0:1], w_in[0].astype(BF16))
    y = _mixer(proj.reshape(bsz, seq, IN_COLS), attn_sinks[0], lb_logits,
               hgrn_norm_g[0:1], attn_norm_g[0:1])
    out = _ffn(x2d, y.reshape(bsz * seq, MIX_WIDTH), w_out[0].astype(BF16), ln2_g[0:1],
               w_gate[0].astype(BF16), w_up[0].astype(BF16), w_down[0].astype(BF16),
               final_g.reshape(1, d))
    return out.reshape(bsz, seq, d)
```

```python
import jax
import jax.numpy as jnp
from jax import lax
from jax.experimental import pallas as pl
from jax.experimental.pallas import tpu as pltpu

D_MODEL = 1024
A_HEADS = 4
A_DK = 128
A_DV = 128
A_QK = A_HEADS * A_DK
A_WIDTH = A_HEADS * A_DV
CHUNK = 64
B_HEADS = 8
B_KV_HEADS = 2
B_HEAD_DIM = 64
B_GROUP = B_HEADS // B_KV_HEADS
B_WIDTH = B_HEADS * B_HEAD_DIM
B_KV = B_KV_HEADS * B_HEAD_DIM
WINDOW = 128
BLOCK = 128
MIX_WIDTH = A_WIDTH + B_WIDTH
D_FF = 2816
IN_COLS = 4 * A_QK + B_WIDTH + 2 * B_KV
EPS = 1e-6
NEG_BIG = -1e30
LANES = 128

F32 = jnp.float32
BF16 = jnp.bfloat16

V7X_VMEM_LIMIT_BYTES = 56 * 1024 * 1024

TM_PROJ = 512
TS_MIX = 512
TM_FFN = 512
FF_CHUNK = 256
HGRN_OUT_CHUNKS = 4
SWA_BLOCKS = 4
LOG2E = 1.4426950408889634

NT_DIMS = (((1,), (1,)), ((), ()))
TN_DIMS = (((0,), (0,)), ((), ()))


def _rms_scale(v):
    return lax.rsqrt(jnp.mean(v * v, axis=-1, keepdims=True) + EPS)


def _silu(v):
    return v * (1.0 / (1.0 + jnp.exp(-v)))


def _sigmoid(v):
    return 1.0 / (1.0 + jnp.exp(-v))


def _in_proj_kernel(x_ref, g_ref, w_ref, o_ref):
    x = x_ref[...]
    u = (x * _rms_scale(x) * g_ref[...]).astype(BF16)
    o_ref[...] = jnp.dot(u, w_ref[...], preferred_element_type=F32)


def _in_proj(x2d, ln1_g, w_in_bf16):
    t = x2d.shape[0]
    return pl.pallas_call(
        _in_proj_kernel,
        grid=(t // TM_PROJ,),
        in_specs=[
            pl.BlockSpec((TM_PROJ, D_MODEL), lambda i: (i, 0)),
            pl.BlockSpec((1, D_MODEL), lambda i: (0, 0)),
            pl.BlockSpec((D_MODEL, IN_COLS), lambda i: (0, 0), pipeline_mode=pl.Buffered(1)),
        ],
        out_specs=pl.BlockSpec((TM_PROJ, IN_COLS), lambda i: (i, 0)),
        out_shape=jax.ShapeDtypeStruct((t, IN_COLS), F32),
        compiler_params=pltpu.CompilerParams(
            dimension_semantics=("arbitrary",), vmem_limit_bytes=V7X_VMEM_LIMIT_BYTES),
        name="in_proj",
    )(x2d, ln1_g, w_in_bf16)


def _mixer_kernel(sinks_ref, hq_ref, hf_ref, hi_ref, hg_ref, bq_ref, bk_ref, bv_ref,
                  lbl_ref, hng_ref, ang_ref, y_ref,
                  state_ref, qd_ref, ki_ref, v_ref, dec_ref, upd_ref, sprev_ref,
                  kp_ref, vt_ref, bias_ref):
    b_idx = pl.program_id(0)
    s_idx = pl.program_id(1)
    ts = hq_ref.shape[1]
    n_chunk = ts // CHUNK
    n_blk = ts // BLOCK

    @pl.when((b_idx == 0) & (s_idx == 0))
    def _():
        kj = lax.broadcasted_iota(jnp.int32, (2 * BLOCK, BLOCK), 0)
        qi = lax.broadcasted_iota(jnp.int32, (2 * BLOCK, BLOCK), 1)
        dist = qi + BLOCK - kj
        valid = (dist >= 0) & (dist < WINDOW)
        valid_first = valid & (kj >= BLOCK)
        distf = dist.astype(F32)
        for h in range(B_HEADS):
            slope = 2.0 ** (-(8.0 / B_HEADS) * (h + 1))
            sc = (-slope * LOG2E) * distf
            bias_ref[0, h] = jnp.where(valid, sc, NEG_BIG)
            bias_ref[1, h] = jnp.where(valid_first, sc, NEG_BIG)

    @pl.when(s_idx == 0)
    def _():
        state_ref[...] = jnp.zeros_like(state_ref)
        kp_ref[:, 0] = jnp.zeros((4, BLOCK, LANES), BF16)
        vt_ref[0] = jnp.zeros((LANES, BLOCK), BF16)

    lbl = lbl_ref[...]
    lmax = jnp.max(lbl, axis=0, keepdims=True)
    le = jnp.exp(lbl - lmax)
    lb = le[0:1, :] / jnp.sum(le, axis=0, keepdims=True)

    r = lax.broadcasted_iota(jnp.int32, (CHUNK, CHUNK), 0)
    c = lax.broadcasted_iota(jnp.int32, (CHUNK, CHUNK), 1)
    causal = r >= c
    tri = jnp.where(causal, 1.0, 0.0).astype(BF16)

    def hgrn_prep(ci, carry):
        rows = pl.ds(pl.multiple_of(ci * CHUNK, CHUNK), CHUNK)
        forget = lb + (1.0 - lb) * _sigmoid(hf_ref[0, rows, :])
        key = 1.0 - forget
        logf = jnp.log(forget)
        p0 = logf.astype(BF16)
        r1 = logf - p0.astype(F32)
        p1 = r1.astype(BF16)
        p2 = (r1 - p1.astype(F32)).astype(BF16)
        bc3 = jnp.dot(tri, jnp.concatenate([p0, p1, p2], axis=-1), preferred_element_type=F32)
        bcum = bc3[:, :A_QK] + bc3[:, A_QK:2 * A_QK] + bc3[:, 2 * A_QK:]
        decay = jnp.exp(bcum[CHUNK - 1:CHUNK, :])
        q_dec = _silu(hq_ref[0, rows, :]) * jnp.exp(bcum)
        k_intra = key * jnp.exp(-bcum)
        k_state = (k_intra * decay).astype(BF16)
        vb = hi_ref[0, rows, :].astype(BF16)
        qd_ref[rows, :] = q_dec.astype(BF16)
        ki_ref[rows, :] = k_intra.astype(BF16)
        v_ref[rows, :] = vb
        dec_ref[ci] = decay
        for h in range(A_HEADS):
            cs = slice(h * A_DK, (h + 1) * A_DK)
            upd_ref[ci, h] = lax.dot_general(vb[:, cs], k_state[:, cs], TN_DIMS,
                                             preferred_element_type=F32)
        return carry

    lax.fori_loop(0, n_chunk, hgrn_prep, 0, unroll=2)

    for h in range(A_HEADS):
        cs = slice(h * A_DK, (h + 1) * A_DK)
        st = state_ref[h]
        for ci in range(n_chunk):
            sprev_ref[ci, h] = st.astype(BF16)
            st = st * dec_ref[ci][:, cs] + upd_ref[ci, h]
        state_ref[h] = st

    hng = hng_ref[...]

    def hgrn_out(cp, carry):
        units = []
        for dc in range(HGRN_OUT_CHUNKS):
            ci = cp * HGRN_OUT_CHUNKS + dc
            rows = pl.ds(pl.multiple_of(ci * CHUNK, CHUNK), CHUNK)
            for h in range(A_HEADS):
                units.append((ci, rows, slice(h * A_DK, (h + 1) * A_DK), h))
        qds = [qd_ref[rows, cs] for (_, rows, cs, _) in units]
        a_s = [lax.dot_general(qd, ki_ref[rows, cs], NT_DIMS, preferred_element_type=F32)
               for qd, (_, rows, cs, _) in zip(qds, units)]
        o_inter = [lax.dot_general(qd, sprev_ref[ci, h], NT_DIMS, preferred_element_type=F32)
                   for qd, (ci, _, _, h) in zip(qds, units)]
        a_s = [jnp.where(causal, a, 0.0).astype(BF16) for a in a_s]
        o_s = [jnp.dot(a, v_ref[rows, cs], preferred_element_type=F32) + oi
               for a, oi, (_, rows, cs, _) in zip(a_s, o_inter, units)]
        for o, (_, rows, cs, _) in zip(o_s, units):
            gate = _silu(hg_ref[0, rows, cs])
            o = o * _rms_scale(o)
            y_ref[0, rows, cs] = (o * hng[:, cs] * gate).astype(y_ref.dtype)
        return carry

    lax.fori_loop(0, n_chunk // HGRN_OUT_CHUNKS, hgrn_out, 0)

    lane = lax.broadcasted_iota(jnp.int32, (ts, LANES), 1)
    lo = lane < B_HEAD_DIM
    k_all = bk_ref[0]
    k_rot = pltpu.roll(k_all, B_HEAD_DIM, axis=1)
    k_pad = (jnp.where(lo, k_all, 0.0), jnp.where(lo, 0.0, k_rot),
             jnp.where(lo, k_rot, 0.0), jnp.where(lo, 0.0, k_all))
    for idx in range(4):
        kp_ref[idx, 1:] = k_pad[idx].astype(BF16).reshape(n_blk, BLOCK, LANES)
    for j in range(n_blk):
        vt_ref[j + 1] = bv_ref[0, j * BLOCK:(j + 1) * BLOCK, :].T.astype(BF16)
    ang = ang_ref[...]

    def swa_blocks(jp, carry):
        blocks = [jp * SWA_BLOCKS + dj for dj in range(SWA_BLOCKS)]
        units = [(bi, kvh, par) for bi in range(SWA_BLOCKS)
                 for kvh in range(B_KV_HEADS) for par in range(2)]
        qs = [(bq_ref[0, pl.ds(pl.multiple_of(j * BLOCK, BLOCK), BLOCK), :]
               * (B_HEAD_DIM ** -0.5 * LOG2E)).astype(BF16) for j in blocks]
        s_ts = []
        for bi, kvh, par in units:
            j = blocks[bi]
            q_pairs = jnp.concatenate(
                [qs[bi][:, (2 * kvh) * LANES:(2 * kvh + 1) * LANES],
                 qs[bi][:, (2 * kvh + 1) * LANES:(2 * kvh + 2) * LANES]], axis=0)
            kk = jnp.concatenate([kp_ref[2 * kvh + par, j], kp_ref[2 * kvh + par, j + 1]], axis=0)
            s_ts.append(lax.dot_general(kk, q_pairs, NT_DIMS, preferred_element_type=F32))
        p_ts, inv_ls = [], []
        for (bi, kvh, par), s_t in zip(units, s_ts):
            first = jnp.where((s_idx == 0) & (blocks[bi] == 0), 1, 0)
            p_halves = []
            for half, h in enumerate((4 * kvh + par, 4 * kvh + 2 + par)):
                s = s_t[:, half * BLOCK:(half + 1) * BLOCK] + bias_ref[first, h]
                sink = sinks_ref[h] * LOG2E
                m = jnp.maximum(jnp.max(s, axis=0, keepdims=True), sink)
                p = jnp.exp2(s - m)
                inv_ls.append(1.0 / (jnp.sum(p, axis=0, keepdims=True) + jnp.exp2(sink - m)))
                p_halves.append(p.astype(BF16))
            p_ts.append(jnp.concatenate(p_halves, axis=1))
        o_ts = [[None] * B_HEADS for _ in blocks]
        for ui, ((bi, kvh, par), p_t) in enumerate(zip(units, p_ts)):
            j = blocks[bi]
            vt_h = jnp.concatenate(
                [vt_ref[j, kvh * B_HEAD_DIM:(kvh + 1) * B_HEAD_DIM, :],
                 vt_ref[j + 1, kvh * B_HEAD_DIM:(kvh + 1) * B_HEAD_DIM, :]], axis=1)
            pv = jnp.dot(vt_h, p_t, preferred_element_type=F32)
            for half, h in enumerate((4 * kvh + par, 4 * kvh + 2 + par)):
                o_ts[bi][h] = pv[:, half * BLOCK:(half + 1) * BLOCK] * inv_ls[2 * ui + half]
        for bi, j in enumerate(blocks):
            rows = pl.ds(pl.multiple_of(j * BLOCK, BLOCK), BLOCK)
            o_all_t = jnp.concatenate(o_ts[bi], axis=0)
            scale = lax.rsqrt(jnp.sum(o_all_t * o_all_t, axis=0, keepdims=True) * (1.0 / B_WIDTH) + EPS)
            o_all = (o_all_t * scale).T
            y_ref[0, rows, A_WIDTH:] = (o_all * ang).astype(y_ref.dtype)
        return carry

    lax.fori_loop(0, n_blk // SWA_BLOCKS, swa_blocks, 0)

    kp_ref[:, 0] = kp_ref[:, n_blk]
    vt_ref[0] = vt_ref[n_blk]


def _mixer(proj3d, sinks, lb_logits, hgrn_norm_g, attn_norm_g):
    bsz, seq, _ = proj3d.shape
    ts = TS_MIX
    n_chunk = ts // CHUNK
    n_blk = ts // BLOCK
    wide = lambda col: pl.BlockSpec((1, ts, A_QK), lambda b, s, col=col: (b, s, col))
    kv = lambda col: pl.BlockSpec((1, ts, B_KV), lambda b, s, col=col: (b, s, col))
    kv_col0 = (4 * A_QK + B_WIDTH) // B_KV
    full = lambda shape: pl.BlockSpec(shape, lambda b, s: (0,) * len(shape))
    return pl.pallas_call(
        _mixer_kernel,
        grid=(bsz, seq // ts),
        in_specs=[
            pl.BlockSpec(memory_space=pltpu.SMEM),
            wide(0), wide(1), wide(2), wide(3), wide(4), kv(kv_col0), kv(kv_col0 + 1),
            full((2, A_QK)), full((1, A_WIDTH)), full((1, B_WIDTH)),
        ],
        out_specs=pl.BlockSpec((1, ts, MIX_WIDTH), lambda b, s: (b, s, 0)),
        out_shape=jax.ShapeDtypeStruct((bsz, seq, MIX_WIDTH), BF16),
        scratch_shapes=[
            pltpu.VMEM((A_HEADS, A_DV, A_DK), F32),
            pltpu.VMEM((ts, A_QK), BF16),
            pltpu.VMEM((ts, A_QK), BF16),
            pltpu.VMEM((ts, A_WIDTH), BF16),
            pltpu.VMEM((n_chunk, 1, A_QK), F32),
            pltpu.VMEM((n_chunk, A_HEADS, A_DV, A_DK), F32),
            pltpu.VMEM((n_chunk, A_HEADS, A_DV, A_DK), BF16),
            pltpu.VMEM((4, n_blk + 1, BLOCK, LANES), BF16),
            pltpu.VMEM((n_blk + 1, LANES, BLOCK), BF16),
            pltpu.VMEM((2, B_HEADS, 2 * BLOCK, BLOCK), F32),
        ],
        compiler_params=pltpu.CompilerParams(
            dimension_semantics=("arbitrary", "arbitrary"),
            vmem_limit_bytes=V7X_VMEM_LIMIT_BYTES),
        name="mixer",
    )(sinks, proj3d, proj3d, proj3d, proj3d, proj3d, proj3d, proj3d,
      lb_logits, hgrn_norm_g, attn_norm_g)


def _ffn_kernel(x_ref, y_ref, wo_ref, g2_ref, wg_ref, wu_ref, wd_ref, gf_ref, o_ref, acc_ref):
    h = x_ref[...] + jnp.dot(y_ref[...], wo_ref[...], preferred_element_type=F32)
    u = (h * _rms_scale(h) * g2_ref[...]).astype(BF16)
    acc_ref[...] = h
    for c0 in range(0, D_FF, FF_CHUNK):
        cs = slice(c0, c0 + FF_CHUNK)
        gt = jnp.dot(u, wg_ref[:, cs], preferred_element_type=F32)
        up = jnp.dot(u, wu_ref[:, cs], preferred_element_type=F32)
        act = (_silu(gt) * up).astype(BF16)
        acc_ref[...] += jnp.dot(act, wd_ref[cs, :], preferred_element_type=F32)
    h2 = acc_ref[...]
    o_ref[...] = h2 * _rms_scale(h2) * gf_ref[...]


def _ffn(x2d, y2d, w_out, ln2_g, w_gate, w_up, w_down, final_g):
    t = x2d.shape[0]
    tm = TM_FFN
    const = lambda shape: pl.BlockSpec(shape, lambda i: (0, 0), pipeline_mode=pl.Buffered(1))
    return pl.pallas_call(
        _ffn_kernel,
        grid=(t // tm,),
        in_specs=[
            pl.BlockSpec((tm, D_MODEL), lambda i: (i, 0)),
            pl.BlockSpec((tm, MIX_WIDTH), lambda i: (i, 0)),
            const((MIX_WIDTH, D_MODEL)),
            const((1, D_MODEL)),
            const((D_MODEL, D_FF)),
            const((D_MODEL, D_FF)),
            const((D_FF, D_MODEL)),
            const((1, D_MODEL)),
        ],
        out_specs=pl.BlockSpec((tm, D_MODEL), lambda i: (i, 0)),
        out_shape=jax.ShapeDtypeStruct((t, D_MODEL), F32),
        scratch_shapes=[pltpu.VMEM((tm, D_MODEL), F32)],
        compiler_params=pltpu.CompilerParams(
            dimension_semantics=("arbitrary",), vmem_limit_bytes=V7X_VMEM_LIMIT_BYTES),
        name="ffn",
    )(x2d, y2d, w_out, ln2_g, w_gate, w_up, w_down, final_g)


def kernel(x, ln1_g, w_in, lb_logits, hgrn_norm_g, attn_sinks, attn_norm_g, w_out, ln2_g,
           w_gate, w_up, w_down, final_g):
    bsz, seq, d = x.shape
    x2d = x.reshape(bsz * seq, d)
    proj = _in_proj(x2d, ln1_g[0:1], w_in[0].astype(BF16))
    y = _mixer(proj.reshape(bsz, seq, IN_COLS), attn_sinks[0], lb_logits,
               hgrn_norm_g[0:1], attn_norm_g[0:1])
    out = _ffn(x2d, y.reshape(bsz * seq, MIX_WIDTH), w_out[0].astype(BF16), ln2_g[0:1],
               w_gate[0].astype(BF16), w_up[0].astype(BF16), w_down[0].astype(BF16),
               final_g.reshape(1, d))
    return out.reshape(bsz, seq, d)
```

```python
import jax
import jax.numpy as jnp
from jax import lax
from jax.experimental import pallas as pl
from jax.experimental.pallas import tpu as pltpu

D_MODEL = 1024
A_HEADS = 4
A_DK = 128
A_DV = 128
A_QK = A_HEADS * A_DK
A_WIDTH = A_HEADS * A_DV
CHUNK = 64
B_HEADS = 8
B_KV_HEADS = 2
B_HEAD_DIM = 64
B_GROUP = B_HEADS // B_KV_HEADS
B_WIDTH = B_HEADS * B_HEAD_DIM
B_KV = B_KV_HEADS * B_HEAD_DIM
WINDOW = 128
BLOCK = 128
MIX_WIDTH = A_WIDTH + B_WIDTH
D_FF = 2816
IN_COLS = 4 * A_QK + B_WIDTH + 2 * B_KV
EPS = 1e-6
NEG_BIG = -1e30
LANES = 128

F32 = jnp.float32
BF16 = jnp.bfloat16

V7X_VMEM_LIMIT_BYTES = 56 * 1024 * 1024

TM_PROJ = 512
TS_MIX = 512
TM_FFN = 512
FF_CHUNK = 256
HGRN_STAGE_CHUNKS = 4
LOG2E = 1.4426950408889634

NT_DIMS = (((1,), (1,)), ((), ()))
TN_DIMS = (((0,), (0,)), ((), ()))


def _rms_scale(v):
    return lax.rsqrt(jnp.mean(v * v, axis=-1, keepdims=True) + EPS)


def _silu(v):
    return v * (1.0 / (1.0 + jnp.exp(-v)))


def _sigmoid(v):
    return 1.0 / (1.0 + jnp.exp(-v))


def _in_proj_kernel(x_ref, g_ref, w_ref, lbl_ref, hng_ref,
                    qd_ref, ki_ref, ks_ref, hv_ref, gate_ref, dec_ref, bq_ref, kp_ref, vt_ref):
    tm = x_ref.shape[0]
    x = x_ref[...]
    u = (x * _rms_scale(x) * g_ref[...]).astype(BF16)

    lbl = lbl_ref[...]
    lmax = jnp.max(lbl, axis=0, keepdims=True)
    le = jnp.exp(lbl - lmax)
    lb = le[0:1, :] / jnp.sum(le, axis=0, keepdims=True)

    r = lax.broadcasted_iota(jnp.int32, (CHUNK, CHUNK), 0)
    c = lax.broadcasted_iota(jnp.int32, (CHUNK, CHUNK), 1)
    tri = jnp.where(r >= c, 1.0, 0.0).astype(BF16)

    pf = jnp.dot(u, w_ref[:, A_QK:2 * A_QK], preferred_element_type=F32)
    pq = jnp.dot(u, w_ref[:, 0:A_QK], preferred_element_type=F32)
    for ci in range(tm // CHUNK):
        rows = slice(ci * CHUNK, (ci + 1) * CHUNK)
        forget = lb + (1.0 - lb) * _sigmoid(pf[rows])
        key = 1.0 - forget
        logf = jnp.log(forget)
        p0 = logf.astype(BF16)
        r1 = logf - p0.astype(F32)
        p1 = r1.astype(BF16)
        p2 = (r1 - p1.astype(F32)).astype(BF16)
        bc3 = jnp.dot(tri, jnp.concatenate([p0, p1, p2], axis=-1), preferred_element_type=F32)
        bcum = bc3[:, :A_QK] + bc3[:, A_QK:2 * A_QK] + bc3[:, 2 * A_QK:]
        decay = jnp.exp(bcum[CHUNK - 1:CHUNK, :])
        k_intra = key * jnp.exp(-bcum)
        qd_ref[rows, :] = (_silu(pq[rows]) * jnp.exp(bcum)).astype(BF16)
        ki_ref[rows, :] = k_intra.astype(BF16)
        ks_ref[rows, :] = (k_intra * decay).astype(BF16)
        dec_ref[ci:ci + 1, :] = decay

    pi = jnp.dot(u, w_ref[:, 2 * A_QK:2 * A_QK + A_WIDTH], preferred_element_type=F32)
    hv_ref[...] = pi.astype(BF16)
    pg = jnp.dot(u, w_ref[:, 2 * A_QK + A_WIDTH:2 * A_QK + 2 * A_WIDTH], preferred_element_type=F32)
    gate_ref[...] = (_silu(pg) * hng_ref[...]).astype(BF16)

    c0 = 2 * A_QK + 2 * A_WIDTH
    pa = jnp.dot(u, w_ref[:, c0:], preferred_element_type=F32)
    bq_ref[...] = (pa[:, :B_WIDTH] * (B_HEAD_DIM ** -0.5 * LOG2E)).astype(BF16)
    lo = lax.broadcasted_iota(jnp.int32, (tm, LANES), 1) < B_HEAD_DIM
    k_all = pa[:, B_WIDTH:B_WIDTH + B_KV]
    k_rot = pltpu.roll(k_all, B_HEAD_DIM, axis=1)
    kp_ref[0] = jnp.where(lo, k_all, 0.0).astype(BF16)
    kp_ref[1] = jnp.where(lo, 0.0, k_rot).astype(BF16)
    kp_ref[2] = jnp.where(lo, k_rot, 0.0).astype(BF16)
    kp_ref[3] = jnp.where(lo, 0.0, k_all).astype(BF16)
    vt_ref[...] = pa[:, B_WIDTH + B_KV:].T.astype(BF16)


def _in_proj(x2d, ln1_g, w_in_bf16, lb_logits, hgrn_norm_g):
    t = x2d.shape[0]
    tm = TM_PROJ
    row_blk = lambda width: pl.BlockSpec((tm, width), lambda i: (i, 0))
    const = lambda shape, **kw: pl.BlockSpec(shape, lambda i: (0, 0), **kw)
    bf = lambda shape: jax.ShapeDtypeStruct(shape, BF16)
    return pl.pallas_call(
        _in_proj_kernel,
        grid=(t // tm,),
        in_specs=[
            row_blk(D_MODEL),
            const((1, D_MODEL)),
            const((D_MODEL, IN_COLS), pipeline_mode=pl.Buffered(1)),
            const((2, A_QK)),
            const((1, A_WIDTH)),
        ],
        out_specs=[
            row_blk(A_QK), row_blk(A_QK), row_blk(A_QK), row_blk(A_WIDTH), row_blk(A_WIDTH),
            pl.BlockSpec((tm // CHUNK, A_QK), lambda i: (i, 0)),
            row_blk(B_WIDTH),
            pl.BlockSpec((4, tm, LANES), lambda i: (0, i, 0)),
            pl.BlockSpec((LANES, tm), lambda i: (0, i)),
        ],
        out_shape=[
            bf((t, A_QK)), bf((t, A_QK)), bf((t, A_QK)), bf((t, A_WIDTH)), bf((t, A_WIDTH)),
            jax.ShapeDtypeStruct((t // CHUNK, A_QK), F32),
            bf((t, B_WIDTH)),
            bf((4, t, LANES)),
            bf((LANES, t)),
        ],
        compiler_params=pltpu.CompilerParams(
            dimension_semantics=("arbitrary",), vmem_limit_bytes=V7X_VMEM_LIMIT_BYTES),
        name="in_proj",
    )(x2d, ln1_g, w_in_bf16, lb_logits, hgrn_norm_g)


def _mixer_kernel(sinks_ref, qd_ref, ki_ref, ks_ref, hv_ref, gate_ref, dec_ref, bq_ref,
                  kp_ref, kph_ref, vt_ref, vth_ref, ang_ref, y_ref,
                  state_ref, upd_ref, sprev_ref, bias_ref):
    b_idx = pl.program_id(0)
    s_idx = pl.program_id(1)
    ts = qd_ref.shape[0]
    n_chunk = ts // CHUNK
    n_blk = ts // BLOCK

    @pl.when((b_idx == 0) & (s_idx == 0))
    def _():
        kj = lax.broadcasted_iota(jnp.int32, (2 * BLOCK, BLOCK), 0)
        qi = lax.broadcasted_iota(jnp.int32, (2 * BLOCK, BLOCK), 1)
        dist = qi + BLOCK - kj
        valid = (dist >= 0) & (dist < WINDOW)
        valid_first = valid & (kj >= BLOCK)
        distf = dist.astype(F32)
        for h in range(B_HEADS):
            slope = 2.0 ** (-(8.0 / B_HEADS) * (h + 1))
            sc = (-slope * LOG2E) * distf
            bias_ref[0, h] = jnp.where(valid, sc, NEG_BIG)
            bias_ref[1, h] = jnp.where(valid_first, sc, NEG_BIG)

    @pl.when(s_idx == 0)
    def _():
        state_ref[...] = jnp.zeros_like(state_ref)

    r = lax.broadcasted_iota(jnp.int32, (CHUNK, CHUNK), 0)
    c = lax.broadcasted_iota(jnp.int32, (CHUNK, CHUNK), 1)
    causal = r >= c

    def chunk_units(cp):
        units = []
        for dc in range(HGRN_STAGE_CHUNKS):
            ci = cp * HGRN_STAGE_CHUNKS + dc
            rows = pl.ds(pl.multiple_of(ci * CHUNK, CHUNK), CHUNK)
            for h in range(A_HEADS):
                units.append((ci, rows, slice(h * A_DK, (h + 1) * A_DK), h))
        return units

    def hgrn_upd(cp, carry):
        units = chunk_units(cp)
        upds = [lax.dot_general(hv_ref[rows, cs], ks_ref[rows, cs], TN_DIMS,
                                preferred_element_type=F32) for (_, rows, cs, _) in units]
        for upd, (ci, _, _, h) in zip(upds, units):
            upd_ref[ci, h] = upd
        return carry

    lax.fori_loop(0, n_chunk // HGRN_STAGE_CHUNKS, hgrn_upd, 0)

    for h in range(A_HEADS):
        cs = slice(h * A_DK, (h + 1) * A_DK)
        st = state_ref[h]
        for ci in range(n_chunk):
            sprev_ref[ci, h] = st.astype(BF16)
            st = st * dec_ref[ci:ci + 1, cs] + upd_ref[ci, h]
        state_ref[h] = st

    def hgrn_out(cp, carry):
        units = chunk_units(cp)
        qds = [qd_ref[rows, cs] for (_, rows, cs, _) in units]
        a_s = [lax.dot_general(qd, ki_ref[rows, cs], NT_DIMS, preferred_element_type=F32)
               for qd, (_, rows, cs, _) in zip(qds, units)]
        o_inter = [lax.dot_general(qd, sprev_ref[ci, h], NT_DIMS, preferred_element_type=F32)
                   for qd, (ci, _, _, h) in zip(qds, units)]
        a_s = [jnp.where(causal, a, 0.0).astype(BF16) for a in a_s]
        o_s = [jnp.dot(a, hv_ref[rows, cs], preferred_element_type=F32) + oi
               for a, oi, (_, rows, cs, _) in zip(a_s, o_inter, units)]
        for o, (_, rows, cs, _) in zip(o_s, units):
            gate = gate_ref[rows, cs].astype(F32)
            y_ref[rows, cs] = (o * _rms_scale(o) * gate).astype(y_ref.dtype)
        return carry

    lax.fori_loop(0, n_chunk // HGRN_STAGE_CHUNKS, hgrn_out, 0)

    ang = ang_ref[...]
    units = [(j, kvh, par) for j in range(n_blk)
             for kvh in range(B_KV_HEADS) for par in range(2)]
    s_ts = []
    for j, kvh, par in units:
        rows = slice(j * BLOCK, (j + 1) * BLOCK)
        q_pairs = jnp.concatenate(
            [bq_ref[rows, (2 * kvh) * LANES:(2 * kvh + 1) * LANES],
             bq_ref[rows, (2 * kvh + 1) * LANES:(2 * kvh + 2) * LANES]], axis=0)
        idx = 2 * kvh + par
        if j == 0:
            kk = jnp.concatenate([kph_ref[idx], kp_ref[idx, 0:BLOCK, :]], axis=0)
        else:
            kk = kp_ref[idx, (j - 1) * BLOCK:(j + 1) * BLOCK, :]
        s_ts.append(lax.dot_general(kk, q_pairs, NT_DIMS, preferred_element_type=F32))
    first = jnp.where(s_idx == 0, 1, 0)
    p_ts, inv_ls = [], []
    for (j, kvh, par), s_t in zip(units, s_ts):
        p_halves = []
        for half, h in enumerate((4 * kvh + par, 4 * kvh + 2 + par)):
            bias = bias_ref[first, h] if j == 0 else bias_ref[0, h]
            s = s_t[:, half * BLOCK:(half + 1) * BLOCK] + bias
            sink = sinks_ref[h] * LOG2E
            m = jnp.maximum(jnp.max(s, axis=0, keepdims=True), sink)
            p = jnp.exp2(s - m)
            inv_ls.append(1.0 / (jnp.sum(p, axis=0, keepdims=True) + jnp.exp2(sink - m)))
            p_halves.append(p.astype(BF16))
        p_ts.append(jnp.concatenate(p_halves, axis=1))
    o_ts = [[None] * B_HEADS for _ in range(n_blk)]
    for ui, ((j, kvh, par), p_t) in enumerate(zip(units, p_ts)):
        drow = slice(kvh * B_HEAD_DIM, (kvh + 1) * B_HEAD_DIM)
        if j == 0:
            vt_h = jnp.concatenate([vth_ref[drow, :], vt_ref[drow, 0:BLOCK]], axis=1)
        else:
            vt_h = vt_ref[drow, (j - 1) * BLOCK:(j + 1) * BLOCK]
        pv = jnp.dot(vt_h, p_t, preferred_element_type=F32)
        for half, h in enumerate((4 * kvh + par, 4 * kvh + 2 + par)):
            o_ts[j][h] = pv[:, half * BLOCK:(half + 1) * BLOCK] * inv_ls[2 * ui + half]
    for j in range(n_blk):
        o_all_t = jnp.concatenate(o_ts[j], axis=0)
        scale = lax.rsqrt(jnp.sum(o_all_t * o_all_t, axis=0, keepdims=True) * (1.0 / B_WIDTH) + EPS)
        o_all = (o_all_t * scale).T
        y_ref[j * BLOCK:(j + 1) * BLOCK, A_WIDTH:] = (o_all * ang).astype(y_ref.dtype)


def _mixer(prep, sinks, attn_norm_g, bsz, seq):
    qd, ki, ks, hv, gate, dec, bq, kp, vt = prep
    ts = TS_MIX
    n_s = seq // ts
    n_chunk = ts // CHUNK
    n_blk = ts // BLOCK
    row_blk = lambda width: pl.BlockSpec((ts, width), lambda b, s: (b * n_s + s, 0))
    halo = lambda b, s: b * n_s * n_blk + jnp.maximum(s * n_blk - 1, 0)
    return pl.pallas_call(
        _mixer_kernel,
        grid=(bsz, n_s),
        in_specs=[
            pl.BlockSpec(memory_space=pltpu.SMEM),
            row_blk(A_QK), row_blk(A_QK), row_blk(A_QK), row_blk(A_WIDTH), row_blk(A_WIDTH),
            pl.BlockSpec((n_chunk, A_QK), lambda b, s: (b * n_s + s, 0)),
            row_blk(B_WIDTH),
            pl.BlockSpec((4, ts, LANES), lambda b, s: (0, b * n_s + s, 0)),
            pl.BlockSpec((4, BLOCK, LANES), lambda b, s: (0, halo(b, s), 0)),
            pl.BlockSpec((LANES, ts), lambda b, s: (0, b * n_s + s)),
            pl.BlockSpec((LANES, BLOCK), lambda b, s: (0, halo(b, s))),
            pl.BlockSpec((1, B_WIDTH), lambda b, s: (0, 0)),
        ],
        out_specs=pl.BlockSpec((ts, MIX_WIDTH), lambda b, s: (b * n_s + s, 0)),
        out_shape=jax.ShapeDtypeStruct((bsz * seq, MIX_WIDTH), BF16),
        scratch_shapes=[
            pltpu.VMEM((A_HEADS, A_DV, A_DK), F32),
            pltpu.VMEM((n_chunk, A_HEADS, A_DV, A_DK), F32),
            pltpu.VMEM((n_chunk, A_HEADS, A_DV, A_DK), BF16),
            pltpu.VMEM((2, B_HEADS, 2 * BLOCK, BLOCK), F32),
        ],
        compiler_params=pltpu.CompilerParams(
            dimension_semantics=("arbitrary", "arbitrary"),
            vmem_limit_bytes=V7X_VMEM_LIMIT_BYTES),
        name="mixer",
    )(sinks, qd, ki, ks, hv, gate, dec, bq, kp, kp, vt, vt, attn_norm_g)


def _ffn_kernel(x_ref, y_ref, wo_ref, g2_ref, wg_ref, wu_ref, wd_ref, gf_ref, o_ref, acc_ref):
    h = x_ref[...] + jnp.dot(y_ref[...], wo_ref[...], preferred_element_type=F32)
    u = (h * _rms_scale(h) * g2_ref[...]).astype(BF16)
    acc_ref[...] = h
    for c0 in range(0, D_FF, FF_CHUNK):
        cs = slice(c0, c0 + FF_CHUNK)
        gt = jnp.dot(u, wg_ref[:, cs], preferred_element_type=F32)
        up = jnp.dot(u, wu_ref[:, cs], preferred_element_type=F32)
        act = (_silu(gt) * up).astype(BF16)
        acc_ref[...] += jnp.dot(act, wd_ref[cs, :], preferred_element_type=F32)
    h2 = acc_ref[...]
    o_ref[...] = h2 * _rms_scale(h2) * gf_ref[...]


def _ffn(x2d, y2d, w_out, ln2_g, w_gate, w_up, w_down, final_g):
    t = x2d.shape[0]
    tm = TM_FFN
    const = lambda shape: pl.BlockSpec(shape, lambda i: (0, 0), pipeline_mode=pl.Buffered(1))
    return pl.pallas_call(
        _ffn_kernel,
        grid=(t // tm,),
        in_specs=[
            pl.BlockSpec((tm, D_MODEL), lambda i: (i, 0)),
            pl.BlockSpec((tm, MIX_WIDTH), lambda i: (i, 0)),
            const((MIX_WIDTH, D_MODEL)),
            const((1, D_MODEL)),
            const((D_MODEL, D_FF)),
            const((D_MODEL, D_FF)),
            const((D_FF, D_MODEL)),
            const((1, D_MODEL)),
        ],
        out_specs=pl.BlockSpec((tm, D_MODEL), lambda i: (i, 0)),
        out_shape=jax.ShapeDtypeStruct((t, D_MODEL), F32),
        scratch_shapes=[pltpu.VMEM((tm, D_MODEL), F32)],
        compiler_params=pltpu.CompilerParams(
            dimension_semantics=("arbitrary",), vmem_limit_bytes=V7X_VMEM_LIMIT_BYTES),
        name="ffn",
    )(x2d, y2d, w_out, ln2_g, w_gate, w_up, w_down, final_g)


def kernel(x, ln1_g, w_in, lb_logits, hgrn_norm_g, attn_sinks, attn_norm_g, w_out, ln2_g,
           w_gate, w_up, w_down, final_g):
    bsz, seq, d = x.shape
    x2d = x.reshape(bsz * seq, d)
    prep = _in_proj(x2d, ln1_g[0:1], w_in[0].astype(BF16), lb_logits, hgrn_norm_g[0:1])
    y = _mixer(prep, attn_sinks[0], attn_norm_g[0:1], bsz, seq)
    out = _ffn(x2d, y, w_out[0].astype(BF16), ln2_g[0:1],
               w_gate[0].astype(BF16), w_up[0].astype(BF16), w_down[0].astype(BF16),
               final_g.reshape(1, d))
    return out.reshape(bsz, seq, d)
```

```python
import functools

import jax
import jax.numpy as jnp
from jax import lax
from jax.experimental import pallas as pl
from jax.experimental.pallas import tpu as pltpu

D_MODEL = 1024
A_HEADS = 4
A_DK = 128
A_DV = 128
A_QK = A_HEADS * A_DK
A_WIDTH = A_HEADS * A_DV
CHUNK = 64
B_HEADS = 8
B_KV_HEADS = 2
B_HEAD_DIM = 64
B_GROUP = B_HEADS // B_KV_HEADS
B_WIDTH = B_HEADS * B_HEAD_DIM
B_KV = B_KV_HEADS * B_HEAD_DIM
WINDOW = 128
BLOCK = 128
MIX_WIDTH = A_WIDTH + B_WIDTH
D_FF = 2816
IN_COLS = 4 * A_QK + B_WIDTH + 2 * B_KV
EPS = 1e-6
NEG_BIG = -1e30
LANES = 128

F32 = jnp.float32
BF16 = jnp.bfloat16

V7X_VMEM_LIMIT_BYTES = 56 * 1024 * 1024

TM_PROJ = 512
TS_MIX = 512
FF_CHUNK = 256
LOG2E = 1.4426950408889634

NT_DIMS = (((1,), (1,)), ((), ()))
TN_DIMS = (((0,), (0,)), ((), ()))


def _rms_scale(v):
    return lax.rsqrt(jnp.mean(v * v, axis=-1, keepdims=True) + EPS)


def _silu(v):
    return v * (1.0 / (1.0 + jnp.exp(-v)))


def _sigmoid(v):
    return 1.0 / (1.0 + jnp.exp(-v))


def _in_proj_kernel(x_ref, g_ref, w_ref, lbl_ref, hng_ref,
                    qd_ref, ki_ref, ks_ref, hv_ref, gate_ref, dec_ref, bq_ref, kp_ref, vt_ref):
    tm = x_ref.shape[0]
    x = x_ref[...]
    u = (x * _rms_scale(x) * g_ref[...]).astype(BF16)

    lbl = lbl_ref[...]
    lmax = jnp.max(lbl, axis=0, keepdims=True)
    le = jnp.exp(lbl - lmax)
    lb = le[0:1, :] / jnp.sum(le, axis=0, keepdims=True)

    r = lax.broadcasted_iota(jnp.int32, (CHUNK, CHUNK), 0)
    c = lax.broadcasted_iota(jnp.int32, (CHUNK, CHUNK), 1)
    tri = jnp.where(r >= c, 1.0, 0.0).astype(BF16)

    pf = jnp.dot(u, w_ref[:, A_QK:2 * A_QK], preferred_element_type=F32)
    pq = jnp.dot(u, w_ref[:, 0:A_QK], preferred_element_type=F32)
    for ci in range(tm // CHUNK):
        rows = slice(ci * CHUNK, (ci + 1) * CHUNK)
        forget = lb + (1.0 - lb) * _sigmoid(pf[rows])
        key = 1.0 - forget
        logf = jnp.log(forget)
        p0 = logf.astype(BF16)
        r1 = logf - p0.astype(F32)
        p1 = r1.astype(BF16)
        p2 = (r1 - p1.astype(F32)).astype(BF16)
        bc3 = jnp.dot(tri, jnp.concatenate([p0, p1, p2], axis=-1), preferred_element_type=F32)
        bcum = bc3[:, :A_QK] + bc3[:, A_QK:2 * A_QK] + bc3[:, 2 * A_QK:]
        decay = jnp.exp(bcum[CHUNK - 1:CHUNK, :])
        k_intra = key * jnp.exp(-bcum)
        qd_ref[rows, :] = (_silu(pq[rows]) * jnp.exp(bcum)).astype(BF16)
        ki_ref[rows, :] = k_intra.astype(BF16)
        ks_ref[rows, :] = (k_intra * decay).astype(BF16)
        dec_ref[ci:ci + 1, :] = decay

    pi = jnp.dot(u, w_ref[:, 2 * A_QK:2 * A_QK + A_WIDTH], preferred_element_type=F32)
    hv_ref[...] = pi.astype(BF16)
    pg = jnp.dot(u, w_ref[:, 2 * A_QK + A_WIDTH:2 * A_QK + 2 * A_WIDTH], preferred_element_type=F32)
    gate_ref[...] = (_silu(pg) * hng_ref[...]).astype(BF16)

    c0 = 2 * A_QK + 2 * A_WIDTH
    pa = jnp.dot(u, w_ref[:, c0:], preferred_element_type=F32)
    bq_ref[...] = (pa[:, :B_WIDTH] * (B_HEAD_DIM ** -0.5 * LOG2E)).astype(BF16)
    lo = lax.broadcasted_iota(jnp.int32, (tm, LANES), 1) < B_HEAD_DIM
    k_all = pa[:, B_WIDTH:B_WIDTH + B_KV]
    k_rot = pltpu.roll(k_all, B_HEAD_DIM, axis=1)
    kp_ref[0] = jnp.where(lo, k_all, 0.0).astype(BF16)
    kp_ref[1] = jnp.where(lo, 0.0, k_rot).astype(BF16)
    kp_ref[2] = jnp.where(lo, k_rot, 0.0).astype(BF16)
    kp_ref[3] = jnp.where(lo, 0.0, k_all).astype(BF16)
    vt_ref[...] = pa[:, B_WIDTH + B_KV:].T.astype(BF16)


def _in_proj(x2d, ln1_g, w_in_bf16, lb_logits, hgrn_norm_g):
    t = x2d.shape[0]
    tm = TM_PROJ
    row_blk = lambda width: pl.BlockSpec((tm, width), lambda i: (i, 0))
    const = lambda shape, **kw: pl.BlockSpec(shape, lambda i: (0, 0), **kw)
    bf = lambda shape: jax.ShapeDtypeStruct(shape, BF16)
    return pl.pallas_call(
        _in_proj_kernel,
        grid=(t // tm,),
        in_specs=[
            row_blk(D_MODEL),
            const((1, D_MODEL)),
            const((D_MODEL, IN_COLS), pipeline_mode=pl.Buffered(1)),
            const((2, A_QK)),
            const((1, A_WIDTH)),
        ],
        out_specs=[
            row_blk(A_QK), row_blk(A_QK), row_blk(A_QK), row_blk(A_WIDTH), row_blk(A_WIDTH),
            pl.BlockSpec((tm // CHUNK, A_QK), lambda i: (i, 0)),
            row_blk(B_WIDTH),
            pl.BlockSpec((4, tm, LANES), lambda i: (0, i, 0)),
            pl.BlockSpec((LANES, tm), lambda i: (0, i)),
        ],
        out_shape=[
            bf((t, A_QK)), bf((t, A_QK)), bf((t, A_QK)), bf((t, A_WIDTH)), bf((t, A_WIDTH)),
            jax.ShapeDtypeStruct((t // CHUNK, A_QK), F32),
            bf((t, B_WIDTH)),
            bf((4, t, LANES)),
            bf((LANES, t)),
        ],
        compiler_params=pltpu.CompilerParams(
            dimension_semantics=("arbitrary",), vmem_limit_bytes=V7X_VMEM_LIMIT_BYTES),
        name="in_proj",
    )(x2d, ln1_g, w_in_bf16, lb_logits, hgrn_norm_g)


def _bias_table_init(bias_ref):
    kj = lax.broadcasted_iota(jnp.int32, (2 * BLOCK, BLOCK), 0)
    qi = lax.broadcasted_iota(jnp.int32, (2 * BLOCK, BLOCK), 1)
    dist = qi + BLOCK - kj
    valid = (dist >= 0) & (dist < WINDOW)
    valid_first = valid & (kj >= BLOCK)
    distf = dist.astype(F32)
    for h in range(B_HEADS):
        slope = 2.0 ** (-(8.0 / B_HEADS) * (h + 1))
        sc = (-slope * LOG2E) * distf
        bias_ref[0, h] = jnp.where(valid, sc, NEG_BIG)
        bias_ref[1, h] = jnp.where(valid_first, sc, NEG_BIG)


def _mixer_stages(first, sinks_ref, qd_ref, ki_ref, ks_ref, hv_ref, gate_ref, dec_ref, bq_ref,
                  kp_ref, kph_ref, vt_ref, vth_ref, ang_ref, y_ref,
                  state_ref, upd_ref, sprev_ref, bias_ref):
    ts = qd_ref.shape[0]
    n_chunk = ts // CHUNK
    n_blk = ts // BLOCK
    head_cols = [slice(h * A_DK, (h + 1) * A_DK) for h in range(A_HEADS)]

    def chunk_units(c_lo, c_hi):
        return [(ci, slice(ci * CHUNK, (ci + 1) * CHUNK), head_cols[h], h)
                for ci in range(c_lo, c_hi) for h in range(A_HEADS)]

    def hgrn_updates():
        units = chunk_units(0, n_chunk)
        upds = [lax.dot_general(hv_ref[rows, cs], ks_ref[rows, cs], TN_DIMS,
                                preferred_element_type=F32) for (_, rows, cs, _) in units]
        for upd, (ci, _, _, h) in zip(upds, units):
            upd_ref[ci, h] = upd

    def hgrn_scan():
        for h in range(A_HEADS):
            st = state_ref[h]
            for ci in range(n_chunk):
                sprev_ref[ci, h] = st.astype(BF16)
                st = st * dec_ref[ci:ci + 1, head_cols[h]] + upd_ref[ci, h]
            state_ref[h] = st

    def hgrn_outputs(c_lo, c_hi):
        r = lax.broadcasted_iota(jnp.int32, (CHUNK, CHUNK), 0)
        c = lax.broadcasted_iota(jnp.int32, (CHUNK, CHUNK), 1)
        causal = r >= c
        units = chunk_units(c_lo, c_hi)
        qds = [qd_ref[rows, cs] for (_, rows, cs, _) in units]
        a_s = [lax.dot_general(qd, ki_ref[rows, cs], NT_DIMS, preferred_element_type=F32)
               for qd, (_, rows, cs, _) in zip(qds, units)]
        o_inter = [lax.dot_general(qd, sprev_ref[ci, h], NT_DIMS, preferred_element_type=F32)
                   for qd, (ci, _, _, h) in zip(qds, units)]
        a_s = [jnp.where(causal, a, 0.0).astype(BF16) for a in a_s]
        o_s = [jnp.dot(a, hv_ref[rows, cs], preferred_element_type=F32) + oi
               for a, oi, (_, rows, cs, _) in zip(a_s, o_inter, units)]
        for o, (_, rows, cs, _) in zip(o_s, units):
            gate = gate_ref[rows, cs].astype(F32)
            y_ref[rows, cs] = (o * _rms_scale(o) * gate).astype(y_ref.dtype)

    def swa_blocks(j_lo, j_hi):
        units = [(j, kvh, par) for j in range(j_lo, j_hi)
                 for kvh in range(B_KV_HEADS) for par in range(2)]
        s_ts = []
        for j, kvh, par in units:
            rows = slice(j * BLOCK, (j + 1) * BLOCK)
            q_pairs = jnp.concatenate(
                [bq_ref[rows, (2 * kvh) * LANES:(2 * kvh + 1) * LANES],
                 bq_ref[rows, (2 * kvh + 1) * LANES:(2 * kvh + 2) * LANES]], axis=0)
            idx = 2 * kvh + par
            if j == 0:
                kk = jnp.concatenate([kph_ref[idx], kp_ref[idx, 0:BLOCK, :]], axis=0)
            else:
                kk = kp_ref[idx, (j - 1) * BLOCK:(j + 1) * BLOCK, :]
            s_ts.append(lax.dot_general(kk, q_pairs, NT_DIMS, preferred_element_type=F32))
        p_ts, inv_ls = [], []
        for (j, kvh, par), s_t in zip(units, s_ts):
            p_halves = []
            for half, h in enumerate((4 * kvh + par, 4 * kvh + 2 + par)):
                bias = bias_ref[first, h] if j == 0 else bias_ref[0, h]
                s = s_t[:, half * BLOCK:(half + 1) * BLOCK] + bias
                sink = sinks_ref[h] * LOG2E
                m = jnp.maximum(jnp.max(s, axis=0, keepdims=True), sink)
                p = jnp.exp2(s - m)
                inv_ls.append(1.0 / (jnp.sum(p, axis=0, keepdims=True) + jnp.exp2(sink - m)))
                p_halves.append(p.astype(BF16))
            p_ts.append(jnp.concatenate(p_halves, axis=1))
        o_ts = {j: [None] * B_HEADS for j in range(j_lo, j_hi)}
        for ui, ((j, kvh, par), p_t) in enumerate(zip(units, p_ts)):
            drow = slice(kvh * B_HEAD_DIM, (kvh + 1) * B_HEAD_DIM)
            if j == 0:
                vt_h = jnp.concatenate([vth_ref[drow, :], vt_ref[drow, 0:BLOCK]], axis=1)
            else:
                vt_h = vt_ref[drow, (j - 1) * BLOCK:(j + 1) * BLOCK]
            pv = jnp.dot(vt_h, p_t, preferred_element_type=F32)
            for half, h in enumerate((4 * kvh + par, 4 * kvh + 2 + par)):
                o_ts[j][h] = pv[:, half * BLOCK:(half + 1) * BLOCK] * inv_ls[2 * ui + half]
        for j in range(j_lo, j_hi):
            o_all_t = jnp.concatenate(o_ts[j], axis=0)
            scale = lax.rsqrt(jnp.sum(o_all_t * o_all_t, axis=0, keepdims=True) * (1.0 / B_WIDTH) + EPS)
            o_all = (o_all_t * scale).T
            y_ref[j * BLOCK:(j + 1) * BLOCK, A_WIDTH:] = (o_all * ang_ref[...]).astype(y_ref.dtype)

    half_c = n_chunk // 2
    stages = [hgrn_updates, hgrn_scan,
              functools.partial(hgrn_outputs, 0, half_c),
              functools.partial(hgrn_outputs, half_c, n_chunk)]
    stages += [functools.partial(swa_blocks, j, j + 1) for j in range(n_blk)]
    return stages


def _ffn_stages(x_ref, y_ref, wo_ref, g2_ref, wg_ref, wu_ref, wd_ref, gf_ref, o_ref, acc_ref):
    box = {}

    def head():
        h = x_ref[...] + jnp.dot(y_ref[...], wo_ref[...], preferred_element_type=F32)
        box["u"] = (h * _rms_scale(h) * g2_ref[...]).astype(BF16)
        acc_ref[...] = h

    def ff_chunk(c0):
        cs = slice(c0, c0 + FF_CHUNK)
        gt = jnp.dot(box["u"], wg_ref[:, cs], preferred_element_type=F32)
        up = jnp.dot(box["u"], wu_ref[:, cs], preferred_element_type=F32)
        act = (_silu(gt) * up).astype(BF16)
        acc_ref[...] += jnp.dot(act, wd_ref[cs, :], preferred_element_type=F32)

    def tail():
        h2 = acc_ref[...]
        o_ref[...] = h2 * _rms_scale(h2) * gf_ref[...]

    return [head] + [functools.partial(ff_chunk, c0) for c0 in range(0, D_FF, FF_CHUNK)] + [tail]


def _run_interleaved(major, minor):
    span = max(len(major) - 2, 1)
    done = 0
    for k, stage in enumerate(major):
        stage()
        want = min(len(minor), -(-(k + 1) * len(minor) // span))
        while done < want:
            minor[done]()
            done += 1


def _mixer_ffn_kernel(sinks_ref, qd_ref, ki_ref, ks_ref, hv_ref, gate_ref, dec_ref, bq_ref,
                      kp_ref, kph_ref, vt_ref, vth_ref, ang_ref,
                      x_ref, wo_ref, g2_ref, wg_ref, wu_ref, wd_ref, gf_ref, o_ref,
                      state_ref, upd_ref, sprev_ref, bias_ref, y_cur, y_prev, acc_ref,
                      *, n_tiles, n_s):
    i = pl.program_id(0)
    seq_start = lax.rem(i, n_s) == 0

    @pl.when(i == 0)
    def _():
        _bias_table_init(bias_ref)

    @pl.when(seq_start)
    def _():
        state_ref[...] = jnp.zeros_like(state_ref)

    first = jnp.where(seq_start, 1, 0)
    mixer_args = (first, sinks_ref, qd_ref, ki_ref, ks_ref, hv_ref, gate_ref, dec_ref, bq_ref,
                  kp_ref, kph_ref, vt_ref, vth_ref, ang_ref, y_cur,
                  state_ref, upd_ref, sprev_ref, bias_ref)
    ffn_args = (x_ref, y_prev, wo_ref, g2_ref, wg_ref, wu_ref, wd_ref, gf_ref, o_ref, acc_ref)

    @pl.when(i == 0)
    def _():
        _run_interleaved(_mixer_stages(*mixer_args), [])
        y_prev[...] = y_cur[...]

    @pl.when((i > 0) & (i < n_tiles))
    def _():
        _run_interleaved(_ffn_stages(*ffn_args), _mixer_stages(*mixer_args))
        y_prev[...] = y_cur[...]

    @pl.when(i == n_tiles)
    def _():
        _run_interleaved(_ffn_stages(*ffn_args), [])


def _mixer_ffn(prep, sinks, attn_norm_g, x2d, w_out, ln2_g, w_gate, w_up, w_down, final_g, seq):
    qd, ki, ks, hv, gate, dec, bq, kp, vt = prep
    t = x2d.shape[0]
    ts = TS_MIX
    n_tiles = t // ts
    n_s = seq // ts
    n_chunk = ts // CHUNK
    n_blk = ts // BLOCK
    cur = lambda i: jnp.minimum(i, n_tiles - 1)
    prev = lambda i: jnp.maximum(i - 1, 0)
    halo = lambda i: jnp.maximum(cur(i) * n_blk - 1, 0)
    row_blk = lambda width: pl.BlockSpec((ts, width), lambda i: (cur(i), 0))
    const = lambda shape: pl.BlockSpec(shape, lambda i: (0, 0), pipeline_mode=pl.Buffered(1))
    return pl.pallas_call(
        functools.partial(_mixer_ffn_kernel, n_tiles=n_tiles, n_s=n_s),
        grid=(n_tiles + 1,),
        in_specs=[
            pl.BlockSpec(memory_space=pltpu.SMEM),
            row_blk(A_QK), row_blk(A_QK), row_blk(A_QK), row_blk(A_WIDTH), row_blk(A_WIDTH),
            pl.BlockSpec((n_chunk, A_QK), lambda i: (cur(i), 0)),
            row_blk(B_WIDTH),
            pl.BlockSpec((4, ts, LANES), lambda i: (0, cur(i), 0)),
            pl.BlockSpec((4, BLOCK, LANES), lambda i: (0, halo(i), 0)),
            pl.BlockSpec((LANES, ts), lambda i: (0, cur(i))),
            pl.BlockSpec((LANES, BLOCK), lambda i: (0, halo(i))),
            pl.BlockSpec((1, B_WIDTH), lambda i: (0, 0)),
            pl.BlockSpec((ts, D_MODEL), lambda i: (prev(i), 0)),
            const((MIX_WIDTH, D_MODEL)),
            const((1, D_MODEL)),
            const((D_MODEL, D_FF)),
            const((D_MODEL, D_FF)),
            const((D_FF, D_MODEL)),
            const((1, D_MODEL)),
        ],
        out_specs=pl.BlockSpec((ts, D_MODEL), lambda i: (prev(i), 0)),
        out_shape=jax.ShapeDtypeStruct((t, D_MODEL), F32),
        scratch_shapes=[
            pltpu.VMEM((A_HEADS, A_DV, A_DK), F32),
            pltpu.VMEM((n_chunk, A_HEADS, A_DV, A_DK), F32),
            pltpu.VMEM((n_chunk, A_HEADS, A_DV, A_DK), BF16),
            pltpu.VMEM((2, B_HEADS, 2 * BLOCK, BLOCK), F32),
            pltpu.VMEM((ts, MIX_WIDTH), BF16),
            pltpu.VMEM((ts, MIX_WIDTH), BF16),
            pltpu.VMEM((ts, D_MODEL), F32),
        ],
        compiler_params=pltpu.CompilerParams(
            dimension_semantics=("arbitrary",), vmem_limit_bytes=V7X_VMEM_LIMIT_BYTES),
        name="mixer_ffn",
    )(sinks, qd, ki, ks, hv, gate, dec, bq, kp, kp, vt, vt, attn_norm_g,
      x2d, w_out, ln2_g, w_gate, w_up, w_down, final_g)


def kernel(x, ln1_g, w_in, lb_logits, hgrn_norm_g, attn_sinks, attn_norm_g, w_out, ln2_g,
           w_gate, w_up, w_down, final_g):
    bsz, seq, d = x.shape
    x2d = x.reshape(bsz * seq, d)
    prep = _in_proj(x2d, ln1_g[0:1], w_in[0].astype(BF16), lb_logits, hgrn_norm_g[0:1])
    out = _mixer_ffn(prep, attn_sinks[0], attn_norm_g[0:1], x2d, w_out[0].astype(BF16), ln2_g[0:1],
                     w_gate[0].astype(BF16), w_up[0].astype(BF16), w_down[0].astype(BF16),
                     final_g.reshape(1, d), seq)
    return out.reshape(bsz, seq, d)
```

```python
import functools

import jax
import jax.numpy as jnp
from jax import lax
from jax.experimental import pallas as pl
from jax.experimental.pallas import tpu as pltpu

D_MODEL = 1024
A_HEADS = 4
A_DK = 128
A_DV = 128
A_QK = A_HEADS * A_DK
A_WIDTH = A_HEADS * A_DV
CHUNK = 64
B_HEADS = 8
B_KV_HEADS = 2
B_HEAD_DIM = 64
B_GROUP = B_HEADS // B_KV_HEADS
B_WIDTH = B_HEADS * B_HEAD_DIM
B_KV = B_KV_HEADS * B_HEAD_DIM
WINDOW = 128
BLOCK = 128
MIX_WIDTH = A_WIDTH + B_WIDTH
D_FF = 2816
IN_COLS = 4 * A_QK + B_WIDTH + 2 * B_KV
EPS = 1e-6
NEG_BIG = -1e30
LANES = 128

F32 = jnp.float32
BF16 = jnp.bfloat16

V7X_VMEM_LIMIT_BYTES = 56 * 1024 * 1024

TM_PROJ = 512
TS_MIX = 512
FF_CHUNK = 256
HGRN_GROUP_CHUNKS = 2
LOG2E = 1.4426950408889634

NT_DIMS = (((1,), (1,)), ((), ()))
TN_DIMS = (((0,), (0,)), ((), ()))


def _rms_scale(v):
    return lax.rsqrt(jnp.mean(v * v, axis=-1, keepdims=True) + EPS)


def _silu(v):
    return v * (1.0 / (1.0 + jnp.exp(-v)))


def _sigmoid(v):
    return 1.0 / (1.0 + jnp.exp(-v))


def _in_proj_kernel(x_ref, g_ref, w_ref, lbl_ref, hng_ref,
                    qd_ref, ki_ref, ks_ref, hv_ref, gate_ref, dec_ref, bq_ref, kp_ref, vt_ref):
    tm = x_ref.shape[0]
    x = x_ref[...]
    u = (x * _rms_scale(x) * g_ref[...]).astype(BF16)

    lbl = lbl_ref[...]
    lmax = jnp.max(lbl, axis=0, keepdims=True)
    le = jnp.exp(lbl - lmax)
    lb = le[0:1, :] / jnp.sum(le, axis=0, keepdims=True)

    r = lax.broadcasted_iota(jnp.int32, (CHUNK, CHUNK), 0)
    c = lax.broadcasted_iota(jnp.int32, (CHUNK, CHUNK), 1)
    tri = jnp.where(r >= c, 1.0, 0.0).astype(BF16)

    pf = jnp.dot(u, w_ref[:, A_QK:2 * A_QK], preferred_element_type=F32)
    pq = jnp.dot(u, w_ref[:, 0:A_QK], preferred_element_type=F32)
    for ci in range(tm // CHUNK):
        rows = slice(ci * CHUNK, (ci + 1) * CHUNK)
        forget = lb + (1.0 - lb) * _sigmoid(pf[rows])
        key = 1.0 - forget
        logf = jnp.log(forget)
        p0 = logf.astype(BF16)
        r1 = logf - p0.astype(F32)
        p1 = r1.astype(BF16)
        p2 = (r1 - p1.astype(F32)).astype(BF16)
        bc3 = jnp.dot(tri, jnp.concatenate([p0, p1, p2], axis=-1), preferred_element_type=F32)
        bcum = bc3[:, :A_QK] + bc3[:, A_QK:2 * A_QK] + bc3[:, 2 * A_QK:]
        decay = jnp.exp(bcum[CHUNK - 1:CHUNK, :])
        k_intra = key * jnp.exp(-bcum)
        qd_ref[rows, :] = (_silu(pq[rows]) * jnp.exp(bcum)).astype(BF16)
        ki_ref[rows, :] = k_intra.astype(BF16)
        ks_ref[rows, :] = (k_intra * decay).astype(BF16)
        dec_ref[ci:ci + 1, :] = decay

    c0 = 2 * A_QK + 2 * A_WIDTH
    pa = jnp.dot(u, w_ref[:, c0:], preferred_element_type=F32)
    bq_ref[...] = (pa[:, :B_WIDTH] * (B_HEAD_DIM ** -0.5 * LOG2E)).astype(BF16)
    lo = lax.broadcasted_iota(jnp.int32, (tm, LANES), 1) < B_HEAD_DIM
    k_all = pa[:, B_WIDTH:B_WIDTH + B_KV]
    k_rot = pltpu.roll(k_all, B_HEAD_DIM, axis=1)
    kp_ref[0] = jnp.where(lo, k_all, 0.0).astype(BF16)
    kp_ref[1] = jnp.where(lo, 0.0, k_rot).astype(BF16)
    kp_ref[2] = jnp.where(lo, k_rot, 0.0).astype(BF16)
    kp_ref[3] = jnp.where(lo, 0.0, k_all).astype(BF16)
    vt_ref[...] = pa[:, B_WIDTH + B_KV:].T.astype(BF16)

    pg = jnp.dot(u, w_ref[:, 2 * A_QK + A_WIDTH:2 * A_QK + 2 * A_WIDTH], preferred_element_type=F32)
    gate_ref[...] = (_silu(pg) * hng_ref[...]).astype(BF16)
    pi = jnp.dot(u, w_ref[:, 2 * A_QK:2 * A_QK + A_WIDTH], preferred_element_type=F32)
    hv_ref[...] = pi.astype(BF16)


def _in_proj(x2d, ln1_g, w_in_bf16, lb_logits, hgrn_norm_g):
    t = x2d.shape[0]
    tm = TM_PROJ
    row_blk = lambda width: pl.BlockSpec((tm, width), lambda i: (i, 0))
    const = lambda shape, **kw: pl.BlockSpec(shape, lambda i: (0, 0), **kw)
    bf = lambda shape: jax.ShapeDtypeStruct(shape, BF16)
    return pl.pallas_call(
        _in_proj_kernel,
        grid=(t // tm,),
        in_specs=[
            row_blk(D_MODEL),
            const((1, D_MODEL)),
            const((D_MODEL, IN_COLS), pipeline_mode=pl.Buffered(1)),
            const((2, A_QK)),
            const((1, A_WIDTH)),
        ],
        out_specs=[
            row_blk(A_QK), row_blk(A_QK), row_blk(A_QK), row_blk(A_WIDTH), row_blk(A_WIDTH),
            pl.BlockSpec((tm // CHUNK, A_QK), lambda i: (i, 0)),
            row_blk(B_WIDTH),
            pl.BlockSpec((4, tm, LANES), lambda i: (0, i, 0)),
            pl.BlockSpec((LANES, tm), lambda i: (0, i)),
        ],
        out_shape=[
            bf((t, A_QK)), bf((t, A_QK)), bf((t, A_QK)), bf((t, A_WIDTH)), bf((t, A_WIDTH)),
            jax.ShapeDtypeStruct((t // CHUNK, A_QK), F32),
            bf((t, B_WIDTH)),
            bf((4, t, LANES)),
            bf((LANES, t)),
        ],
        compiler_params=pltpu.CompilerParams(
            dimension_semantics=("arbitrary",), vmem_limit_bytes=V7X_VMEM_LIMIT_BYTES),
        name="in_proj",
    )(x2d, ln1_g, w_in_bf16, lb_logits, hgrn_norm_g)


def _bias_table_init(bias_ref):
    kj = lax.broadcasted_iota(jnp.int32, (2 * BLOCK, BLOCK), 0)
    qi = lax.broadcasted_iota(jnp.int32, (2 * BLOCK, BLOCK), 1)
    dist = qi + BLOCK - kj
    valid = (dist >= 0) & (dist < WINDOW)
    valid_first = valid & (kj >= BLOCK)
    distf = dist.astype(F32)
    for h in range(B_HEADS):
        slope = 2.0 ** (-(8.0 / B_HEADS) * (h + 1))
        sc = (-slope * LOG2E) * distf
        bias_ref[0, h] = jnp.where(valid, sc, NEG_BIG)
        bias_ref[1, h] = jnp.where(valid_first, sc, NEG_BIG)


def _mixer_stages(first, sinks_ref, qd_ref, ki_ref, ks_ref, hv_ref, gate_ref, dec_ref, bq_ref,
                  kp_ref, kph_ref, vt_ref, vth_ref, ang_ref, y_ref,
                  state_ref, upd_ref, sprev_ref, bias_ref):
    ts = qd_ref.shape[0]
    n_chunk = ts // CHUNK
    n_blk = ts // BLOCK
    head_cols = [slice(h * A_DK, (h + 1) * A_DK) for h in range(A_HEADS)]

    def chunk_units(c_lo, c_hi):
        return [(ci, slice(ci * CHUNK, (ci + 1) * CHUNK), head_cols[h], h)
                for ci in range(c_lo, c_hi) for h in range(A_HEADS)]

    def hgrn_updates(c_lo, c_hi):
        units = chunk_units(c_lo, c_hi)
        upds = [lax.dot_general(hv_ref[rows, cs], ks_ref[rows, cs], TN_DIMS,
                                preferred_element_type=F32) for (_, rows, cs, _) in units]
        for upd, (ci, _, _, h) in zip(upds, units):
            upd_ref[ci, h] = upd

    def hgrn_scan(h):
        st = state_ref[h]
        for ci in range(n_chunk):
            sprev_ref[ci, h] = st.astype(BF16)
            st = st * dec_ref[ci:ci + 1, head_cols[h]] + upd_ref[ci, h]
        state_ref[h] = st

    def hgrn_scores(c_lo, c_hi, box):
        units = chunk_units(c_lo, c_hi)
        qds = [qd_ref[rows, cs] for (_, rows, cs, _) in units]
        box["a"] = [lax.dot_general(qd, ki_ref[rows, cs], NT_DIMS, preferred_element_type=F32)
                    for qd, (_, rows, cs, _) in zip(qds, units)]
        box["o_inter"] = [lax.dot_general(qd, sprev_ref[ci, h], NT_DIMS, preferred_element_type=F32)
                          for qd, (ci, _, _, h) in zip(qds, units)]

    def hgrn_values(c_lo, c_hi, box):
        r = lax.broadcasted_iota(jnp.int32, (CHUNK, CHUNK), 0)
        c = lax.broadcasted_iota(jnp.int32, (CHUNK, CHUNK), 1)
        causal = r >= c
        a_s = [jnp.where(causal, a, 0.0).astype(BF16) for a in box["a"]]
        box["o"] = [jnp.dot(a, hv_ref[rows, cs], preferred_element_type=F32) + oi
                    for a, oi, (_, rows, cs, _) in zip(a_s, box["o_inter"], chunk_units(c_lo, c_hi))]

    def hgrn_store(c_lo, c_hi, box):
        for o, (_, rows, cs, _) in zip(box["o"], chunk_units(c_lo, c_hi)):
            gate = gate_ref[rows, cs].astype(F32)
            y_ref[rows, cs] = (o * _rms_scale(o) * gate).astype(y_ref.dtype)

    swa_units = [(kvh, par) for kvh in range(B_KV_HEADS) for par in range(2)]

    def swa_scores(j, box):
        s_ts = []
        for kvh, par in swa_units:
            rows = slice(j * BLOCK, (j + 1) * BLOCK)
            q_pairs = jnp.concatenate(
                [bq_ref[rows, (2 * kvh) * LANES:(2 * kvh + 1) * LANES],
                 bq_ref[rows, (2 * kvh + 1) * LANES:(2 * kvh + 2) * LANES]], axis=0)
            idx = 2 * kvh + par
            if j == 0:
                kk = jnp.concatenate([kph_ref[idx], kp_ref[idx, 0:BLOCK, :]], axis=0)
            else:
                kk = kp_ref[idx, (j - 1) * BLOCK:(j + 1) * BLOCK, :]
            s_ts.append(lax.dot_general(kk, q_pairs, NT_DIMS, preferred_element_type=F32))
        box["s"] = s_ts

    def swa_softmax(j, box):
        p_ts, inv_ls = [], []
        for (kvh, par), s_t in zip(swa_units, box["s"]):
            p_halves = []
            for half, h in enumerate((4 * kvh + par, 4 * kvh + 2 + par)):
                bias = bias_ref[first, h] if j == 0 else bias_ref[0, h]
                s = s_t[:, half * BLOCK:(half + 1) * BLOCK] + bias
                sink = sinks_ref[h] * LOG2E
                m = jnp.maximum(jnp.max(s, axis=0, keepdims=True), sink)
                p = jnp.exp2(s - m)
                inv_ls.append(1.0 / (jnp.sum(p, axis=0, keepdims=True) + jnp.exp2(sink - m)))
                p_halves.append(p.astype(BF16))
            p_ts.append(jnp.concatenate(p_halves, axis=1))
        box["p"], box["inv_l"] = p_ts, inv_ls

    def swa_values(j, box):
        o_t = [None] * B_HEADS
        for ui, ((kvh, par), p_t) in enumerate(zip(swa_units, box["p"])):
            drow = slice(kvh * B_HEAD_DIM, (kvh + 1) * B_HEAD_DIM)
            if j == 0:
                vt_h = jnp.concatenate([vth_ref[drow, :], vt_ref[drow, 0:BLOCK]], axis=1)
            else:
                vt_h = vt_ref[drow, (j - 1) * BLOCK:(j + 1) * BLOCK]
            pv = jnp.dot(vt_h, p_t, preferred_element_type=F32)
            for half, h in enumerate((4 * kvh + par, 4 * kvh + 2 + par)):
                o_t[h] = pv[:, half * BLOCK:(half + 1) * BLOCK] * box["inv_l"][2 * ui + half]
        box["o_t"] = o_t

    def swa_store(j, box):
        o_all_t = jnp.concatenate(box["o_t"], axis=0)
        scale = lax.rsqrt(jnp.sum(o_all_t * o_all_t, axis=0, keepdims=True) * (1.0 / B_WIDTH) + EPS)
        o_all = (o_all_t * scale).T
        y_ref[j * BLOCK:(j + 1) * BLOCK, A_WIDTH:] = (o_all * ang_ref[...]).astype(y_ref.dtype)

    group = HGRN_GROUP_CHUNKS
    hgrn = [functools.partial(hgrn_updates, c, c + group) for c in range(0, n_chunk, group)]
    hgrn += [functools.partial(hgrn_scan, h) for h in range(A_HEADS)]
    for c in range(0, n_chunk, group):
        box = {}
        hgrn += [functools.partial(f, c, c + group, box) for f in (hgrn_scores, hgrn_values, hgrn_store)]
    swa = []
    for j in range(n_blk):
        box = {}
        swa += [functools.partial(f, j, box) for f in (swa_scores, swa_softmax, swa_values, swa_store)]
    stages = []
    for k in range(max(len(hgrn), len(swa))):
        stages += hgrn[k:k + 1] + swa[k:k + 1]
    return stages


def _ffn_stages(x_ref, y_ref, wo_ref, g2_ref, wg_ref, wu_ref, wd_ref, gf_ref, o_ref, acc_ref):
    box = {}

    def head():
        h = x_ref[...] + jnp.dot(y_ref[...], wo_ref[...], preferred_element_type=F32)
        box["u"] = (h * _rms_scale(h) * g2_ref[...]).astype(BF16)
        acc_ref[...] = h

    def ff_up(c0):
        cs = slice(c0, c0 + FF_CHUNK)
        box["gate"] = jnp.dot(box["u"], wg_ref[:, cs], preferred_element_type=F32)
        box["up"] = jnp.dot(box["u"], wu_ref[:, cs], preferred_element_type=F32)

    def ff_down(c0):
        act = (_silu(box["gate"]) * box["up"]).astype(BF16)
        acc_ref[...] += jnp.dot(act, wd_ref[c0:c0 + FF_CHUNK, :], preferred_element_type=F32)

    def tail():
        h2 = acc_ref[...]
        o_ref[...] = h2 * _rms_scale(h2) * gf_ref[...]

    stages = [head]
    for c0 in range(0, D_FF, FF_CHUNK):
        stages += [functools.partial(ff_up, c0), functools.partial(ff_down, c0)]
    return stages + [tail]


def _run_interleaved(major, minor):
    span = max(len(major) - 2, 1)
    done = 0
    for k, stage in enumerate(major):
        stage()
        want = min(len(minor), -(-(k + 1) * len(minor) // span))
        while done < want:
            minor[done]()
            done += 1


def _mixer_ffn_kernel(sinks_ref, qd_ref, ki_ref, ks_ref, hv_ref, gate_ref, dec_ref, bq_ref,
                      kp_ref, kph_ref, vt_ref, vth_ref, ang_ref,
                      x_ref, wo_ref, g2_ref, wg_ref, wu_ref, wd_ref, gf_ref, o_ref,
                      state_ref, upd_ref, sprev_ref, bias_ref, y_cur, y_prev, acc_ref,
                      *, n_tiles, n_s):
    i = pl.program_id(0)
    seq_start = lax.rem(i, n_s) == 0

    @pl.when(i == 0)
    def _():
        _bias_table_init(bias_ref)

    @pl.when(seq_start)
    def _():
        state_ref[...] = jnp.zeros_like(state_ref)

    first = jnp.where(seq_start, 1, 0)
    mixer_args = (first, sinks_ref, qd_ref, ki_ref, ks_ref, hv_ref, gate_ref, dec_ref, bq_ref,
                  kp_ref, kph_ref, vt_ref, vth_ref, ang_ref, y_cur,
                  state_ref, upd_ref, sprev_ref, bias_ref)
    ffn_args = (x_ref, y_prev, wo_ref, g2_ref, wg_ref, wu_ref, wd_ref, gf_ref, o_ref, acc_ref)

    @pl.when(i == 0)
    def _():
        _run_interleaved(_mixer_stages(*mixer_args), [])
        y_prev[...] = y_cur[...]

    @pl.when((i > 0) & (i < n_tiles))
    def _():
        _run_interleaved(_ffn_stages(*ffn_args), _mixer_stages(*mixer_args))
        y_prev[...] = y_cur[...]

    @pl.when(i == n_tiles)
    def _():
        _run_interleaved(_ffn_stages(*ffn_args), [])


def _mixer_ffn(prep, sinks, attn_norm_g, x2d, w_out, ln2_g, w_gate, w_up, w_down, final_g, seq):
    qd, ki, ks, hv, gate, dec, bq, kp, vt = prep
    t = x2d.shape[0]
    ts = TS_MIX
    n_tiles = t // ts
    n_s = seq // ts
    n_chunk = ts // CHUNK
    n_blk = ts // BLOCK
    cur = lambda i: jnp.minimum(i, n_tiles - 1)
    prev = lambda i: jnp.maximum(i - 1, 0)
    halo = lambda i: jnp.maximum(cur(i) * n_blk - 1, 0)
    row_blk = lambda width: pl.BlockSpec((ts, width), lambda i: (cur(i), 0))
    const = lambda shape: pl.BlockSpec(shape, lambda i: (0, 0), pipeline_mode=pl.Buffered(1))
    return pl.pallas_call(
        functools.partial(_mixer_ffn_kernel, n_tiles=n_tiles, n_s=n_s),
        grid=(n_tiles + 1,),
        in_specs=[
            pl.BlockSpec(memory_space=pltpu.SMEM),
            row_blk(A_QK), row_blk(A_QK), row_blk(A_QK), row_blk(A_WIDTH), row_blk(A_WIDTH),
            pl.BlockSpec((n_chunk, A_QK), lambda i: (cur(i), 0)),
            row_blk(B_WIDTH),
            pl.BlockSpec((4, ts, LANES), lambda i: (0, cur(i), 0)),
            pl.BlockSpec((4, BLOCK, LANES), lambda i: (0, halo(i), 0)),
            pl.BlockSpec((LANES, ts), lambda i: (0, cur(i))),
            pl.BlockSpec((LANES, BLOCK), lambda i: (0, halo(i))),
            pl.BlockSpec((1, B_WIDTH), lambda i: (0, 0)),
            pl.BlockSpec((ts, D_MODEL), lambda i: (prev(i), 0)),
            const((MIX_WIDTH, D_MODEL)),
            const((1, D_MODEL)),
            const((D_MODEL, D_FF)),
            const((D_MODEL, D_FF)),
            const((D_FF, D_MODEL)),
            const((1, D_MODEL)),
        ],
        out_specs=pl.BlockSpec((ts, D_MODEL), lambda i: (prev(i), 0)),
        out_shape=jax.ShapeDtypeStruct((t, D_MODEL), F32),
        scratch_shapes=[
            pltpu.VMEM((A_HEADS, A_DV, A_DK), F32),
            pltpu.VMEM((n_chunk, A_HEADS, A_DV, A_DK), F32),
            pltpu.VMEM((n_chunk, A_HEADS, A_DV, A_DK), BF16),
            pltpu.VMEM((2, B_HEADS, 2 * BLOCK, BLOCK), F32),
            pltpu.VMEM((ts, MIX_WIDTH), BF16),
            pltpu.VMEM((ts, MIX_WIDTH), BF16),
            pltpu.VMEM((ts, D_MODEL), F32),
        ],
        compiler_params=pltpu.CompilerParams(
            dimension_semantics=("arbitrary",), vmem_limit_bytes=V7X_VMEM_LIMIT_BYTES),
        name="mixer_ffn",
    )(sinks, qd, ki, ks, hv, gate, dec, bq, kp, kp, vt, vt, attn_norm_g,
      x2d, w_out, ln2_g, w_gate, w_up, w_down, final_g)


def kernel(x, ln1_g, w_in, lb_logits, hgrn_norm_g, attn_sinks, attn_norm_g, w_out, ln2_g,
           w_gate, w_up, w_down, final_g):
    bsz, seq, d = x.shape
    x2d = x.reshape(bsz * seq, d)
    prep = _in_proj(x2d, ln1_g[0:1], w_in[0].astype(BF16), lb_logits, hgrn_norm_g[0:1])
    out = _mixer_ffn(prep, attn_sinks[0], attn_norm_g[0:1], x2d, w_out[0].astype(BF16), ln2_g[0:1],
                     w_gate[0].astype(BF16), w_up[0].astype(BF16), w_down[0].astype(BF16),
                     final_g.reshape(1, d), seq)
    return out.reshape(bsz, seq, d)
```

```python
import functools

import jax
import jax.numpy as jnp
from jax import lax
from jax.experimental import pallas as pl
from jax.experimental.pallas import tpu as pltpu

D_MODEL = 1024
A_HEADS = 4
A_DK = 128
A_DV = 128
A_QK = A_HEADS * A_DK
A_WIDTH = A_HEADS * A_DV
CHUNK = 64
B_HEADS = 8
B_KV_HEADS = 2
B_HEAD_DIM = 64
B_GROUP = B_HEADS // B_KV_HEADS
B_WIDTH = B_HEADS * B_HEAD_DIM
B_KV = B_KV_HEADS * B_HEAD_DIM
WINDOW = 128
BLOCK = 128
MIX_WIDTH = A_WIDTH + B_WIDTH
D_FF = 2816
IN_COLS = 4 * A_QK + B_WIDTH + 2 * B_KV
EPS = 1e-6
NEG_BIG = -1e30
LANES = 128

F32 = jnp.float32
BF16 = jnp.bfloat16

V7X_VMEM_LIMIT_BYTES = 56 * 1024 * 1024

TM_PROJ = 512
TS_MIX = 512
FF_CHUNK = 256
HGRN_GROUP_CHUNKS = 2
CAST_ROWS = 128
LOG2E = 1.4426950408889634

NT_DIMS = (((1,), (1,)), ((), ()))
TN_DIMS = (((0,), (0,)), ((), ()))


def _rms_scale(v):
    return lax.rsqrt(jnp.mean(v * v, axis=-1, keepdims=True) + EPS)


def _silu(v):
    return v * (1.0 / (1.0 + jnp.exp(-v)))


def _sigmoid(v):
    return 1.0 / (1.0 + jnp.exp(-v))


def _in_proj_kernel(x_ref, g_ref, w32_ref, lbl_ref, hng_ref, wo32_ref, wg32_ref, wu32_ref, wd32_ref,
                    qd_ref, ki_ref, ks_ref, hv_ref, gate_ref, dec_ref, bq_ref, kp_ref, vt_ref,
                    wo16_ref, wg16_ref, wu16_ref, wd16_ref, w_ref, *, n_down_slabs):
    tm = x_ref.shape[0]

    @pl.when(pl.program_id(0) == 0)
    def _():
        for r0 in range(0, D_MODEL, CAST_ROWS):
            w_ref[r0:r0 + CAST_ROWS, :] = w32_ref[r0:r0 + CAST_ROWS, :].astype(BF16)

    @pl.when(pl.program_id(0) < n_down_slabs)
    def _():
        wd16_ref[...] = wd32_ref[...].astype(BF16)

    x = x_ref[...]
    u = (x * _rms_scale(x) * g_ref[...]).astype(BF16)

    lbl = lbl_ref[...]
    lmax = jnp.max(lbl, axis=0, keepdims=True)
    le = jnp.exp(lbl - lmax)
    lb = le[0:1, :] / jnp.sum(le, axis=0, keepdims=True)

    r = lax.broadcasted_iota(jnp.int32, (CHUNK, CHUNK), 0)
    c = lax.broadcasted_iota(jnp.int32, (CHUNK, CHUNK), 1)
    tri = jnp.where(r >= c, 1.0, 0.0).astype(BF16)

    pf = jnp.dot(u, w_ref[:, A_QK:2 * A_QK], preferred_element_type=F32)
    pq = jnp.dot(u, w_ref[:, 0:A_QK], preferred_element_type=F32)
    for ci in range(tm // CHUNK):
        rows = slice(ci * CHUNK, (ci + 1) * CHUNK)
        forget = lb + (1.0 - lb) * _sigmoid(pf[rows])
        key = 1.0 - forget
        logf = jnp.log(forget)
        p0 = logf.astype(BF16)
        r1 = logf - p0.astype(F32)
        p1 = r1.astype(BF16)
        p2 = (r1 - p1.astype(F32)).astype(BF16)
        bc3 = jnp.dot(tri, jnp.concatenate([p0, p1, p2], axis=-1), preferred_element_type=F32)
        bcum = bc3[:, :A_QK] + bc3[:, A_QK:2 * A_QK] + bc3[:, 2 * A_QK:]
        decay = jnp.exp(bcum[CHUNK - 1:CHUNK, :])
        k_intra = key * jnp.exp(-bcum)
        qd_ref[rows, :] = (_silu(pq[rows]) * jnp.exp(bcum)).astype(BF16)
        ki_ref[rows, :] = k_intra.astype(BF16)
        ks_ref[rows, :] = (k_intra * decay).astype(BF16)
        dec_ref[ci:ci + 1, :] = decay

    wo16_ref[...] = wo32_ref[...].astype(BF16)
    wg16_ref[...] = wg32_ref[...].astype(BF16)
    wu16_ref[...] = wu32_ref[...].astype(BF16)

    c0 = 2 * A_QK + 2 * A_WIDTH
    pa = jnp.dot(u, w_ref[:, c0:], preferred_element_type=F32)
    bq_ref[...] = (pa[:, :B_WIDTH] * (B_HEAD_DIM ** -0.5 * LOG2E)).astype(BF16)
    lo = lax.broadcasted_iota(jnp.int32, (tm, LANES), 1) < B_HEAD_DIM
    k_all = pa[:, B_WIDTH:B_WIDTH + B_KV]
    k_rot = pltpu.roll(k_all, B_HEAD_DIM, axis=1)
    kp_ref[0] = jnp.where(lo, k_all, 0.0).astype(BF16)
    kp_ref[1] = jnp.where(lo, 0.0, k_rot).astype(BF16)
    kp_ref[2] = jnp.where(lo, k_rot, 0.0).astype(BF16)
    kp_ref[3] = jnp.where(lo, 0.0, k_all).astype(BF16)
    vt_ref[...] = pa[:, B_WIDTH + B_KV:].T.astype(BF16)

    pg = jnp.dot(u, w_ref[:, 2 * A_QK + A_WIDTH:2 * A_QK + 2 * A_WIDTH], preferred_element_type=F32)
    gate_ref[...] = (_silu(pg) * hng_ref[...]).astype(BF16)
    pi = jnp.dot(u, w_ref[:, 2 * A_QK:2 * A_QK + A_WIDTH], preferred_element_type=F32)
    hv_ref[...] = pi.astype(BF16)


def _in_proj(x2d, ln1_g, w_in, lb_logits, hgrn_norm_g, w_out, w_gate, w_up, w_down):
    t = x2d.shape[0]
    tm = TM_PROJ
    n_steps = t // tm
    row_blk = lambda width: pl.BlockSpec((tm, width), lambda i: (i, 0))
    const = lambda shape, **kw: pl.BlockSpec(shape, lambda i: (0, 0), **kw)
    bf = lambda shape: jax.ShapeDtypeStruct(shape, BF16)

    def slab(w, rows):
        n_slabs = w.shape[0] // rows
        assert w.shape[0] % rows == 0 and n_slabs <= n_steps, (w.shape, rows)
        return pl.BlockSpec((rows, w.shape[1]), lambda i: (jnp.minimum(i, n_slabs - 1), 0))

    cast_weights = (w_out, w_gate, w_up, w_down)
    even_rows = D_MODEL // n_steps
    slab_rows = (even_rows, even_rows, even_rows, CAST_ROWS)
    slab_specs = lambda: [slab(w, rows) for w, rows in zip(cast_weights, slab_rows)]
    return pl.pallas_call(
        functools.partial(_in_proj_kernel, n_down_slabs=D_FF // CAST_ROWS),
        grid=(n_steps,),
        in_specs=[
            row_blk(D_MODEL),
            const((1, D_MODEL)),
            const((D_MODEL, IN_COLS), pipeline_mode=pl.Buffered(1)),
            const((2, A_QK)),
            const((1, A_WIDTH)),
        ] + slab_specs(),
        out_specs=[
            row_blk(A_QK), row_blk(A_QK), row_blk(A_QK), row_blk(A_WIDTH), row_blk(A_WIDTH),
            pl.BlockSpec((tm // CHUNK, A_QK), lambda i: (i, 0)),
            row_blk(B_WIDTH),
            pl.BlockSpec((4, tm, LANES), lambda i: (0, i, 0)),
            pl.BlockSpec((LANES, tm), lambda i: (0, i)),
        ] + slab_specs(),
        out_shape=[
            bf((t, A_QK)), bf((t, A_QK)), bf((t, A_QK)), bf((t, A_WIDTH)), bf((t, A_WIDTH)),
            jax.ShapeDtypeStruct((t // CHUNK, A_QK), F32),
            bf((t, B_WIDTH)),
            bf((4, t, LANES)),
            bf((LANES, t)),
        ] + [bf(w.shape) for w in cast_weights],
        scratch_shapes=[pltpu.VMEM((D_MODEL, IN_COLS), BF16)],
        compiler_params=pltpu.CompilerParams(
            dimension_semantics=("arbitrary",), vmem_limit_bytes=V7X_VMEM_LIMIT_BYTES),
        name="in_proj",
    )(x2d, ln1_g, w_in, lb_logits, hgrn_norm_g, *cast_weights)


def _bias_table_init(bias_ref):
    kj = lax.broadcasted_iota(jnp.int32, (2 * BLOCK, BLOCK), 0)
    qi = lax.broadcasted_iota(jnp.int32, (2 * BLOCK, BLOCK), 1)
    dist = qi + BLOCK - kj
    valid = (dist >= 0) & (dist < WINDOW)
    valid_first = valid & (kj >= BLOCK)
    distf = dist.astype(F32)
    for h in range(B_HEADS):
        slope = 2.0 ** (-(8.0 / B_HEADS) * (h + 1))
        sc = (-slope * LOG2E) * distf
        bias_ref[0, h] = jnp.where(valid, sc, NEG_BIG)
        bias_ref[1, h] = jnp.where(valid_first, sc, NEG_BIG)


def _mixer_stages(first, sinks_ref, qd_ref, ki_ref, ks_ref, hv_ref, gate_ref, dec_ref, bq_ref,
                  kp_ref, kph_ref, vt_ref, vth_ref, ang_ref, y_ref,
                  state_ref, upd_ref, sprev_ref, bias_ref):
    ts = qd_ref.shape[0]
    n_chunk = ts // CHUNK
    n_blk = ts // BLOCK
    head_cols = [slice(h * A_DK, (h + 1) * A_DK) for h in range(A_HEADS)]

    def chunk_units(c_lo, c_hi):
        return [(ci, slice(ci * CHUNK, (ci + 1) * CHUNK), head_cols[h], h)
                for ci in range(c_lo, c_hi) for h in range(A_HEADS)]

    def hgrn_updates(c_lo, c_hi):
        units = chunk_units(c_lo, c_hi)
        upds = [lax.dot_general(hv_ref[rows, cs], ks_ref[rows, cs], TN_DIMS,
                                preferred_element_type=F32) for (_, rows, cs, _) in units]
        for upd, (ci, _, _, h) in zip(upds, units):
            upd_ref[ci, h] = upd

    def hgrn_scan(h):
        st = state_ref[h]
        for ci in range(n_chunk):
            sprev_ref[ci, h] = st.astype(BF16)
            st = st * dec_ref[ci:ci + 1, head_cols[h]] + upd_ref[ci, h]
        state_ref[h] = st

    def hgrn_scores(c_lo, c_hi, box):
        units = chunk_units(c_lo, c_hi)
        qds = [qd_ref[rows, cs] for (_, rows, cs, _) in units]
        box["a"] = [lax.dot_general(qd, ki_ref[rows, cs], NT_DIMS, preferred_element_type=F32)
                    for qd, (_, rows, cs, _) in zip(qds, units)]
        box["o_inter"] = [lax.dot_general(qd, sprev_ref[ci, h], NT_DIMS, preferred_element_type=F32)
                          for qd, (ci, _, _, h) in zip(qds, units)]

    def hgrn_values(c_lo, c_hi, box):
        r = lax.broadcasted_iota(jnp.int32, (CHUNK, CHUNK), 0)
        c = lax.broadcasted_iota(jnp.int32, (CHUNK, CHUNK), 1)
        causal = r >= c
        a_s = [jnp.where(causal, a, 0.0).astype(BF16) for a in box["a"]]
        box["o"] = [jnp.dot(a, hv_ref[rows, cs], preferred_element_type=F32) + oi
                    for a, oi, (_, rows, cs, _) in zip(a_s, box["o_inter"], chunk_units(c_lo, c_hi))]

    def hgrn_store(c_lo, c_hi, box):
        for o, (_, rows, cs, _) in zip(box["o"], chunk_units(c_lo, c_hi)):
            gate = gate_ref[rows, cs].astype(F32)
            y_ref[rows, cs] = (o * _rms_scale(o) * gate).astype(y_ref.dtype)

    swa_units = [(kvh, par) for kvh in range(B_KV_HEADS) for par in range(2)]

    def swa_scores(j, box):
        s_ts = []
        for kvh, par in swa_units:
            rows = slice(j * BLOCK, (j + 1) * BLOCK)
            q_pairs = jnp.concatenate(
                [bq_ref[rows, (2 * kvh) * LANES:(2 * kvh + 1) * LANES],
                 bq_ref[rows, (2 * kvh + 1) * LANES:(2 * kvh + 2) * LANES]], axis=0)
            idx = 2 * kvh + par
            if j == 0:
                kk = jnp.concatenate([kph_ref[idx], kp_ref[idx, 0:BLOCK, :]], axis=0)
            else:
                kk = kp_ref[idx, (j - 1) * BLOCK:(j + 1) * BLOCK, :]
            s_ts.append(lax.dot_general(kk, q_pairs, NT_DIMS, preferred_element_type=F32))
        box["s"] = s_ts

    def swa_softmax(j, box):
        p_ts, inv_ls = [], []
        for (kvh, par), s_t in zip(swa_units, box["s"]):
            p_halves = []
            for half, h in enumerate((4 * kvh + par, 4 * kvh + 2 + par)):
                bias = bias_ref[first, h] if j == 0 else bias_ref[0, h]
                s = s_t[:, half * BLOCK:(half + 1) * BLOCK] + bias
                sink = sinks_ref[h] * LOG2E
                m = jnp.maximum(jnp.max(s, axis=0, keepdims=True), sink)
                p = jnp.exp2(s - m)
                inv_ls.append(1.0 / (jnp.sum(p, axis=0, keepdims=True) + jnp.exp2(sink - m)))
                p_halves.append(p.astype(BF16))
            p_ts.append(jnp.concatenate(p_halves, axis=1))
        box["p"], box["inv_l"] = p_ts, inv_ls

    def swa_values(j, box):
        o_t = [None] * B_HEADS
        for ui, ((kvh, par), p_t) in enumerate(zip(swa_units, box["p"])):
            drow = slice(kvh * B_HEAD_DIM, (kvh + 1) * B_HEAD_DIM)
            if j == 0:
                vt_h = jnp.concatenate([vth_ref[drow, :], vt_ref[drow, 0:BLOCK]], axis=1)
            else:
                vt_h = vt_ref[drow, (j - 1) * BLOCK:(j + 1) * BLOCK]
            pv = jnp.dot(vt_h, p_t, preferred_element_type=F32)
            for half, h in enumerate((4 * kvh + par, 4 * kvh + 2 + par)):
                o_t[h] = pv[:, half * BLOCK:(half + 1) * BLOCK] * box["inv_l"][2 * ui + half]
        box["o_t"] = o_t

    def swa_store(j, box):
        o_all_t = jnp.concatenate(box["o_t"], axis=0)
        scale = lax.rsqrt(jnp.sum(o_all_t * o_all_t, axis=0, keepdims=True) * (1.0 / B_WIDTH) + EPS)
        o_all = (o_all_t * scale).T
        y_ref[j * BLOCK:(j + 1) * BLOCK, A_WIDTH:] = (o_all * ang_ref[...]).astype(y_ref.dtype)

    group = HGRN_GROUP_CHUNKS
    hgrn = [functools.partial(hgrn_updates, c, c + group) for c in range(0, n_chunk, group)]
    hgrn += [functools.partial(hgrn_scan, h) for h in range(A_HEADS)]
    for c in range(0, n_chunk, group):
        box = {}
        hgrn += [functools.partial(f, c, c + group, box) for f in (hgrn_scores, hgrn_values, hgrn_store)]
    swa = []
    for j in range(n_blk):
        box = {}
        swa += [functools.partial(f, j, box) for f in (swa_scores, swa_softmax, swa_values, swa_store)]
    stages = []
    for k in range(max(len(hgrn), len(swa))):
        stages += hgrn[k:k + 1] + swa[k:k + 1]
    return stages


def _ffn_stages(x_ref, y_ref, wo_ref, g2_ref, wg_ref, wu_ref, wd_ref, gf_ref, o_ref, acc_ref):
    box = {}

    def head():
        h = x_ref[...] + jnp.dot(y_ref[...], wo_ref[...], preferred_element_type=F32)
        box["u"] = (h * _rms_scale(h) * g2_ref[...]).astype(BF16)
        acc_ref[...] = h

    def ff_up(c0):
        cs = slice(c0, c0 + FF_CHUNK)
        box["gate"] = jnp.dot(box["u"], wg_ref[:, cs], preferred_element_type=F32)
        box["up"] = jnp.dot(box["u"], wu_ref[:, cs], preferred_element_type=F32)

    def ff_down(c0):
        act = (_silu(box["gate"]) * box["up"]).astype(BF16)
        acc_ref[...] += jnp.dot(act, wd_ref[c0:c0 + FF_CHUNK, :], preferred_element_type=F32)

    def tail():
        h2 = acc_ref[...]
        o_ref[...] = h2 * _rms_scale(h2) * gf_ref[...]

    stages = [head]
    for c0 in range(0, D_FF, FF_CHUNK):
        stages += [functools.partial(ff_up, c0), functools.partial(ff_down, c0)]
    return stages + [tail]


def _run_interleaved(major, minor):
    span = max(len(major) - 2, 1)
    done = 0
    for k, stage in enumerate(major):
        stage()
        want = min(len(minor), -(-(k + 1) * len(minor) // span))
        while done < want:
            minor[done]()
            done += 1


def _mixer_ffn_kernel(sinks_ref, qd_ref, ki_ref, ks_ref, hv_ref, gate_ref, dec_ref, bq_ref,
                      kp_ref, kph_ref, vt_ref, vth_ref, ang_ref,
                      x_ref, wo_ref, g2_ref, wg_ref, wu_ref, wd_ref, gf_ref, o_ref,
                      state_ref, upd_ref, sprev_ref, bias_ref, y_cur, y_prev, acc_ref,
                      *, n_tiles, n_s):
    i = pl.program_id(0)
    seq_start = lax.rem(i, n_s) == 0

    @pl.when(i == 0)
    def _():
        _bias_table_init(bias_ref)

    @pl.when(seq_start)
    def _():
        state_ref[...] = jnp.zeros_like(state_ref)

    first = jnp.where(seq_start, 1, 0)
    mixer_args = (first, sinks_ref, qd_ref, ki_ref, ks_ref, hv_ref, gate_ref, dec_ref, bq_ref,
                  kp_ref, kph_ref, vt_ref, vth_ref, ang_ref, y_cur,
                  state_ref, upd_ref, sprev_ref, bias_ref)
    ffn_args = (x_ref, y_prev, wo_ref, g2_ref, wg_ref, wu_ref, wd_ref, gf_ref, o_ref, acc_ref)

    @pl.when(i == 0)
    def _():
        _run_interleaved(_mixer_stages(*mixer_args), [])
        y_prev[...] = y_cur[...]

    @pl.when((i > 0) & (i < n_tiles))
    def _():
        _run_interleaved(_ffn_stages(*ffn_args), _mixer_stages(*mixer_args))
        y_prev[...] = y_cur[...]

    @pl.when(i == n_tiles)
    def _():
        _run_interleaved(_ffn_stages(*ffn_args), [])


def _mixer_ffn(prep, sinks, attn_norm_g, x2d, w_out, ln2_g, w_gate, w_up, w_down, final_g, seq):
    qd, ki, ks, hv, gate, dec, bq, kp, vt = prep
    t = x2d.shape[0]
    ts = TS_MIX
    n_tiles = t // ts
    n_s = seq // ts
    n_chunk = ts // CHUNK
    n_blk = ts // BLOCK
    cur = lambda i: jnp.minimum(i, n_tiles - 1)
    prev = lambda i: jnp.maximum(i - 1, 0)
    halo = lambda i: jnp.maximum(cur(i) * n_blk - 1, 0)
    row_blk = lambda width: pl.BlockSpec((ts, width), lambda i: (cur(i), 0))
    const = lambda shape: pl.BlockSpec(shape, lambda i: (0, 0), pipeline_mode=pl.Buffered(1))
    return pl.pallas_call(
        functools.partial(_mixer_ffn_kernel, n_tiles=n_tiles, n_s=n_s),
        grid=(n_tiles + 1,),
        in_specs=[
            pl.BlockSpec(memory_space=pltpu.SMEM),
            row_blk(A_QK), row_blk(A_QK), row_blk(A_QK), row_blk(A_WIDTH), row_blk(A_WIDTH),
            pl.BlockSpec((n_chunk, A_QK), lambda i: (cur(i), 0)),
            row_blk(B_WIDTH),
            pl.BlockSpec((4, ts, LANES), lambda i: (0, cur(i), 0)),
            pl.BlockSpec((4, BLOCK, LANES), lambda i: (0, halo(i), 0)),
            pl.BlockSpec((LANES, ts), lambda i: (0, cur(i))),
            pl.BlockSpec((LANES, BLOCK), lambda i: (0, halo(i))),
            pl.BlockSpec((1, B_WIDTH), lambda i: (0, 0)),
            pl.BlockSpec((ts, D_MODEL), lambda i: (prev(i), 0)),
            const((MIX_WIDTH, D_MODEL)),
            const((1, D_MODEL)),
            const((D_MODEL, D_FF)),
            const((D_MODEL, D_FF)),
            const((D_FF, D_MODEL)),
            const((1, D_MODEL)),
        ],
        out_specs=pl.BlockSpec((ts, D_MODEL), lambda i: (prev(i), 0)),
        out_shape=jax.ShapeDtypeStruct((t, D_MODEL), F32),
        scratch_shapes=[
            pltpu.VMEM((A_HEADS, A_DV, A_DK), F32),
            pltpu.VMEM((n_chunk, A_HEADS, A_DV, A_DK), F32),
            pltpu.VMEM((n_chunk, A_HEADS, A_DV, A_DK), BF16),
            pltpu.VMEM((2, B_HEADS, 2 * BLOCK, BLOCK), F32),
            pltpu.VMEM((ts, MIX_WIDTH), BF16),
            pltpu.VMEM((ts, MIX_WIDTH), BF16),
            pltpu.VMEM((ts, D_MODEL), F32),
        ],
        compiler_params=pltpu.CompilerParams(
            dimension_semantics=("arbitrary",), vmem_limit_bytes=V7X_VMEM_LIMIT_BYTES),
        name="mixer_ffn",
    )(sinks, qd, ki, ks, hv, gate, dec, bq, kp, kp, vt, vt, attn_norm_g,
      x2d, w_out, ln2_g, w_gate, w_up, w_down, final_g)


def kernel(x, ln1_g, w_in, lb_logits, hgrn_norm_g, attn_sinks, attn_norm_g, w_out, ln2_g,
           w_gate, w_up, w_down, final_g):
    bsz, seq, d = x.shape
    x2d = x.reshape(bsz * seq, d)
    *prep, wo16, wg16, wu16, wd16 = _in_proj(x2d, ln1_g[0:1], w_in[0], lb_logits, hgrn_norm_g[0:1],
                                             w_out[0], w_gate[0], w_up[0], w_down[0])
    out = _mixer_ffn(prep, attn_sinks[0], attn_norm_g[0:1], x2d, wo16, ln2_g[0:1],
                     wg16, wu16, wd16, final_g.reshape(1, d), seq)
    return out.reshape(bsz, seq, d)
```

```python
import functools

import jax
import jax.numpy as jnp
from jax import lax
from jax.experimental import pallas as pl
from jax.experimental.pallas import tpu as pltpu

D_MODEL = 1024
A_HEADS = 4
A_DK = 128
A_DV = 128
A_QK = A_HEADS * A_DK
A_WIDTH = A_HEADS * A_DV
CHUNK = 64
B_HEADS = 8
B_KV_HEADS = 2
B_HEAD_DIM = 64
B_GROUP = B_HEADS // B_KV_HEADS
B_WIDTH = B_HEADS * B_HEAD_DIM
B_KV = B_KV_HEADS * B_HEAD_DIM
WINDOW = 128
BLOCK = 128
MIX_WIDTH = A_WIDTH + B_WIDTH
D_FF = 2816
IN_COLS = 4 * A_QK + B_WIDTH + 2 * B_KV
EPS = 1e-6
NEG_BIG = -1e30
LANES = 128

F32 = jnp.float32
BF16 = jnp.bfloat16

V7X_VMEM_LIMIT_BYTES = 56 * 1024 * 1024

TM_PROJ = 512
TS_MIX = 512
FF_CHUNK = 256
HGRN_GROUP_CHUNKS = 2
PAIR = 2 * CHUNK
CAST_ROWS = 128
LOG2E = 1.4426950408889634

NT_DIMS = (((1,), (1,)), ((), ()))
TN_DIMS = (((0,), (0,)), ((), ()))


def _rms_scale(v):
    return lax.rsqrt(jnp.mean(v * v, axis=-1, keepdims=True) + EPS)


def _silu(v):
    return v * (1.0 / (1.0 + jnp.exp(-v)))


def _sigmoid(v):
    return 1.0 / (1.0 + jnp.exp(-v))


def _in_proj_kernel(x_ref, g_ref, w32_ref, lbl_ref, hng_ref, wo32_ref, wg32_ref, wu32_ref, wd32_ref,
                    qd_ref, kit_ref, ks_ref, hv_ref, gate_ref, dec_ref, bq_ref, kp_ref, vt_ref,
                    wo16_ref, wg16_ref, wu16_ref, wd16_ref, w_ref, *, n_down_slabs):
    tm = x_ref.shape[0]

    @pl.when(pl.program_id(0) == 0)
    def _():
        for r0 in range(0, D_MODEL, CAST_ROWS):
            w_ref[r0:r0 + CAST_ROWS, :] = w32_ref[r0:r0 + CAST_ROWS, :].astype(BF16)

    @pl.when(pl.program_id(0) < n_down_slabs)
    def _():
        wd16_ref[...] = wd32_ref[...].astype(BF16)

    x = x_ref[...]
    u = (x * _rms_scale(x) * g_ref[...]).astype(BF16)

    lbl = lbl_ref[...]
    lmax = jnp.max(lbl, axis=0, keepdims=True)
    le = jnp.exp(lbl - lmax)
    lb = le[0:1, :] / jnp.sum(le, axis=0, keepdims=True)

    r = lax.broadcasted_iota(jnp.int32, (CHUNK, CHUNK), 0)
    c = lax.broadcasted_iota(jnp.int32, (CHUNK, CHUNK), 1)
    tri = jnp.where(r >= c, 1.0, 0.0).astype(BF16)

    pf = jnp.dot(u, w_ref[:, A_QK:2 * A_QK], preferred_element_type=F32)
    pq = jnp.dot(u, w_ref[:, 0:A_QK], preferred_element_type=F32)
    k_pair = []
    for ci in range(tm // CHUNK):
        rows = slice(ci * CHUNK, (ci + 1) * CHUNK)
        forget = lb + (1.0 - lb) * _sigmoid(pf[rows])
        key = 1.0 - forget
        logf = jnp.log(forget)
        p0 = logf.astype(BF16)
        r1 = logf - p0.astype(F32)
        p1 = r1.astype(BF16)
        p2 = (r1 - p1.astype(F32)).astype(BF16)
        bc3 = jnp.dot(tri, jnp.concatenate([p0, p1, p2], axis=-1), preferred_element_type=F32)
        bcum = bc3[:, :A_QK] + bc3[:, A_QK:2 * A_QK] + bc3[:, 2 * A_QK:]
        decay = jnp.exp(bcum[CHUNK - 1:CHUNK, :])
        k_intra = key * jnp.exp(-bcum)
        qd_ref[rows, :] = (_silu(pq[rows]) * jnp.exp(bcum)).astype(BF16)
        ks_ref[rows, :] = (k_intra * decay).astype(BF16)
        dec_ref[ci:ci + 1, :] = decay
        k_pair.append(k_intra)
        if len(k_pair) == PAIR // CHUNK:
            p0_row = (ci + 1) * CHUNK - PAIR
            kit_ref[:, p0_row:p0_row + PAIR] = jnp.concatenate(k_pair, axis=0).T.astype(BF16)
            k_pair = []

    wo16_ref[...] = wo32_ref[...].astype(BF16)
    wg16_ref[...] = wg32_ref[...].astype(BF16)
    wu16_ref[...] = wu32_ref[...].astype(BF16)

    c0 = 2 * A_QK + 2 * A_WIDTH
    pa = jnp.dot(u, w_ref[:, c0:], preferred_element_type=F32)
    bq_ref[...] = (pa[:, :B_WIDTH] * (B_HEAD_DIM ** -0.5 * LOG2E)).astype(BF16)
    lo = lax.broadcasted_iota(jnp.int32, (tm, LANES), 1) < B_HEAD_DIM
    k_all = pa[:, B_WIDTH:B_WIDTH + B_KV]
    k_rot = pltpu.roll(k_all, B_HEAD_DIM, axis=1)
    kp_ref[0] = jnp.where(lo, k_all, 0.0).astype(BF16)
    kp_ref[1] = jnp.where(lo, 0.0, k_rot).astype(BF16)
    kp_ref[2] = jnp.where(lo, k_rot, 0.0).astype(BF16)
    kp_ref[3] = jnp.where(lo, 0.0, k_all).astype(BF16)
    vt_ref[...] = pa[:, B_WIDTH + B_KV:].T.astype(BF16)

    pg = jnp.dot(u, w_ref[:, 2 * A_QK + A_WIDTH:2 * A_QK + 2 * A_WIDTH], preferred_element_type=F32)
    gate_ref[...] = (_silu(pg) * hng_ref[...]).astype(BF16)
    pi = jnp.dot(u, w_ref[:, 2 * A_QK:2 * A_QK + A_WIDTH], preferred_element_type=F32)
    hv_ref[...] = pi.astype(BF16)


def _in_proj(x2d, ln1_g, w_in, lb_logits, hgrn_norm_g, w_out, w_gate, w_up, w_down):
    t = x2d.shape[0]
    tm = TM_PROJ
    n_steps = t // tm
    row_blk = lambda width: pl.BlockSpec((tm, width), lambda i: (i, 0))
    const = lambda shape, **kw: pl.BlockSpec(shape, lambda i: (0, 0), **kw)
    bf = lambda shape: jax.ShapeDtypeStruct(shape, BF16)

    def slab(w, rows):
        n_slabs = w.shape[0] // rows
        assert w.shape[0] % rows == 0 and n_slabs <= n_steps, (w.shape, rows)
        return pl.BlockSpec((rows, w.shape[1]), lambda i: (jnp.minimum(i, n_slabs - 1), 0))

    cast_weights = (w_out, w_gate, w_up, w_down)
    even_rows = D_MODEL // n_steps
    slab_rows = (even_rows, even_rows, even_rows, CAST_ROWS)
    slab_specs = lambda: [slab(w, rows) for w, rows in zip(cast_weights, slab_rows)]
    return pl.pallas_call(
        functools.partial(_in_proj_kernel, n_down_slabs=D_FF // CAST_ROWS),
        grid=(n_steps,),
        in_specs=[
            row_blk(D_MODEL),
            const((1, D_MODEL)),
            const((D_MODEL, IN_COLS), pipeline_mode=pl.Buffered(1)),
            const((2, A_QK)),
            const((1, A_WIDTH)),
        ] + slab_specs(),
        out_specs=[
            row_blk(A_QK), pl.BlockSpec((A_QK, tm), lambda i: (0, i)), row_blk(A_QK),
            row_blk(A_WIDTH), row_blk(A_WIDTH),
            pl.BlockSpec((tm // CHUNK, A_QK), lambda i: (i, 0)),
            row_blk(B_WIDTH),
            pl.BlockSpec((4, tm, LANES), lambda i: (0, i, 0)),
            pl.BlockSpec((LANES, tm), lambda i: (0, i)),
        ] + slab_specs(),
        out_shape=[
            bf((t, A_QK)), bf((A_QK, t)), bf((t, A_QK)), bf((t, A_WIDTH)), bf((t, A_WIDTH)),
            jax.ShapeDtypeStruct((t // CHUNK, A_QK), F32),
            bf((t, B_WIDTH)),
            bf((4, t, LANES)),
            bf((LANES, t)),
        ] + [bf(w.shape) for w in cast_weights],
        scratch_shapes=[pltpu.VMEM((D_MODEL, IN_COLS), BF16)],
        compiler_params=pltpu.CompilerParams(
            dimension_semantics=("arbitrary",), vmem_limit_bytes=V7X_VMEM_LIMIT_BYTES),
        name="in_proj",
    )(x2d, ln1_g, w_in, lb_logits, hgrn_norm_g, *cast_weights)


def _bias_table_init(bias_ref):
    kj = lax.broadcasted_iota(jnp.int32, (2 * BLOCK, BLOCK), 0)
    qi = lax.broadcasted_iota(jnp.int32, (2 * BLOCK, BLOCK), 1)
    dist = qi + BLOCK - kj
    valid = (dist >= 0) & (dist < WINDOW)
    valid_first = valid & (kj >= BLOCK)
    distf = dist.astype(F32)
    for h in range(B_HEADS):
        slope = 2.0 ** (-(8.0 / B_HEADS) * (h + 1))
        sc = (-slope * LOG2E) * distf
        bias_ref[0, h] = jnp.where(valid, sc, NEG_BIG)
        bias_ref[1, h] = jnp.where(valid_first, sc, NEG_BIG)


def _mixer_stages(first, sinks_ref, qd_ref, kit_ref, ks_ref, hv_ref, gate_ref, dec_ref, bq_ref,
                  kp_ref, kph_ref, vt_ref, vth_ref, ang_ref, y_ref,
                  state_ref, upd_ref, sprev_ref, bias_ref):
    ts = qd_ref.shape[0]
    n_chunk = ts // CHUNK
    n_blk = ts // BLOCK
    head_cols = [slice(h * A_DK, (h + 1) * A_DK) for h in range(A_HEADS)]

    def chunk_units(c_lo, c_hi):
        return [(ci, slice(ci * CHUNK, (ci + 1) * CHUNK), head_cols[h], h)
                for ci in range(c_lo, c_hi) for h in range(A_HEADS)]

    def hgrn_updates(c_lo, c_hi):
        units = chunk_units(c_lo, c_hi)
        upds = [lax.dot_general(hv_ref[rows, cs], ks_ref[rows, cs], TN_DIMS,
                                preferred_element_type=F32) for (_, rows, cs, _) in units]
        for upd, (ci, _, _, h) in zip(upds, units):
            upd_ref[ci, h] = upd

    def hgrn_scan(h):
        st = state_ref[h]
        for ci in range(n_chunk):
            sprev_ref[ci, h] = st.T.astype(BF16)
            st = st * dec_ref[ci:ci + 1, head_cols[h]] + upd_ref[ci, h]
        state_ref[h] = st

    def hgrn_scores(p, box):
        rows = slice(p * PAIR, (p + 1) * PAIR)
        box["a"] = [jnp.dot(qd_ref[rows, cs], kit_ref[cs, rows], preferred_element_type=F32)
                    for cs in head_cols]
        box["o_inter"] = [
            jnp.concatenate(
                [jnp.dot(qd_ref[ci * CHUNK:(ci + 1) * CHUNK, cs], sprev_ref[ci, h],
                         preferred_element_type=F32)
                 for ci in range(p * PAIR // CHUNK, (p + 1) * PAIR // CHUNK)], axis=0)
            for h, cs in enumerate(head_cols)]

    def hgrn_values(p, box):
        rows = slice(p * PAIR, (p + 1) * PAIR)
        r = lax.broadcasted_iota(jnp.int32, (PAIR, PAIR), 0)
        c = lax.broadcasted_iota(jnp.int32, (PAIR, PAIR), 1)
        same_chunk_causal = (c <= r) & (c >= (r // CHUNK) * CHUNK)
        a_s = [jnp.where(same_chunk_causal, a, 0.0).astype(BF16) for a in box["a"]]
        box["o"] = [jnp.dot(a, hv_ref[rows, cs], preferred_element_type=F32) + oi
                    for a, oi, cs in zip(a_s, box["o_inter"], head_cols)]

    def hgrn_store(p, box):
        rows = slice(p * PAIR, (p + 1) * PAIR)
        for o, cs in zip(box["o"], head_cols):
            gate = gate_ref[rows, cs].astype(F32)
            y_ref[rows, cs] = (o * _rms_scale(o) * gate).astype(y_ref.dtype)

    swa_units = [(kvh, par) for kvh in range(B_KV_HEADS) for par in range(2)]

    def swa_scores(j, box):
        s_ts = []
        for kvh, par in swa_units:
            rows = slice(j * BLOCK, (j + 1) * BLOCK)
            q_pairs = jnp.concatenate(
                [bq_ref[rows, (2 * kvh) * LANES:(2 * kvh + 1) * LANES],
                 bq_ref[rows, (2 * kvh + 1) * LANES:(2 * kvh + 2) * LANES]], axis=0)
            idx = 2 * kvh + par
            if j == 0:
                kk = jnp.concatenate([kph_ref[idx], kp_ref[idx, 0:BLOCK, :]], axis=0)
            else:
                kk = kp_ref[idx, (j - 1) * BLOCK:(j + 1) * BLOCK, :]
            s_ts.append(lax.dot_general(kk, q_pairs, NT_DIMS, preferred_element_type=F32))
        box["s"] = s_ts

    def swa_softmax(j, box):
        p_ts, inv_ls = [], []
        for (kvh, par), s_t in zip(swa_units, box["s"]):
            p_halves = []
            for half, h in enumerate((4 * kvh + par, 4 * kvh + 2 + par)):
                bias = bias_ref[first, h] if j == 0 else bias_ref[0, h]
                s = s_t[:, half * BLOCK:(half + 1) * BLOCK] + bias
                sink = sinks_ref[h] * LOG2E
                m = jnp.maximum(jnp.max(s, axis=0, keepdims=True), sink)
                p = jnp.exp2(s - m)
                inv_ls.append(1.0 / (jnp.sum(p, axis=0, keepdims=True) + jnp.exp2(sink - m)))
                p_halves.append(p.astype(BF16))
            p_ts.append(jnp.concatenate(p_halves, axis=1))
        box["p"], box["inv_l"] = p_ts, inv_ls

    def swa_values(j, box):
        o_t = [None] * B_HEADS
        for ui, ((kvh, par), p_t) in enumerate(zip(swa_units, box["p"])):
            drow = slice(kvh * B_HEAD_DIM, (kvh + 1) * B_HEAD_DIM)
            if j == 0:
                vt_h = jnp.concatenate([vth_ref[drow, :], vt_ref[drow, 0:BLOCK]], axis=1)
            else:
                vt_h = vt_ref[drow, (j - 1) * BLOCK:(j + 1) * BLOCK]
            pv = jnp.dot(vt_h, p_t, preferred_element_type=F32)
            for half, h in enumerate((4 * kvh + par, 4 * kvh + 2 + par)):
                o_t[h] = pv[:, half * BLOCK:(half + 1) * BLOCK] * box["inv_l"][2 * ui + half]
        box["o_t"] = o_t

    def swa_store(j, box):
        o_all_t = jnp.concatenate(box["o_t"], axis=0)
        scale = lax.rsqrt(jnp.sum(o_all_t * o_all_t, axis=0, keepdims=True) * (1.0 / B_WIDTH) + EPS)
        o_all = (o_all_t * scale).T
        y_ref[j * BLOCK:(j + 1) * BLOCK, A_WIDTH:] = (o_all * ang_ref[...]).astype(y_ref.dtype)

    group = HGRN_GROUP_CHUNKS
    hgrn = [functools.partial(hgrn_updates, c, c + group) for c in range(0, n_chunk, group)]
    hgrn += [functools.partial(hgrn_scan, h) for h in range(A_HEADS)]
    for p in range(ts // PAIR):
        box = {}
        hgrn += [functools.partial(f, p, box) for f in (hgrn_scores, hgrn_values, hgrn_store)]
    swa = []
    for j in range(n_blk):
        box = {}
        swa += [functools.partial(f, j, box) for f in (swa_scores, swa_softmax, swa_values, swa_store)]
    stages = []
    for k in range(max(len(hgrn), len(swa))):
        stages += hgrn[k:k + 1] + swa[k:k + 1]
    return stages


def _ffn_stages(x_ref, y_ref, wo_ref, g2_ref, wg_ref, wu_ref, wd_ref, gf_ref, o_ref, acc_ref):
    box = {}

    def head():
        h = x_ref[...] + jnp.dot(y_ref[...], wo_ref[...], preferred_element_type=F32)
        box["u"] = (h * _rms_scale(h) * g2_ref[...]).astype(BF16)
        acc_ref[...] = h

    def ff_up(c0):
        cs = slice(c0, c0 + FF_CHUNK)
        box["gate"] = jnp.dot(box["u"], wg_ref[:, cs], preferred_element_type=F32)
        box["up"] = jnp.dot(box["u"], wu_ref[:, cs], preferred_element_type=F32)

    def ff_down(c0):
        act = (_silu(box["gate"]) * box["up"]).astype(BF16)
        acc_ref[...] += jnp.dot(act, wd_ref[c0:c0 + FF_CHUNK, :], preferred_element_type=F32)

    def tail():
        h2 = acc_ref[...]
        o_ref[...] = h2 * _rms_scale(h2) * gf_ref[...]

    stages = [head]
    for c0 in range(0, D_FF, FF_CHUNK):
        stages += [functools.partial(ff_up, c0), functools.partial(ff_down, c0)]
    return stages + [tail]


def _run_interleaved(major, minor):
    span = max(len(major) - 2, 1)
    done = 0
    for k, stage in enumerate(major):
        stage()
        want = min(len(minor), -(-(k + 1) * len(minor) // span))
        while done < want:
            minor[done]()
            done += 1


def _mixer_ffn_kernel(sinks_ref, qd_ref, kit_ref, ks_ref, hv_ref, gate_ref, dec_ref, bq_ref,
                      kp_ref, kph_ref, vt_ref, vth_ref, ang_ref,
                      x_ref, wo_ref, g2_ref, wg_ref, wu_ref, wd_ref, gf_ref, o_ref,
                      state_ref, upd_ref, sprev_ref, bias_ref, y_cur, y_prev, acc_ref,
                      *, n_tiles, n_s):
    i = pl.program_id(0)
    seq_start = lax.rem(i, n_s) == 0

    @pl.when(i == 0)
    def _():
        _bias_table_init(bias_ref)

    @pl.when(seq_start)
    def _():
        state_ref[...] = jnp.zeros_like(state_ref)

    first = jnp.where(seq_start, 1, 0)
    mixer_args = (first, sinks_ref, qd_ref, kit_ref, ks_ref, hv_ref, gate_ref, dec_ref, bq_ref,
                  kp_ref, kph_ref, vt_ref, vth_ref, ang_ref, y_cur,
                  state_ref, upd_ref, sprev_ref, bias_ref)
    ffn_args = (x_ref, y_prev, wo_ref, g2_ref, wg_ref, wu_ref, wd_ref, gf_ref, o_ref, acc_ref)

    @pl.when(i == 0)
    def _():
        _run_interleaved(_mixer_stages(*mixer_args), [])
        y_prev[...] = y_cur[...]

    @pl.when((i > 0) & (i < n_tiles))
    def _():
        _run_interleaved(_ffn_stages(*ffn_args), _mixer_stages(*mixer_args))
        y_prev[...] = y_cur[...]

    @pl.when(i == n_tiles)
    def _():
        _run_interleaved(_ffn_stages(*ffn_args), [])


def _mixer_ffn(prep, sinks, attn_norm_g, x2d, w_out, ln2_g, w_gate, w_up, w_down, final_g, seq):
    qd, kit, ks, hv, gate, dec, bq, kp, vt = prep
    t = x2d.shape[0]
    ts = TS_MIX
    n_tiles = t // ts
    n_s = seq // ts
    n_chunk = ts // CHUNK
    n_blk = ts // BLOCK
    cur = lambda i: jnp.minimum(i, n_tiles - 1)
    prev = lambda i: jnp.maximum(i - 1, 0)
    halo = lambda i: jnp.maximum(cur(i) * n_blk - 1, 0)
    row_blk = lambda width: pl.BlockSpec((ts, width), lambda i: (cur(i), 0))
    const = lambda shape: pl.BlockSpec(shape, lambda i: (0, 0), pipeline_mode=pl.Buffered(1))
    return pl.pallas_call(
        functools.partial(_mixer_ffn_kernel, n_tiles=n_tiles, n_s=n_s),
        grid=(n_tiles + 1,),
        in_specs=[
            pl.BlockSpec(memory_space=pltpu.SMEM),
            row_blk(A_QK), pl.BlockSpec((A_QK, ts), lambda i: (0, cur(i))), row_blk(A_QK),
            row_blk(A_WIDTH), row_blk(A_WIDTH),
            pl.BlockSpec((n_chunk, A_QK), lambda i: (cur(i), 0)),
            row_blk(B_WIDTH),
            pl.BlockSpec((4, ts, LANES), lambda i: (0, cur(i), 0)),
            pl.BlockSpec((4, BLOCK, LANES), lambda i: (0, halo(i), 0)),
            pl.BlockSpec((LANES, ts), lambda i: (0, cur(i))),
            pl.BlockSpec((LANES, BLOCK), lambda i: (0, halo(i))),
            pl.BlockSpec((1, B_WIDTH), lambda i: (0, 0)),
            pl.BlockSpec((ts, D_MODEL), lambda i: (prev(i), 0)),
            const((MIX_WIDTH, D_MODEL)),
            const((1, D_MODEL)),
            const((D_MODEL, D_FF)),
            const((D_MODEL, D_FF)),
            const((D_FF, D_MODEL)),
            const((1, D_MODEL)),
        ],
        out_specs=pl.BlockSpec((ts, D_MODEL), lambda i: (prev(i), 0)),
        out_shape=jax.ShapeDtypeStruct((t, D_MODEL), F32),
        scratch_shapes=[
            pltpu.VMEM((A_HEADS, A_DV, A_DK), F32),
            pltpu.VMEM((n_chunk, A_HEADS, A_DV, A_DK), F32),
            pltpu.VMEM((n_chunk, A_HEADS, A_DV, A_DK), BF16),
            pltpu.VMEM((2, B_HEADS, 2 * BLOCK, BLOCK), F32),
            pltpu.VMEM((ts, MIX_WIDTH), BF16),
            pltpu.VMEM((ts, MIX_WIDTH), BF16),
            pltpu.VMEM((ts, D_MODEL), F32),
        ],
        compiler_params=pltpu.CompilerParams(
            dimension_semantics=("arbitrary",), vmem_limit_bytes=V7X_VMEM_LIMIT_BYTES),
        name="mixer_ffn",
    )(sinks, qd, kit, ks, hv, gate, dec, bq, kp, kp, vt, vt, attn_norm_g,
      x2d, w_out, ln2_g, w_gate, w_up, w_down, final_g)


def kernel(x, ln1_g, w_in, lb_logits, hgrn_norm_g, attn_sinks, attn_norm_g, w_out, ln2_g,
           w_gate, w_up, w_down, final_g):
    bsz, seq, d = x.shape
    x2d = x.reshape(bsz * seq, d)
    *prep, wo16, wg16, wu16, wd16 = _in_proj(x2d, ln1_g[0:1], w_in[0], lb_logits, hgrn_norm_g[0:1],
                                             w_out[0], w_gate[0], w_up[0], w_down[0])
    out = _mixer_ffn(prep, attn_sinks[0], attn_norm_g[0:1], x2d, wo16, ln2_g[0:1],
                     wg16, wu16, wd16, final_g.reshape(1, d), seq)
    return out.reshape(bsz, seq, d)
```

```python
import functools

import jax
import jax.numpy as jnp
from jax import lax
from jax.experimental import pallas as pl
from jax.experimental.pallas import tpu as pltpu

D_MODEL = 1024
A_HEADS = 4
A_DK = 128
A_DV = 128
A_QK = A_HEADS * A_DK
A_WIDTH = A_HEADS * A_DV
CHUNK = 64
B_HEADS = 8
B_KV_HEADS = 2
B_HEAD_DIM = 64
B_GROUP = B_HEADS // B_KV_HEADS
B_WIDTH = B_HEADS * B_HEAD_DIM
B_KV = B_KV_HEADS * B_HEAD_DIM
WINDOW = 128
BLOCK = 128
MIX_WIDTH = A_WIDTH + B_WIDTH
D_FF = 2816
IN_COLS = 4 * A_QK + B_WIDTH + 2 * B_KV
EPS = 1e-6
NEG_BIG = -1e30
LANES = 128

F32 = jnp.float32
BF16 = jnp.bfloat16

V7X_VMEM_LIMIT_BYTES = 56 * 1024 * 1024

TM_PROJ = 512
TS_MIX = 512
FF_CHUNK = 256
HGRN_GROUP_CHUNKS = 2
PAIR = 2 * CHUNK
CAST_ROWS = 128
LOG2E = 1.4426950408889634

NT_DIMS = (((1,), (1,)), ((), ()))
TN_DIMS = (((0,), (0,)), ((), ()))


def _rms_scale(v):
    return lax.rsqrt(jnp.mean(v * v, axis=-1, keepdims=True) + EPS)


def _silu(v):
    return v * (1.0 / (1.0 + jnp.exp(-v)))


def _sigmoid(v):
    return 1.0 / (1.0 + jnp.exp(-v))


def _in_proj_kernel(x_ref, g_ref, w32_ref, lbl_ref, hng_ref, wo32_ref, wg32_ref, wu32_ref, wd32_ref,
                    qd_ref, kit_ref, ks_ref, hv_ref, gate_ref, dec_ref, bq_ref, kp_ref, vt_ref,
                    wo16_ref, wg16_ref, wu16_ref, wd16_ref, w_ref, *, n_down_slabs):
    tm = x_ref.shape[0]

    @pl.when(pl.program_id(0) == 0)
    def _():
        for r0 in range(0, D_MODEL, CAST_ROWS):
            w_ref[r0:r0 + CAST_ROWS, :] = w32_ref[r0:r0 + CAST_ROWS, :].astype(BF16)

    @pl.when(pl.program_id(0) < n_down_slabs)
    def _():
        wd16_ref[...] = wd32_ref[...].astype(BF16)

    x = x_ref[...]
    u = (x * _rms_scale(x) * g_ref[...]).astype(BF16)

    lbl = lbl_ref[...]
    lmax = jnp.max(lbl, axis=0, keepdims=True)
    le = jnp.exp(lbl - lmax)
    lb = le[0:1, :] / jnp.sum(le, axis=0, keepdims=True)

    r = lax.broadcasted_iota(jnp.int32, (CHUNK, CHUNK), 0)
    c = lax.broadcasted_iota(jnp.int32, (CHUNK, CHUNK), 1)
    tri = jnp.where(r >= c, 1.0, 0.0).astype(BF16)

    pf = jnp.dot(u, w_ref[:, A_QK:2 * A_QK], preferred_element_type=F32)
    pq = jnp.dot(u, w_ref[:, 0:A_QK], preferred_element_type=F32)
    k_pair = []
    for ci in range(tm // CHUNK):
        rows = slice(ci * CHUNK, (ci + 1) * CHUNK)
        forget = lb + (1.0 - lb) * _sigmoid(pf[rows])
        key = 1.0 - forget
        logf = jnp.log(forget)
        p0 = logf.astype(BF16)
        r1 = logf - p0.astype(F32)
        p1 = r1.astype(BF16)
        p2 = (r1 - p1.astype(F32)).astype(BF16)
        bc3 = jnp.dot(tri, jnp.concatenate([p0, p1, p2], axis=-1), preferred_element_type=F32)
        bcum = bc3[:, :A_QK] + bc3[:, A_QK:2 * A_QK] + bc3[:, 2 * A_QK:]
        decay = jnp.exp(bcum[CHUNK - 1:CHUNK, :])
        k_intra = key * jnp.exp(-bcum)
        qd_ref[rows, :] = (_silu(pq[rows]) * jnp.exp(bcum)).astype(BF16)
        ks_ref[rows, :] = (k_intra * decay).astype(BF16)
        dec_ref[ci:ci + 1, :] = decay
        k_pair.append(k_intra)
        if len(k_pair) == PAIR // CHUNK:
            p0_row = (ci + 1) * CHUNK - PAIR
            kit_ref[:, p0_row:p0_row + PAIR] = jnp.concatenate(k_pair, axis=0).T.astype(BF16)
            k_pair = []

    wo16_ref[...] = wo32_ref[...].astype(BF16)
    wg16_ref[...] = wg32_ref[...].astype(BF16)
    wu16_ref[...] = wu32_ref[...].astype(BF16)

    c0 = 2 * A_QK + 2 * A_WIDTH
    pa = jnp.dot(u, w_ref[:, c0:], preferred_element_type=F32)
    bq_ref[...] = (pa[:, :B_WIDTH] * (B_HEAD_DIM ** -0.5 * LOG2E)).astype(BF16)
    lo = lax.broadcasted_iota(jnp.int32, (tm, LANES), 1) < B_HEAD_DIM
    k_all = pa[:, B_WIDTH:B_WIDTH + B_KV]
    k_rot = pltpu.roll(k_all, B_HEAD_DIM, axis=1)
    kp_ref[0] = jnp.where(lo, k_all, 0.0).astype(BF16)
    kp_ref[1] = jnp.where(lo, 0.0, k_rot).astype(BF16)
    kp_ref[2] = jnp.where(lo, k_rot, 0.0).astype(BF16)
    kp_ref[3] = jnp.where(lo, 0.0, k_all).astype(BF16)
    vt_ref[...] = pa[:, B_WIDTH + B_KV:].T.astype(BF16)

    pg = jnp.dot(u, w_ref[:, 2 * A_QK + A_WIDTH:2 * A_QK + 2 * A_WIDTH], preferred_element_type=F32)
    gate_ref[...] = (_silu(pg) * hng_ref[...]).astype(BF16)
    pi = jnp.dot(u, w_ref[:, 2 * A_QK:2 * A_QK + A_WIDTH], preferred_element_type=F32)
    hv_ref[...] = pi.astype(BF16)


def _in_proj(x2d, ln1_g, w_in, lb_logits, hgrn_norm_g, w_out, w_gate, w_up, w_down):
    t = x2d.shape[0]
    tm = TM_PROJ
    n_steps = t // tm
    row_blk = lambda width: pl.BlockSpec((tm, width), lambda i: (i, 0))
    const = lambda shape, **kw: pl.BlockSpec(shape, lambda i: (0, 0), **kw)
    bf = lambda shape: jax.ShapeDtypeStruct(shape, BF16)

    def slab(w, rows):
        n_slabs = w.shape[0] // rows
        assert w.shape[0] % rows == 0 and n_slabs <= n_steps, (w.shape, rows)
        return pl.BlockSpec((rows, w.shape[1]), lambda i: (jnp.minimum(i, n_slabs - 1), 0))

    cast_weights = (w_out, w_gate, w_up, w_down)
    even_rows = D_MODEL // n_steps
    slab_rows = (even_rows, even_rows, even_rows, CAST_ROWS)
    slab_specs = lambda: [slab(w, rows) for w, rows in zip(cast_weights, slab_rows)]
    return pl.pallas_call(
        functools.partial(_in_proj_kernel, n_down_slabs=D_FF // CAST_ROWS),
        grid=(n_steps,),
        in_specs=[
            row_blk(D_MODEL),
            const((1, D_MODEL)),
            const((D_MODEL, IN_COLS), pipeline_mode=pl.Buffered(1)),
            const((2, A_QK)),
            const((1, A_WIDTH)),
        ] + slab_specs(),
        out_specs=[
            row_blk(A_QK), pl.BlockSpec((A_QK, tm), lambda i: (0, i)), row_blk(A_QK),
            row_blk(A_WIDTH), row_blk(A_WIDTH),
            pl.BlockSpec((tm // CHUNK, A_QK), lambda i: (i, 0)),
            row_blk(B_WIDTH),
            pl.BlockSpec((4, tm, LANES), lambda i: (0, i, 0)),
            pl.BlockSpec((LANES, tm), lambda i: (0, i)),
        ] + slab_specs(),
        out_shape=[
            bf((t, A_QK)), bf((A_QK, t)), bf((t, A_QK)), bf((t, A_WIDTH)), bf((t, A_WIDTH)),
            jax.ShapeDtypeStruct((t // CHUNK, A_QK), F32),
            bf((t, B_WIDTH)),
            bf((4, t, LANES)),
            bf((LANES, t)),
        ] + [bf(w.shape) for w in cast_weights],
        scratch_shapes=[pltpu.VMEM((D_MODEL, IN_COLS), BF16)],
        compiler_params=pltpu.CompilerParams(
            dimension_semantics=("arbitrary",), vmem_limit_bytes=V7X_VMEM_LIMIT_BYTES),
        name="in_proj",
    )(x2d, ln1_g, w_in, lb_logits, hgrn_norm_g, *cast_weights)


def _bias_table_init(bias_ref):
    kj = lax.broadcasted_iota(jnp.int32, (2 * BLOCK, BLOCK), 0)
    qi = lax.broadcasted_iota(jnp.int32, (2 * BLOCK, BLOCK), 1)
    dist = qi + BLOCK - kj
    valid = (dist >= 0) & (dist < WINDOW)
    valid_first = valid & (kj >= BLOCK)
    distf = dist.astype(F32)
    for h in range(B_HEADS):
        slope = 2.0 ** (-(8.0 / B_HEADS) * (h + 1))
        sc = (-slope * LOG2E) * distf
        bias_ref[0, h] = jnp.where(valid, sc, NEG_BIG)
        bias_ref[1, h] = jnp.where(valid_first, sc, NEG_BIG)


def _mixer_stages(first, sinks_ref, qd_ref, kit_ref, ks_ref, hv_ref, gate_ref, dec_ref, bq_ref,
                  kp_ref, kph_ref, vt_ref, vth_ref, ang_ref, y_ref,
                  state_ref, upd_ref, sprev_ref, bias_ref):
    ts = qd_ref.shape[0]
    n_chunk = ts // CHUNK
    n_blk = ts // BLOCK
    head_cols = [slice(h * A_DK, (h + 1) * A_DK) for h in range(A_HEADS)]

    def chunk_units(c_lo, c_hi):
        return [(ci, slice(ci * CHUNK, (ci + 1) * CHUNK), head_cols[h], h)
                for ci in range(c_lo, c_hi) for h in range(A_HEADS)]

    def hgrn_updates(c_lo, c_hi):
        units = chunk_units(c_lo, c_hi)
        upds = [lax.dot_general(hv_ref[rows, cs], ks_ref[rows, cs], TN_DIMS,
                                preferred_element_type=F32) for (_, rows, cs, _) in units]
        for upd, (ci, _, _, h) in zip(upds, units):
            upd_ref[ci, h] = upd

    def hgrn_scan(h):
        st = state_ref[h]
        for ci in range(n_chunk):
            sprev_ref[ci, h] = st.T.astype(BF16)
            st = st * dec_ref[ci:ci + 1, head_cols[h]] + upd_ref[ci, h]
        state_ref[h] = st

    def hgrn_scores(p, box):
        rows = slice(p * PAIR, (p + 1) * PAIR)
        box["a"] = [jnp.dot(qd_ref[rows, cs], kit_ref[cs, rows], preferred_element_type=F32)
                    for cs in head_cols]
        box["o_inter"] = [
            jnp.concatenate(
                [jnp.dot(qd_ref[ci * CHUNK:(ci + 1) * CHUNK, cs], sprev_ref[ci, h],
                         preferred_element_type=F32)
                 for ci in range(p * PAIR // CHUNK, (p + 1) * PAIR // CHUNK)], axis=0)
            for h, cs in enumerate(head_cols)]

    def hgrn_values(p, box):
        rows = slice(p * PAIR, (p + 1) * PAIR)
        r = lax.broadcasted_iota(jnp.int32, (PAIR, PAIR), 0)
        c = lax.broadcasted_iota(jnp.int32, (PAIR, PAIR), 1)
        same_chunk_causal = (c <= r) & (c >= (r // CHUNK) * CHUNK)
        a_s = [jnp.where(same_chunk_causal, a, 0.0).astype(BF16) for a in box["a"]]
        box["o"] = [jnp.dot(a, hv_ref[rows, cs], preferred_element_type=F32) + oi
                    for a, oi, cs in zip(a_s, box["o_inter"], head_cols)]

    def hgrn_store(p, box):
        rows = slice(p * PAIR, (p + 1) * PAIR)
        for o, cs in zip(box["o"], head_cols):
            gate = gate_ref[rows, cs].astype(F32)
            y_ref[rows, cs] = (o * _rms_scale(o) * gate).astype(y_ref.dtype)

    swa_units = [(kvh, par) for kvh in range(B_KV_HEADS) for par in range(2)]

    def swa_scores(j, box):
        s_ts = []
        for kvh, par in swa_units:
            rows = slice(j * BLOCK, (j + 1) * BLOCK)
            q_pairs = jnp.concatenate(
                [bq_ref[rows, (2 * kvh) * LANES:(2 * kvh + 1) * LANES],
                 bq_ref[rows, (2 * kvh + 1) * LANES:(2 * kvh + 2) * LANES]], axis=0)
            idx = 2 * kvh + par
            if j == 0:
                kk = jnp.concatenate([kph_ref[idx], kp_ref[idx, 0:BLOCK, :]], axis=0)
            else:
                kk = kp_ref[idx, (j - 1) * BLOCK:(j + 1) * BLOCK, :]
            s_ts.append(lax.dot_general(kk, q_pairs, NT_DIMS, preferred_element_type=F32))
        box["s"] = s_ts

    def swa_softmax(j, box):
        p_ts, inv_ls = [], []
        for (kvh, par), s_t in zip(swa_units, box["s"]):
            p_halves = []
            for half, h in enumerate((4 * kvh + par, 4 * kvh + 2 + par)):
                bias = bias_ref[first, h] if j == 0 else bias_ref[0, h]
                s = s_t[:, half * BLOCK:(half + 1) * BLOCK] + bias
                sink = sinks_ref[h] * LOG2E
                m = jnp.maximum(jnp.max(s, axis=0, keepdims=True), sink)
                p = jnp.exp2(s - m)
                inv_ls.append(1.0 / (jnp.sum(p, axis=0, keepdims=True) + jnp.exp2(sink - m)))
                p_halves.append(p.astype(BF16))
            p_ts.append(jnp.concatenate(p_halves, axis=1))
        box["p"], box["inv_l"] = p_ts, inv_ls

    def swa_values(j, box):
        o_t = [None] * B_HEADS
        for ui, ((kvh, par), p_t) in enumerate(zip(swa_units, box["p"])):
            drow = slice(kvh * B_HEAD_DIM, (kvh + 1) * B_HEAD_DIM)
            if j == 0:
                vt_h = jnp.concatenate([vth_ref[drow, :], vt_ref[drow, 0:BLOCK]], axis=1)
            else:
                vt_h = vt_ref[drow, (j - 1) * BLOCK:(j + 1) * BLOCK]
            pv = jnp.dot(vt_h, p_t, preferred_element_type=F32)
            for half, h in enumerate((4 * kvh + par, 4 * kvh + 2 + par)):
                o_t[h] = pv[:, half * BLOCK:(half + 1) * BLOCK] * box["inv_l"][2 * ui + half]
        box["o_t"] = o_t

    def swa_store(j, box):
        o_all_t = jnp.concatenate(box["o_t"], axis=0)
        scale = lax.rsqrt(jnp.sum(o_all_t * o_all_t, axis=0, keepdims=True) * (1.0 / B_WIDTH) + EPS)
        o_all = (o_all_t * scale).T
        y_ref[j * BLOCK:(j + 1) * BLOCK, A_WIDTH:] = (o_all * ang_ref[...]).astype(y_ref.dtype)

    group = HGRN_GROUP_CHUNKS
    hgrn = [functools.partial(hgrn_updates, c, c + group) for c in range(0, n_chunk, group)]
    hgrn += [functools.partial(hgrn_scan, h) for h in range(A_HEADS)]
    for p in range(ts // PAIR):
        box = {}
        hgrn += [functools.partial(f, p, box) for f in (hgrn_scores, hgrn_values, hgrn_store)]
    swa = []
    for j in range(n_blk):
        box = {}
        swa += [functools.partial(f, j, box) for f in (swa_scores, swa_softmax, swa_values, swa_store)]
    stages = []
    for k in range(max(len(hgrn), len(swa))):
        stages += hgrn[k:k + 1] + swa[k:k + 1]
    return stages


def _ffn_stages(x_ref, y_ref, wo_ref, g2_ref, wg_ref, wu_ref, wd_ref, gf_ref, o_ref,
                h_ref, act_ref):
    box = {}

    def head():
        h = x_ref[...] + jnp.dot(y_ref[...], wo_ref[...], preferred_element_type=F32)
        box["u"] = (h * _rms_scale(h) * g2_ref[...]).astype(BF16)
        h_ref[...] = h

    def ff_up(c0):
        cs = slice(c0, c0 + FF_CHUNK)
        box["gate"] = jnp.dot(box["u"], wg_ref[:, cs], preferred_element_type=F32)
        box["up"] = jnp.dot(box["u"], wu_ref[:, cs], preferred_element_type=F32)

    def ff_act(c0):
        act_ref[:, c0:c0 + FF_CHUNK] = (_silu(box["gate"]) * box["up"]).astype(BF16)

    def tail():
        h2 = h_ref[...] + jnp.dot(act_ref[...], wd_ref[...], preferred_element_type=F32)
        o_ref[...] = h2 * _rms_scale(h2) * gf_ref[...]

    stages = [head]
    for c0 in range(0, D_FF, FF_CHUNK):
        stages += [functools.partial(ff_up, c0), functools.partial(ff_act, c0)]
    return stages + [tail]


def _run_interleaved(major, minor):
    span = max(len(major) - 2, 1)
    done = 0
    for k, stage in enumerate(major):
        stage()
        want = min(len(minor), -(-(k + 1) * len(minor) // span))
        while done < want:
            minor[done]()
            done += 1


def _mixer_ffn_kernel(sinks_ref, qd_ref, kit_ref, ks_ref, hv_ref, gate_ref, dec_ref, bq_ref,
                      kp_ref, kph_ref, vt_ref, vth_ref, ang_ref,
                      x_ref, wo_ref, g2_ref, wg_ref, wu_ref, wd_ref, gf_ref, o_ref,
                      state_ref, upd_ref, sprev_ref, bias_ref, y_cur, y_prev, h_ref, act_ref,
                      *, n_tiles, n_s):
    i = pl.program_id(0)
    seq_start = lax.rem(i, n_s) == 0

    @pl.when(i == 0)
    def _():
        _bias_table_init(bias_ref)

    @pl.when(seq_start)
    def _():
        state_ref[...] = jnp.zeros_like(state_ref)

    first = jnp.where(seq_start, 1, 0)
    mixer_args = (first, sinks_ref, qd_ref, kit_ref, ks_ref, hv_ref, gate_ref, dec_ref, bq_ref,
                  kp_ref, kph_ref, vt_ref, vth_ref, ang_ref, y_cur,
                  state_ref, upd_ref, sprev_ref, bias_ref)
    ffn_args = (x_ref, y_prev, wo_ref, g2_ref, wg_ref, wu_ref, wd_ref, gf_ref, o_ref,
                h_ref, act_ref)

    @pl.when(i == 0)
    def _():
        _run_interleaved(_mixer_stages(*mixer_args), [])
        y_prev[...] = y_cur[...]

    @pl.when((i > 0) & (i < n_tiles))
    def _():
        _run_interleaved(_ffn_stages(*ffn_args), _mixer_stages(*mixer_args))
        y_prev[...] = y_cur[...]

    @pl.when(i == n_tiles)
    def _():
        _run_interleaved(_ffn_stages(*ffn_args), [])


def _mixer_ffn(prep, sinks, attn_norm_g, x2d, w_out, ln2_g, w_gate, w_up, w_down, final_g, seq):
    qd, kit, ks, hv, gate, dec, bq, kp, vt = prep
    t = x2d.shape[0]
    ts = TS_MIX
    n_tiles = t // ts
    n_s = seq // ts
    n_chunk = ts // CHUNK
    n_blk = ts // BLOCK
    cur = lambda i: jnp.minimum(i, n_tiles - 1)
    prev = lambda i: jnp.maximum(i - 1, 0)
    halo = lambda i: jnp.maximum(cur(i) * n_blk - 1, 0)
    row_blk = lambda width: pl.BlockSpec((ts, width), lambda i: (cur(i), 0))
    const = lambda shape: pl.BlockSpec(shape, lambda i: (0, 0), pipeline_mode=pl.Buffered(1))
    return pl.pallas_call(
        functools.partial(_mixer_ffn_kernel, n_tiles=n_tiles, n_s=n_s),
        grid=(n_tiles + 1,),
        in_specs=[
            pl.BlockSpec(memory_space=pltpu.SMEM),
            row_blk(A_QK), pl.BlockSpec((A_QK, ts), lambda i: (0, cur(i))), row_blk(A_QK),
            row_blk(A_WIDTH), row_blk(A_WIDTH),
            pl.BlockSpec((n_chunk, A_QK), lambda i: (cur(i), 0)),
            row_blk(B_WIDTH),
            pl.BlockSpec((4, ts, LANES), lambda i: (0, cur(i), 0)),
            pl.BlockSpec((4, BLOCK, LANES), lambda i: (0, halo(i), 0)),
            pl.BlockSpec((LANES, ts), lambda i: (0, cur(i))),
            pl.BlockSpec((LANES, BLOCK), lambda i: (0, halo(i))),
            pl.BlockSpec((1, B_WIDTH), lambda i: (0, 0)),
            pl.BlockSpec((ts, D_MODEL), lambda i: (prev(i), 0)),
            const((MIX_WIDTH, D_MODEL)),
            const((1, D_MODEL)),
            const((D_MODEL, D_FF)),
            const((D_MODEL, D_FF)),
            const((D_FF, D_MODEL)),
            const((1, D_MODEL)),
        ],
        out_specs=pl.BlockSpec((ts, D_MODEL), lambda i: (prev(i), 0)),
        out_shape=jax.ShapeDtypeStruct((t, D_MODEL), F32),
        scratch_shapes=[
            pltpu.VMEM((A_HEADS, A_DV, A_DK), F32),
            pltpu.VMEM((n_chunk, A_HEADS, A_DV, A_DK), F32),
            pltpu.VMEM((n_chunk, A_HEADS, A_DV, A_DK), BF16),
            pltpu.VMEM((2, B_HEADS, 2 * BLOCK, BLOCK), F32),
            pltpu.VMEM((ts, MIX_WIDTH), BF16),
            pltpu.VMEM((ts, MIX_WIDTH), BF16),
            pltpu.VMEM((ts, D_MODEL), F32),
            pltpu.VMEM((ts, D_FF), BF16),
        ],
        compiler_params=pltpu.CompilerParams(
            dimension_semantics=("arbitrary",), vmem_limit_bytes=V7X_VMEM_LIMIT_BYTES),
        name="mixer_ffn",
    )(sinks, qd, kit, ks, hv, gate, dec, bq, kp, kp, vt, vt, attn_norm_g,
      x2d, w_out, ln2_g, w_gate, w_up, w_down, final_g)


def kernel(x, ln1_g, w_in, lb_logits, hgrn_norm_g, attn_sinks, attn_norm_g, w_out, ln2_g,
           w_gate, w_up, w_down, final_g):
    bsz, seq, d = x.shape
    x2d = x.reshape(bsz * seq, d)
    *prep, wo16, wg16, wu16, wd16 = _in_proj(x2d, ln1_g[0:1], w_in[0], lb_logits, hgrn_norm_g[0:1],
                                             w_out[0], w_gate[0], w_up[0], w_down[0])
    out = _mixer_ffn(prep, attn_sinks[0], attn_norm_g[0:1], x2d, wo16, ln2_g[0:1],
                     wg16, wu16, wd16, final_g.reshape(1, d), seq)
    return out.reshape(bsz, seq, d)
```

```python
import functools

import jax
import jax.numpy as jnp
from jax import lax
from jax.experimental import pallas as pl
from jax.experimental.pallas import tpu as pltpu

D_MODEL = 1024
A_HEADS = 4
A_DK = 128
A_DV = 128
A_QK = A_HEADS * A_DK
A_WIDTH = A_HEADS * A_DV
CHUNK = 64
B_HEADS = 8
B_KV_HEADS = 2
B_HEAD_DIM = 64
B_GROUP = B_HEADS // B_KV_HEADS
B_WIDTH = B_HEADS * B_HEAD_DIM
B_KV = B_KV_HEADS * B_HEAD_DIM
WINDOW = 128
BLOCK = 128
MIX_WIDTH = A_WIDTH + B_WIDTH
D_FF = 2816
IN_COLS = 4 * A_QK + B_WIDTH + 2 * B_KV
EPS = 1e-6
NEG_BIG = -1e30
LANES = 128
SUBLANES = 8

F32 = jnp.float32
BF16 = jnp.bfloat16

V7X_VMEM_LIMIT_BYTES = 56 * 1024 * 1024

TM_PROJ = 512
TS_MIX = 512
FF_CHUNK = 256
HGRN_GROUP_CHUNKS = 2
PAIR = 2 * CHUNK
CAST_ROWS = 128
LOG2E = 1.4426950408889634

NT_DIMS = (((1,), (1,)), ((), ()))
TN_DIMS = (((0,), (0,)), ((), ()))


def _rms_scale(v):
    return lax.rsqrt(jnp.mean(v * v, axis=-1, keepdims=True) + EPS)


def _silu(v):
    return v * (1.0 / (1.0 + jnp.exp(-v)))


def _sigmoid(v):
    return 1.0 / (1.0 + jnp.exp(-v))


def _chunk_cumsum(v):
    rows, n = v.shape
    groups = rows // SUBLANES
    g = v.reshape(groups, SUBLANES, n)
    sub = lax.broadcasted_iota(jnp.int32, g.shape, 1)
    shift = 1
    while shift < SUBLANES:
        g = g + jnp.where(sub >= shift, pltpu.roll(g, shift, axis=1), 0.0)
        shift *= 2
    out, offset = [], None
    for k in range(groups):
        blk = g[k] if offset is None else g[k] + offset
        out.append(blk)
        offset = blk[SUBLANES - 1:SUBLANES, :]
    return jnp.concatenate(out, axis=0)


def _in_proj_kernel(x_ref, g_ref, w32_ref, lbl_ref, hng_ref, wo32_ref, wg32_ref, wu32_ref, wd32_ref,
                    qd_ref, kit_ref, ks_ref, hv_ref, gate_ref, dec_ref, bq_ref, kp_ref, vt_ref,
                    wo16_ref, wg16_ref, wu16_ref, wd16_ref, w_ref, *, n_down_slabs):
    tm = x_ref.shape[0]

    @pl.when(pl.program_id(0) == 0)
    def _():
        for r0 in range(0, D_MODEL, CAST_ROWS):
            w_ref[r0:r0 + CAST_ROWS, :] = w32_ref[r0:r0 + CAST_ROWS, :].astype(BF16)

    @pl.when(pl.program_id(0) < n_down_slabs)
    def _():
        wd16_ref[...] = wd32_ref[...].astype(BF16)

    x = x_ref[...]
    u = (x * _rms_scale(x) * g_ref[...]).astype(BF16)

    lbl = lbl_ref[...]
    lmax = jnp.max(lbl, axis=0, keepdims=True)
    le = jnp.exp(lbl - lmax)
    lb = le[0:1, :] / jnp.sum(le, axis=0, keepdims=True)

    pf = jnp.dot(u, w_ref[:, A_QK:2 * A_QK], preferred_element_type=F32)
    pq = jnp.dot(u, w_ref[:, 0:A_QK], preferred_element_type=F32)
    k_pair = []
    for ci in range(tm // CHUNK):
        rows = slice(ci * CHUNK, (ci + 1) * CHUNK)
        forget = lb + (1.0 - lb) * _sigmoid(pf[rows])
        key = 1.0 - forget
        logf = jnp.log(forget)
        bcum = _chunk_cumsum(logf)
        decay = jnp.exp(bcum[CHUNK - 1:CHUNK, :])
        k_intra = key * jnp.exp(-bcum)
        qd_ref[rows, :] = (_silu(pq[rows]) * jnp.exp(bcum)).astype(BF16)
        ks_ref[rows, :] = (k_intra * decay).astype(BF16)
        dec_ref[ci:ci + 1, :] = decay
        k_pair.append(k_intra)
        if len(k_pair) == PAIR // CHUNK:
            p0_row = (ci + 1) * CHUNK - PAIR
            kit_ref[:, p0_row:p0_row + PAIR] = jnp.concatenate(k_pair, axis=0).T.astype(BF16)
            k_pair = []

    wo16_ref[...] = wo32_ref[...].astype(BF16)
    wg16_ref[...] = wg32_ref[...].astype(BF16)
    wu16_ref[...] = wu32_ref[...].astype(BF16)

    c0 = 2 * A_QK + 2 * A_WIDTH
    pa = jnp.dot(u, w_ref[:, c0:], preferred_element_type=F32)
    bq_ref[...] = (pa[:, :B_WIDTH] * (B_HEAD_DIM ** -0.5 * LOG2E)).astype(BF16)
    lo = lax.broadcasted_iota(jnp.int32, (tm, LANES), 1) < B_HEAD_DIM
    k_all = pa[:, B_WIDTH:B_WIDTH + B_KV]
    k_rot = pltpu.roll(k_all, B_HEAD_DIM, axis=1)
    kp_ref[0] = jnp.where(lo, k_all, 0.0).astype(BF16)
    kp_ref[1] = jnp.where(lo, 0.0, k_rot).astype(BF16)
    kp_ref[2] = jnp.where(lo, k_rot, 0.0).astype(BF16)
    kp_ref[3] = jnp.where(lo, 0.0, k_all).astype(BF16)
    vt_ref[...] = pa[:, B_WIDTH + B_KV:].T.astype(BF16)

    pg = jnp.dot(u, w_ref[:, 2 * A_QK + A_WIDTH:2 * A_QK + 2 * A_WIDTH], preferred_element_type=F32)
    gate_ref[...] = (_silu(pg) * hng_ref[...]).astype(BF16)
    pi = jnp.dot(u, w_ref[:, 2 * A_QK:2 * A_QK + A_WIDTH], preferred_element_type=F32)
    hv_ref[...] = pi.astype(BF16)


def _in_proj(x2d, ln1_g, w_in, lb_logits, hgrn_norm_g, w_out, w_gate, w_up, w_down):
    t = x2d.shape[0]
    tm = TM_PROJ
    n_steps = t // tm
    row_blk = lambda width: pl.BlockSpec((tm, width), lambda i: (i, 0))
    const = lambda shape, **kw: pl.BlockSpec(shape, lambda i: (0, 0), **kw)
    bf = lambda shape: jax.ShapeDtypeStruct(shape, BF16)

    def slab(w, rows):
        n_slabs = w.shape[0] // rows
        assert w.shape[0] % rows == 0 and n_slabs <= n_steps, (w.shape, rows)
        return pl.BlockSpec((rows, w.shape[1]), lambda i: (jnp.minimum(i, n_slabs - 1), 0))

    cast_weights = (w_out, w_gate, w_up, w_down)
    even_rows = D_MODEL // n_steps
    slab_rows = (even_rows, even_rows, even_rows, CAST_ROWS)
    slab_specs = lambda: [slab(w, rows) for w, rows in zip(cast_weights, slab_rows)]
    return pl.pallas_call(
        functools.partial(_in_proj_kernel, n_down_slabs=D_FF // CAST_ROWS),
        grid=(n_steps,),
        in_specs=[
            row_blk(D_MODEL),
            const((1, D_MODEL)),
            const((D_MODEL, IN_COLS), pipeline_mode=pl.Buffered(1)),
            const((2, A_QK)),
            const((1, A_WIDTH)),
        ] + slab_specs(),
        out_specs=[
            row_blk(A_QK), pl.BlockSpec((A_QK, tm), lambda i: (0, i)), row_blk(A_QK),
            row_blk(A_WIDTH), row_blk(A_WIDTH),
            pl.BlockSpec((tm // CHUNK, A_QK), lambda i: (i, 0)),
            row_blk(B_WIDTH),
            pl.BlockSpec((4, tm, LANES), lambda i: (0, i, 0)),
            pl.BlockSpec((LANES, tm), lambda i: (0, i)),
        ] + slab_specs(),
        out_shape=[
            bf((t, A_QK)), bf((A_QK, t)), bf((t, A_QK)), bf((t, A_WIDTH)), bf((t, A_WIDTH)),
            jax.ShapeDtypeStruct((t // CHUNK, A_QK), F32),
            bf((t, B_WIDTH)),
            bf((4, t, LANES)),
            bf((LANES, t)),
        ] + [bf(w.shape) for w in cast_weights],
        scratch_shapes=[pltpu.VMEM((D_MODEL, IN_COLS), BF16)],
        compiler_params=pltpu.CompilerParams(
            dimension_semantics=("arbitrary",), vmem_limit_bytes=V7X_VMEM_LIMIT_BYTES),
        name="in_proj",
    )(x2d, ln1_g, w_in, lb_logits, hgrn_norm_g, *cast_weights)


def _bias_table_init(bias_ref):
    kj = lax.broadcasted_iota(jnp.int32, (2 * BLOCK, BLOCK), 0)
    qi = lax.broadcasted_iota(jnp.int32, (2 * BLOCK, BLOCK), 1)
    dist = qi + BLOCK - kj
    valid = (dist >= 0) & (dist < WINDOW)
    valid_first = valid & (kj >= BLOCK)
    distf = dist.astype(F32)
    for h in range(B_HEADS):
        slope = 2.0 ** (-(8.0 / B_HEADS) * (h + 1))
        sc = (-slope * LOG2E) * distf
        bias_ref[0, h] = jnp.where(valid, sc, NEG_BIG)
        bias_ref[1, h] = jnp.where(valid_first, sc, NEG_BIG)


def _mixer_stages(first, sinks_ref, qd_ref, kit_ref, ks_ref, hv_ref, gate_ref, dec_ref, bq_ref,
                  kp_ref, kph_ref, vt_ref, vth_ref, ang_ref, y_ref,
                  state_ref, upd_ref, sprev_ref, bias_ref):
    ts = qd_ref.shape[0]
    n_chunk = ts // CHUNK
    n_blk = ts // BLOCK
    head_cols = [slice(h * A_DK, (h + 1) * A_DK) for h in range(A_HEADS)]

    def chunk_units(c_lo, c_hi):
        return [(ci, slice(ci * CHUNK, (ci + 1) * CHUNK), head_cols[h], h)
                for ci in range(c_lo, c_hi) for h in range(A_HEADS)]

    def hgrn_updates(c_lo, c_hi):
        units = chunk_units(c_lo, c_hi)
        upds = [lax.dot_general(hv_ref[rows, cs], ks_ref[rows, cs], TN_DIMS,
                                preferred_element_type=F32) for (_, rows, cs, _) in units]
        for upd, (ci, _, _, h) in zip(upds, units):
            upd_ref[ci, h] = upd

    def hgrn_scan(h):
        st = state_ref[h]
        for ci in range(n_chunk):
            sprev_ref[ci, h] = st.T.astype(BF16)
            st = st * dec_ref[ci:ci + 1, head_cols[h]] + upd_ref[ci, h]
        state_ref[h] = st

    def hgrn_scores(p, box):
        rows = slice(p * PAIR, (p + 1) * PAIR)
        box["a"] = [jnp.dot(qd_ref[rows, cs], kit_ref[cs, rows], preferred_element_type=F32)
                    for cs in head_cols]
        box["o_inter"] = [
            jnp.concatenate(
                [jnp.dot(qd_ref[ci * CHUNK:(ci + 1) * CHUNK, cs], sprev_ref[ci, h],
                         preferred_element_type=F32)
                 for ci in range(p * PAIR // CHUNK, (p + 1) * PAIR // CHUNK)], axis=0)
            for h, cs in enumerate(head_cols)]

    def hgrn_values(p, box):
        rows = slice(p * PAIR, (p + 1) * PAIR)
        r = lax.broadcasted_iota(jnp.int32, (PAIR, PAIR), 0)
        c = lax.broadcasted_iota(jnp.int32, (PAIR, PAIR), 1)
        same_chunk_causal = (c <= r) & (c >= (r // CHUNK) * CHUNK)
        a_s = [jnp.where(same_chunk_causal, a, 0.0).astype(BF16) for a in box["a"]]
        box["o"] = [jnp.dot(a, hv_ref[rows, cs], preferred_element_type=F32) + oi
                    for a, oi, cs in zip(a_s, box["o_inter"], head_cols)]

    def hgrn_store(p, box):
        rows = slice(p * PAIR, (p + 1) * PAIR)
        for o, cs in zip(box["o"], head_cols):
            gate = gate_ref[rows, cs].astype(F32)
            y_ref[rows, cs] = (o * _rms_scale(o) * gate).astype(y_ref.dtype)

    swa_units = [(kvh, par) for kvh in range(B_KV_HEADS) for par in range(2)]

    def swa_scores(j, box):
        s_ts = []
        for kvh, par in swa_units:
            rows = slice(j * BLOCK, (j + 1) * BLOCK)
            q_pairs = jnp.concatenate(
                [bq_ref[rows, (2 * kvh) * LANES:(2 * kvh + 1) * LANES],
                 bq_ref[rows, (2 * kvh + 1) * LANES:(2 * kvh + 2) * LANES]], axis=0)
            idx = 2 * kvh + par
            if j == 0:
                kk = jnp.concatenate([kph_ref[idx], kp_ref[idx, 0:BLOCK, :]], axis=0)
            else:
                kk = kp_ref[idx, (j - 1) * BLOCK:(j + 1) * BLOCK, :]
            s_ts.append(lax.dot_general(kk, q_pairs, NT_DIMS, preferred_element_type=F32))
        box["s"] = s_ts

    def swa_softmax(j, box):
        p_ts, inv_ls = [], []
        for (kvh, par), s_t in zip(swa_units, box["s"]):
            p_halves = []
            for half, h in enumerate((4 * kvh + par, 4 * kvh + 2 + par)):
                bias = bias_ref[first, h] if j == 0 else bias_ref[0, h]
                s = s_t[:, half * BLOCK:(half + 1) * BLOCK] + bias
                sink = sinks_ref[h] * LOG2E
                m = jnp.maximum(jnp.max(s, axis=0, keepdims=True), sink)
                p = jnp.exp2(s - m)
                inv_ls.append(1.0 / (jnp.sum(p, axis=0, keepdims=True) + jnp.exp2(sink - m)))
                p_halves.append(p.astype(BF16))
            p_ts.append(jnp.concatenate(p_halves, axis=1))
        box["p"], box["inv_l"] = p_ts, inv_ls

    def swa_values(j, box):
        o_t = [None] * B_HEADS
        for ui, ((kvh, par), p_t) in enumerate(zip(swa_units, box["p"])):
            drow = slice(kvh * B_HEAD_DIM, (kvh + 1) * B_HEAD_DIM)
            if j == 0:
                vt_h = jnp.concatenate([vth_ref[drow, :], vt_ref[drow, 0:BLOCK]], axis=1)
            else:
                vt_h = vt_ref[drow, (j - 1) * BLOCK:(j + 1) * BLOCK]
            pv = jnp.dot(vt_h, p_t, preferred_element_type=F32)
            for half, h in enumerate((4 * kvh + par, 4 * kvh + 2 + par)):
                o_t[h] = pv[:, half * BLOCK:(half + 1) * BLOCK] * box["inv_l"][2 * ui + half]
        box["o_t"] = o_t

    def swa_store(j, box):
        o_all_t = jnp.concatenate(box["o_t"], axis=0)
        scale = lax.rsqrt(jnp.sum(o_all_t * o_all_t, axis=0, keepdims=True) * (1.0 / B_WIDTH) + EPS)
        o_all = (o_all_t * scale).T
        y_ref[j * BLOCK:(j + 1) * BLOCK, A_WIDTH:] = (o_all * ang_ref[...]).astype(y_ref.dtype)

    group = HGRN_GROUP_CHUNKS
    hgrn = [functools.partial(hgrn_updates, c, c + group) for c in range(0, n_chunk, group)]
    hgrn += [functools.partial(hgrn_scan, h) for h in range(A_HEADS)]
    for p in range(ts // PAIR):
        box = {}
        hgrn += [functools.partial(f, p, box) for f in (hgrn_scores, hgrn_values, hgrn_store)]
    swa = []
    for j in range(n_blk):
        box = {}
        swa += [functools.partial(f, j, box) for f in (swa_scores, swa_softmax, swa_values, swa_store)]
    stages = []
    for k in range(max(len(hgrn), len(swa))):
        stages += hgrn[k:k + 1] + swa[k:k + 1]
    return stages


def _ffn_stages(x_ref, y_ref, wo_ref, g2_ref, wg_ref, wu_ref, wd_ref, gf_ref, o_ref,
                h_ref, act_ref):
    box = {}

    def head():
        h = x_ref[...] + jnp.dot(y_ref[...], wo_ref[...], preferred_element_type=F32)
        box["u"] = (h * _rms_scale(h) * g2_ref[...]).astype(BF16)
        h_ref[...] = h

    def ff_up(c0):
        cs = slice(c0, c0 + FF_CHUNK)
        box["gate"] = jnp.dot(box["u"], wg_ref[:, cs], preferred_element_type=F32)
        box["up"] = jnp.dot(box["u"], wu_ref[:, cs], preferred_element_type=F32)

    def ff_act(c0):
        act_ref[:, c0:c0 + FF_CHUNK] = (_silu(box["gate"]) * box["up"]).astype(BF16)

    def tail():
        h2 = h_ref[...] + jnp.dot(act_ref[...], wd_ref[...], preferred_element_type=F32)
        o_ref[...] = h2 * _rms_scale(h2) * gf_ref[...]

    stages = [head]
    for c0 in range(0, D_FF, FF_CHUNK):
        stages += [functools.partial(ff_up, c0), functools.partial(ff_act, c0)]
    return stages + [tail]


def _run_interleaved(major, minor):
    span = max(len(major) - 2, 1)
    done = 0
    for k, stage in enumerate(major):
        stage()
        want = min(len(minor), -(-(k + 1) * len(minor) // span))
        while done < want:
            minor[done]()
            done += 1


def _mixer_ffn_kernel(sinks_ref, qd_ref, kit_ref, ks_ref, hv_ref, gate_ref, dec_ref, bq_ref,
                      kp_ref, kph_ref, vt_ref, vth_ref, ang_ref,
                      x_ref, wo_ref, g2_ref, wg_ref, wu_ref, wd_ref, gf_ref, o_ref,
                      state_ref, upd_ref, sprev_ref, bias_ref, y_cur, y_prev, h_ref, act_ref,
                      *, n_tiles, n_s):
    i = pl.program_id(0)
    seq_start = lax.rem(i, n_s) == 0

    @pl.when(i == 0)
    def _():
        _bias_table_init(bias_ref)

    @pl.when(seq_start)
    def _():
        state_ref[...] = jnp.zeros_like(state_ref)

    first = jnp.where(seq_start, 1, 0)
    mixer_args = (first, sinks_ref, qd_ref, kit_ref, ks_ref, hv_ref, gate_ref, dec_ref, bq_ref,
                  kp_ref, kph_ref, vt_ref, vth_ref, ang_ref, y_cur,
                  state_ref, upd_ref, sprev_ref, bias_ref)
    ffn_args = (x_ref, y_prev, wo_ref, g2_ref, wg_ref, wu_ref, wd_ref, gf_ref, o_ref,
                h_ref, act_ref)

    @pl.when(i == 0)
    def _():
        _run_interleaved(_mixer_stages(*mixer_args), [])
        y_prev[...] = y_cur[...]

    @pl.when((i > 0) & (i < n_tiles))
    def _():
        _run_interleaved(_ffn_stages(*ffn_args), _mixer_stages(*mixer_args))
        y_prev[...] = y_cur[...]

    @pl.when(i == n_tiles)
    def _():
        _run_interleaved(_ffn_stages(*ffn_args), [])


def _mixer_ffn(prep, sinks, attn_norm_g, x2d, w_out, ln2_g, w_gate, w_up, w_down, final_g, seq):
    qd, kit, ks, hv, gate, dec, bq, kp, vt = prep
    t = x2d.shape[0]
    ts = TS_MIX
    n_tiles = t // ts
    n_s = seq // ts
    n_chunk = ts // CHUNK
    n_blk = ts // BLOCK
    cur = lambda i: jnp.minimum(i, n_tiles - 1)
    prev = lambda i: jnp.maximum(i - 1, 0)
    halo = lambda i: jnp.maximum(cur(i) * n_blk - 1, 0)
    row_blk = lambda width: pl.BlockSpec((ts, width), lambda i: (cur(i), 0))
    const = lambda shape: pl.BlockSpec(shape, lambda i: (0, 0), pipeline_mode=pl.Buffered(1))
    return pl.pallas_call(
        functools.partial(_mixer_ffn_kernel, n_tiles=n_tiles, n_s=n_s),
        grid=(n_tiles + 1,),
        in_specs=[
            pl.BlockSpec(memory_space=pltpu.SMEM),
            row_blk(A_QK), pl.BlockSpec((A_QK, ts), lambda i: (0, cur(i))), row_blk(A_QK),
            row_blk(A_WIDTH), row_blk(A_WIDTH),
            pl.BlockSpec((n_chunk, A_QK), lambda i: (cur(i), 0)),
            row_blk(B_WIDTH),
            pl.BlockSpec((4, ts, LANES), lambda i: (0, cur(i), 0)),
            pl.BlockSpec((4, BLOCK, LANES), lambda i: (0, halo(i), 0)),
            pl.BlockSpec((LANES, ts), lambda i: (0, cur(i))),
            pl.BlockSpec((LANES, BLOCK), lambda i: (0, halo(i))),
            pl.BlockSpec((1, B_WIDTH), lambda i: (0, 0)),
            pl.BlockSpec((ts, D_MODEL), lambda i: (prev(i), 0)),
            const((MIX_WIDTH, D_MODEL)),
            const((1, D_MODEL)),
            const((D_MODEL, D_FF)),
            const((D_MODEL, D_FF)),
            const((D_FF, D_MODEL)),
            const((1, D_MODEL)),
        ],
        out_specs=pl.BlockSpec((ts, D_MODEL), lambda i: (prev(i), 0)),
        out_shape=jax.ShapeDtypeStruct((t, D_MODEL), F32),
        scratch_shapes=[
            pltpu.VMEM((A_HEADS, A_DV, A_DK), F32),
            pltpu.VMEM((n_chunk, A_HEADS, A_DV, A_DK), F32),
            pltpu.VMEM((n_chunk, A_HEADS, A_DV, A_DK), BF16),
            pltpu.VMEM((2, B_HEADS, 2 * BLOCK, BLOCK), F32),
            pltpu.VMEM((ts, MIX_WIDTH), BF16),
            pltpu.VMEM((ts, MIX_WIDTH), BF16),
            pltpu.VMEM((ts, D_MODEL), F32),
            pltpu.VMEM((ts, D_FF), BF16),
        ],
        compiler_params=pltpu.CompilerParams(
            dimension_semantics=("arbitrary",), vmem_limit_bytes=V7X_VMEM_LIMIT_BYTES),
        name="mixer_ffn",
    )(sinks, qd, kit, ks, hv, gate, dec, bq, kp, kp, vt, vt, attn_norm_g,
      x2d, w_out, ln2_g, w_gate, w_up, w_down, final_g)


def kernel(x, ln1_g, w_in, lb_logits, hgrn_norm_g, attn_sinks, attn_norm_g, w_out, ln2_g,
           w_gate, w_up, w_down, final_g):
    bsz, seq, d = x.shape
    x2d = x.reshape(bsz * seq, d)
    *prep, wo16, wg16, wu16, wd16 = _in_proj(x2d, ln1_g[0:1], w_in[0], lb_logits, hgrn_norm_g[0:1],
                                             w_out[0], w_gate[0], w_up[0], w_down[0])
    out = _mixer_ffn(prep, attn_sinks[0], attn_norm_g[0:1], x2d, wo16, ln2_g[0:1],
                     wg16, wu16, wd16, final_g.reshape(1, d), seq)
    return out.reshape(bsz, seq, d)
```

```python
import functools

import jax
import jax.numpy as jnp
from jax import lax
from jax.experimental import pallas as pl
from jax.experimental.pallas import tpu as pltpu

D_MODEL = 1024
A_HEADS = 4
A_DK = 128
A_DV = 128
A_QK = A_HEADS * A_DK
A_WIDTH = A_HEADS * A_DV
CHUNK = 64
B_HEADS = 8
B_KV_HEADS = 2
B_HEAD_DIM = 64
B_GROUP = B_HEADS // B_KV_HEADS
B_WIDTH = B_HEADS * B_HEAD_DIM
B_KV = B_KV_HEADS * B_HEAD_DIM
WINDOW = 128
BLOCK = 128
MIX_WIDTH = A_WIDTH + B_WIDTH
D_FF = 2816
IN_COLS = 4 * A_QK + B_WIDTH + 2 * B_KV
EPS = 1e-6
NEG_BIG = -1e30
LANES = 128
SUBLANES = 8

F32 = jnp.float32
BF16 = jnp.bfloat16

V7X_VMEM_LIMIT_BYTES = 56 * 1024 * 1024

TM_PROJ = 512
TS_MIX = 512
FF_CHUNK = 256
HGRN_GROUP_CHUNKS = 2
PAIR = 2 * CHUNK
CAST_ROWS = 128
LOG2E = 1.4426950408889634

NT_DIMS = (((1,), (1,)), ((), ()))
TN_DIMS = (((0,), (0,)), ((), ()))


def _rms_scale(v):
    return lax.rsqrt(jnp.mean(v * v, axis=-1, keepdims=True) + EPS)


def _silu(v):
    return v * (1.0 / (1.0 + jnp.exp(-v)))


def _sigmoid(v):
    return 1.0 / (1.0 + jnp.exp(-v))


def _chunk_cumsum(v):
    rows, n = v.shape
    groups = rows // SUBLANES
    g = v.reshape(groups, SUBLANES, n)
    sub = lax.broadcasted_iota(jnp.int32, g.shape, 1)
    shift = 1
    while shift < SUBLANES:
        g = g + jnp.where(sub >= shift, pltpu.roll(g, shift, axis=1), 0.0)
        shift *= 2
    out, offset = [], None
    for k in range(groups):
        blk = g[k] if offset is None else g[k] + offset
        out.append(blk)
        offset = blk[SUBLANES - 1:SUBLANES, :]
    return jnp.concatenate(out, axis=0)


def _in_proj_kernel(x_ref, g_ref, w32_ref, hng_ref, wo32_ref, wg32_ref, wu32_ref, wd32_ref,
                    pq_ref, pf_ref, hv_ref, gate_ref, bq_ref, kp_ref, vt_ref,
                    wo16_ref, wg16_ref, wu16_ref, wd16_ref, w_ref, *, n_down_slabs):
    tm = x_ref.shape[0]

    @pl.when(pl.program_id(0) == 0)
    def _():
        for r0 in range(0, D_MODEL, CAST_ROWS):
            w_ref[r0:r0 + CAST_ROWS, :] = w32_ref[r0:r0 + CAST_ROWS, :].astype(BF16)

    @pl.when(pl.program_id(0) < n_down_slabs)
    def _():
        wd16_ref[...] = wd32_ref[...].astype(BF16)

    x = x_ref[...]
    u = (x * _rms_scale(x) * g_ref[...]).astype(BF16)
    proj = jnp.dot(u, w_ref[...], preferred_element_type=F32)

    col_f, col_i, col_g, col_a = A_QK, 2 * A_QK, 2 * A_QK + A_WIDTH, 2 * A_QK + 2 * A_WIDTH
    pq_ref[...] = proj[:, :col_f]
    pf_ref[...] = proj[:, col_f:col_i]
    hv_ref[...] = proj[:, col_i:col_g].astype(BF16)
    gate_ref[...] = (_silu(proj[:, col_g:col_a]) * hng_ref[...]).astype(BF16)

    wo16_ref[...] = wo32_ref[...].astype(BF16)
    wg16_ref[...] = wg32_ref[...].astype(BF16)
    wu16_ref[...] = wu32_ref[...].astype(BF16)

    bq_ref[...] = (proj[:, col_a:col_a + B_WIDTH] * (B_HEAD_DIM ** -0.5 * LOG2E)).astype(BF16)
    lo = lax.broadcasted_iota(jnp.int32, (tm, LANES), 1) < B_HEAD_DIM
    k_all = proj[:, col_a + B_WIDTH:col_a + B_WIDTH + B_KV]
    k_rot = pltpu.roll(k_all, B_HEAD_DIM, axis=1)
    kp_ref[0] = jnp.where(lo, k_all, 0.0).astype(BF16)
    kp_ref[1] = jnp.where(lo, 0.0, k_rot).astype(BF16)
    kp_ref[2] = jnp.where(lo, k_rot, 0.0).astype(BF16)
    kp_ref[3] = jnp.where(lo, 0.0, k_all).astype(BF16)
    vt_ref[...] = proj[:, col_a + B_WIDTH + B_KV:].T.astype(BF16)


def _in_proj(x2d, ln1_g, w_in, hgrn_norm_g, w_out, w_gate, w_up, w_down):
    t = x2d.shape[0]
    tm = TM_PROJ
    n_steps = t // tm
    row_blk = lambda width: pl.BlockSpec((tm, width), lambda i: (i, 0))
    const = lambda shape, **kw: pl.BlockSpec(shape, lambda i: (0, 0), **kw)
    bf = lambda shape: jax.ShapeDtypeStruct(shape, BF16)

    def slab(w, rows):
        n_slabs = w.shape[0] // rows
        assert w.shape[0] % rows == 0 and n_slabs <= n_steps, (w.shape, rows)
        return pl.BlockSpec((rows, w.shape[1]), lambda i: (jnp.minimum(i, n_slabs - 1), 0))

    cast_weights = (w_out, w_gate, w_up, w_down)
    even_rows = D_MODEL // n_steps
    slab_rows = (even_rows, even_rows, even_rows, CAST_ROWS)
    slab_specs = lambda: [slab(w, rows) for w, rows in zip(cast_weights, slab_rows)]
    return pl.pallas_call(
        functools.partial(_in_proj_kernel, n_down_slabs=D_FF // CAST_ROWS),
        grid=(n_steps,),
        in_specs=[
            row_blk(D_MODEL),
            const((1, D_MODEL)),
            const((D_MODEL, IN_COLS), pipeline_mode=pl.Buffered(1)),
            const((1, A_WIDTH)),
        ] + slab_specs(),
        out_specs=[
            row_blk(A_QK), row_blk(A_QK), row_blk(A_WIDTH), row_blk(A_WIDTH),
            row_blk(B_WIDTH),
            pl.BlockSpec((4, tm, LANES), lambda i: (0, i, 0)),
            pl.BlockSpec((LANES, tm), lambda i: (0, i)),
        ] + slab_specs(),
        out_shape=[
            jax.ShapeDtypeStruct((t, A_QK), F32), jax.ShapeDtypeStruct((t, A_QK), F32),
            bf((t, A_WIDTH)), bf((t, A_WIDTH)),
            bf((t, B_WIDTH)),
            bf((4, t, LANES)),
            bf((LANES, t)),
        ] + [bf(w.shape) for w in cast_weights],
        scratch_shapes=[pltpu.VMEM((D_MODEL, IN_COLS), BF16)],
        compiler_params=pltpu.CompilerParams(
            dimension_semantics=("arbitrary",), vmem_limit_bytes=V7X_VMEM_LIMIT_BYTES),
        name="in_proj",
    )(x2d, ln1_g, w_in, hgrn_norm_g, *cast_weights)


def _bias_table_init(bias_ref):
    kj = lax.broadcasted_iota(jnp.int32, (2 * BLOCK, BLOCK), 0)
    qi = lax.broadcasted_iota(jnp.int32, (2 * BLOCK, BLOCK), 1)
    dist = qi + BLOCK - kj
    valid = (dist >= 0) & (dist < WINDOW)
    valid_first = valid & (kj >= BLOCK)
    distf = dist.astype(F32)
    for h in range(B_HEADS):
        slope = 2.0 ** (-(8.0 / B_HEADS) * (h + 1))
        sc = (-slope * LOG2E) * distf
        bias_ref[0, h] = jnp.where(valid, sc, NEG_BIG)
        bias_ref[1, h] = jnp.where(valid_first, sc, NEG_BIG)


def _mixer_stages(first, sinks_ref, pq_ref, pf_ref, lbl_ref, hv_ref, gate_ref, bq_ref,
                  kp_ref, kph_ref, vt_ref, vth_ref, ang_ref, y_ref,
                  state_ref, upd_ref, sprev_ref, bias_ref, qd_ref, kit_ref, ks_ref, dec_ref):
    ts = pq_ref.shape[0]
    n_chunk = ts // CHUNK
    n_blk = ts // BLOCK
    head_cols = [slice(h * A_DK, (h + 1) * A_DK) for h in range(A_HEADS)]

    def hgrn_prepare(p):
        lbl = lbl_ref[...]
        lmax = jnp.max(lbl, axis=0, keepdims=True)
        le = jnp.exp(lbl - lmax)
        lb = le[0:1, :] / jnp.sum(le, axis=0, keepdims=True)
        k_pair = []
        for ci in range(p * PAIR // CHUNK, (p + 1) * PAIR // CHUNK):
            rows = slice(ci * CHUNK, (ci + 1) * CHUNK)
            forget = lb + (1.0 - lb) * _sigmoid(pf_ref[rows, :])
            key = 1.0 - forget
            bcum = _chunk_cumsum(jnp.log(forget))
            decay = jnp.exp(bcum[CHUNK - 1:CHUNK, :])
            k_intra = key * jnp.exp(-bcum)
            qd_ref[rows, :] = (_silu(pq_ref[rows, :]) * jnp.exp(bcum)).astype(BF16)
            ks_ref[rows, :] = (k_intra * decay).astype(BF16)
            dec_ref[ci:ci + 1, :] = decay
            k_pair.append(k_intra)
        kit_ref[:, p * PAIR:(p + 1) * PAIR] = jnp.concatenate(k_pair, axis=0).T.astype(BF16)

    def chunk_units(c_lo, c_hi):
        return [(ci, slice(ci * CHUNK, (ci + 1) * CHUNK), head_cols[h], h)
                for ci in range(c_lo, c_hi) for h in range(A_HEADS)]

    def hgrn_updates(c_lo, c_hi):
        units = chunk_units(c_lo, c_hi)
        upds = [lax.dot_general(hv_ref[rows, cs], ks_ref[rows, cs], TN_DIMS,
                                preferred_element_type=F32) for (_, rows, cs, _) in units]
        for upd, (ci, _, _, h) in zip(upds, units):
            upd_ref[ci, h] = upd

    def hgrn_scan(h):
        st = state_ref[h]
        for ci in range(n_chunk):
            sprev_ref[ci, h] = st.T.astype(BF16)
            st = st * dec_ref[ci:ci + 1, head_cols[h]] + upd_ref[ci, h]
        state_ref[h] = st

    def hgrn_scores(p, box):
        rows = slice(p * PAIR, (p + 1) * PAIR)
        box["a"] = [jnp.dot(qd_ref[rows, cs], kit_ref[cs, rows], preferred_element_type=F32)
                    for cs in head_cols]
        box["o_inter"] = [
            jnp.concatenate(
                [jnp.dot(qd_ref[ci * CHUNK:(ci + 1) * CHUNK, cs], sprev_ref[ci, h],
                         preferred_element_type=F32)
                 for ci in range(p * PAIR // CHUNK, (p + 1) * PAIR // CHUNK)], axis=0)
            for h, cs in enumerate(head_cols)]

    def hgrn_values(p, box):
        rows = slice(p * PAIR, (p + 1) * PAIR)
        r = lax.broadcasted_iota(jnp.int32, (PAIR, PAIR), 0)
        c = lax.broadcasted_iota(jnp.int32, (PAIR, PAIR), 1)
        same_chunk_causal = (c <= r) & (c >= (r // CHUNK) * CHUNK)
        a_s = [jnp.where(same_chunk_causal, a, 0.0).astype(BF16) for a in box["a"]]
        box["o"] = [jnp.dot(a, hv_ref[rows, cs], preferred_element_type=F32) + oi
                    for a, oi, cs in zip(a_s, box["o_inter"], head_cols)]

    def hgrn_store(p, box):
        rows = slice(p * PAIR, (p + 1) * PAIR)
        for o, cs in zip(box["o"], head_cols):
            gate = gate_ref[rows, cs].astype(F32)
            y_ref[rows, cs] = (o * _rms_scale(o) * gate).astype(y_ref.dtype)

    swa_units = [(kvh, par) for kvh in range(B_KV_HEADS) for par in range(2)]

    def swa_scores(j, box):
        s_ts = []
        for kvh, par in swa_units:
            rows = slice(j * BLOCK, (j + 1) * BLOCK)
            q_pairs = jnp.concatenate(
                [bq_ref[rows, (2 * kvh) * LANES:(2 * kvh + 1) * LANES],
                 bq_ref[rows, (2 * kvh + 1) * LANES:(2 * kvh + 2) * LANES]], axis=0)
            idx = 2 * kvh + par
            if j == 0:
                kk = jnp.concatenate([kph_ref[idx], kp_ref[idx, 0:BLOCK, :]], axis=0)
            else:
                kk = kp_ref[idx, (j - 1) * BLOCK:(j + 1) * BLOCK, :]
            s_ts.append(lax.dot_general(kk, q_pairs, NT_DIMS, preferred_element_type=F32))
        box["s"] = s_ts

    def swa_softmax(j, box):
        p_ts, inv_ls = [], []
        for (kvh, par), s_t in zip(swa_units, box["s"]):
            p_halves = []
            for half, h in enumerate((4 * kvh + par, 4 * kvh + 2 + par)):
                bias = bias_ref[first, h] if j == 0 else bias_ref[0, h]
                s = s_t[:, half * BLOCK:(half + 1) * BLOCK] + bias
                sink = sinks_ref[h] * LOG2E
                m = jnp.maximum(jnp.max(s, axis=0, keepdims=True), sink)
                p = jnp.exp2(s - m)
                inv_ls.append(1.0 / (jnp.sum(p, axis=0, keepdims=True) + jnp.exp2(sink - m)))
                p_halves.append(p.astype(BF16))
            p_ts.append(jnp.concatenate(p_halves, axis=1))
        box["p"], box["inv_l"] = p_ts, inv_ls

    def swa_values(j, box):
        o_t = [None] * B_HEADS
        for ui, ((kvh, par), p_t) in enumerate(zip(swa_units, box["p"])):
            drow = slice(kvh * B_HEAD_DIM, (kvh + 1) * B_HEAD_DIM)
            if j == 0:
                vt_h = jnp.concatenate([vth_ref[drow, :], vt_ref[drow, 0:BLOCK]], axis=1)
            else:
                vt_h = vt_ref[drow, (j - 1) * BLOCK:(j + 1) * BLOCK]
            pv = jnp.dot(vt_h, p_t, preferred_element_type=F32)
            for half, h in enumerate((4 * kvh + par, 4 * kvh + 2 + par)):
                o_t[h] = pv[:, half * BLOCK:(half + 1) * BLOCK] * box["inv_l"][2 * ui + half]
        box["o_t"] = o_t

    def swa_store(j, box):
        o_all_t = jnp.concatenate(box["o_t"], axis=0)
        scale = lax.rsqrt(jnp.sum(o_all_t * o_all_t, axis=0, keepdims=True) * (1.0 / B_WIDTH) + EPS)
        o_all = (o_all_t * scale).T
        y_ref[j * BLOCK:(j + 1) * BLOCK, A_WIDTH:] = (o_all * ang_ref[...]).astype(y_ref.dtype)

    group = HGRN_GROUP_CHUNKS
    hgrn = [functools.partial(hgrn_prepare, p) for p in range(ts // PAIR)]
    hgrn += [functools.partial(hgrn_updates, c, c + group) for c in range(0, n_chunk, group)]
    hgrn += [functools.partial(hgrn_scan, h) for h in range(A_HEADS)]
    for p in range(ts // PAIR):
        box = {}
        hgrn += [functools.partial(f, p, box) for f in (hgrn_scores, hgrn_values, hgrn_store)]
    swa = []
    for j in range(n_blk):
        box = {}
        swa += [functools.partial(f, j, box) for f in (swa_scores, swa_softmax, swa_values, swa_store)]
    stages = []
    for k in range(max(len(hgrn), len(swa))):
        stages += hgrn[k:k + 1] + swa[k:k + 1]
    return stages


def _ffn_stages(x_ref, y_ref, wo_ref, g2_ref, wg_ref, wu_ref, wd_ref, gf_ref, o_ref,
                h_ref, act_ref):
    box = {}

    def head():
        h = x_ref[...] + jnp.dot(y_ref[...], wo_ref[...], preferred_element_type=F32)
        box["u"] = (h * _rms_scale(h) * g2_ref[...]).astype(BF16)
        h_ref[...] = h

    def ff_up(c0):
        cs = slice(c0, c0 + FF_CHUNK)
        box["gate"] = jnp.dot(box["u"], wg_ref[:, cs], preferred_element_type=F32)
        box["up"] = jnp.dot(box["u"], wu_ref[:, cs], preferred_element_type=F32)

    def ff_act(c0):
        act_ref[:, c0:c0 + FF_CHUNK] = (_silu(box["gate"]) * box["up"]).astype(BF16)

    def tail():
        h2 = h_ref[...] + jnp.dot(act_ref[...], wd_ref[...], preferred_element_type=F32)
        o_ref[...] = h2 * _rms_scale(h2) * gf_ref[...]

    stages = [head]
    for c0 in range(0, D_FF, FF_CHUNK):
        stages += [functools.partial(ff_up, c0), functools.partial(ff_act, c0)]
    return stages + [tail]


def _run_interleaved(major, minor):
    span = max(len(major) - 2, 1)
    done = 0
    for k, stage in enumerate(major):
        stage()
        want = min(len(minor), -(-(k + 1) * len(minor) // span))
        while done < want:
            minor[done]()
            done += 1


def _mixer_ffn_kernel(sinks_ref, pq_ref, pf_ref, lbl_ref, hv_ref, gate_ref, bq_ref,
                      kp_ref, kph_ref, vt_ref, vth_ref, ang_ref,
                      x_ref, wo_ref, g2_ref, wg_ref, wu_ref, wd_ref, gf_ref, o_ref,
                      state_ref, upd_ref, sprev_ref, bias_ref, qd_ref, kit_ref, ks_ref, dec_ref,
                      y_cur, y_prev, h_ref, act_ref, *, n_tiles, n_s):
    i = pl.program_id(0)
    seq_start = lax.rem(i, n_s) == 0

    @pl.when(i == 0)
    def _():
        _bias_table_init(bias_ref)

    @pl.when(seq_start)
    def _():
        state_ref[...] = jnp.zeros_like(state_ref)

    first = jnp.where(seq_start, 1, 0)
    mixer_args = (first, sinks_ref, pq_ref, pf_ref, lbl_ref, hv_ref, gate_ref, bq_ref,
                  kp_ref, kph_ref, vt_ref, vth_ref, ang_ref, y_cur,
                  state_ref, upd_ref, sprev_ref, bias_ref, qd_ref, kit_ref, ks_ref, dec_ref)
    ffn_args = (x_ref, y_prev, wo_ref, g2_ref, wg_ref, wu_ref, wd_ref, gf_ref, o_ref,
                h_ref, act_ref)

    @pl.when(i == 0)
    def _():
        _run_interleaved(_mixer_stages(*mixer_args), [])
        y_prev[...] = y_cur[...]

    @pl.when((i > 0) & (i < n_tiles))
    def _():
        _run_interleaved(_ffn_stages(*ffn_args), _mixer_stages(*mixer_args))
        y_prev[...] = y_cur[...]

    @pl.when(i == n_tiles)
    def _():
        _run_interleaved(_ffn_stages(*ffn_args), [])


def _mixer_ffn(prep, sinks, lb_logits, attn_norm_g, x2d, w_out, ln2_g, w_gate, w_up, w_down,
               final_g, seq):
    pq, pf, hv, gate, bq, kp, vt = prep
    t = x2d.shape[0]
    ts = TS_MIX
    n_tiles = t // ts
    n_s = seq // ts
    n_chunk = ts // CHUNK
    n_blk = ts // BLOCK
    cur = lambda i: jnp.minimum(i, n_tiles - 1)
    prev = lambda i: jnp.maximum(i - 1, 0)
    halo = lambda i: jnp.maximum(cur(i) * n_blk - 1, 0)
    row_blk = lambda width: pl.BlockSpec((ts, width), lambda i: (cur(i), 0))
    const = lambda shape: pl.BlockSpec(shape, lambda i: (0, 0), pipeline_mode=pl.Buffered(1))
    return pl.pallas_call(
        functools.partial(_mixer_ffn_kernel, n_tiles=n_tiles, n_s=n_s),
        grid=(n_tiles + 1,),
        in_specs=[
            pl.BlockSpec(memory_space=pltpu.SMEM),
            row_blk(A_QK), row_blk(A_QK),
            pl.BlockSpec((2, A_QK), lambda i: (0, 0)),
            row_blk(A_WIDTH), row_blk(A_WIDTH),
            row_blk(B_WIDTH),
            pl.BlockSpec((4, ts, LANES), lambda i: (0, cur(i), 0)),
            pl.BlockSpec((4, BLOCK, LANES), lambda i: (0, halo(i), 0)),
            pl.BlockSpec((LANES, ts), lambda i: (0, cur(i))),
            pl.BlockSpec((LANES, BLOCK), lambda i: (0, halo(i))),
            pl.BlockSpec((1, B_WIDTH), lambda i: (0, 0)),
            pl.BlockSpec((ts, D_MODEL), lambda i: (prev(i), 0)),
            const((MIX_WIDTH, D_MODEL)),
            const((1, D_MODEL)),
            const((D_MODEL, D_FF)),
            const((D_MODEL, D_FF)),
            const((D_FF, D_MODEL)),
            const((1, D_MODEL)),
        ],
        out_specs=pl.BlockSpec((ts, D_MODEL), lambda i: (prev(i), 0)),
        out_shape=jax.ShapeDtypeStruct((t, D_MODEL), F32),
        scratch_shapes=[
            pltpu.VMEM((A_HEADS, A_DV, A_DK), F32),
            pltpu.VMEM((n_chunk, A_HEADS, A_DV, A_DK), F32),
            pltpu.VMEM((n_chunk, A_HEADS, A_DV, A_DK), BF16),
            pltpu.VMEM((2, B_HEADS, 2 * BLOCK, BLOCK), F32),
            pltpu.VMEM((ts, A_QK), BF16),
            pltpu.VMEM((A_QK, ts), BF16),
            pltpu.VMEM((ts, A_QK), BF16),
            pltpu.VMEM((n_chunk, A_QK), F32),
            pltpu.VMEM((ts, MIX_WIDTH), BF16),
            pltpu.VMEM((ts, MIX_WIDTH), BF16),
            pltpu.VMEM((ts, D_MODEL), F32),
            pltpu.VMEM((ts, D_FF), BF16),
        ],
        compiler_params=pltpu.CompilerParams(
            dimension_semantics=("arbitrary",), vmem_limit_bytes=V7X_VMEM_LIMIT_BYTES),
        name="mixer_ffn",
    )(sinks, pq, pf, lb_logits, hv, gate, bq, kp, kp, vt, vt, attn_norm_g,
      x2d, w_out, ln2_g, w_gate, w_up, w_down, final_g)


def kernel(x, ln1_g, w_in, lb_logits, hgrn_norm_g, attn_sinks, attn_norm_g, w_out, ln2_g,
           w_gate, w_up, w_down, final_g):
    bsz, seq, d = x.shape
    x2d = x.reshape(bsz * seq, d)
    *prep, wo16, wg16, wu16, wd16 = _in_proj(x2d, ln1_g[0:1], w_in[0], hgrn_norm_g[0:1],
                                             w_out[0], w_gate[0], w_up[0], w_down[0])
    out = _mixer_ffn(prep, attn_sinks[0], lb_logits, attn_norm_g[0:1], x2d, wo16, ln2_g[0:1],
                     wg16, wu16, wd16, final_g.reshape(1, d), seq)
    return out.reshape(bsz, seq, d)
```

```python
import functools

import jax
import jax.numpy as jnp
from jax import lax
from jax.experimental import pallas as pl
from jax.experimental.pallas import tpu as pltpu

D_MODEL = 1024
A_HEADS = 4
A_DK = 128
A_DV = 128
A_QK = A_HEADS * A_DK
A_WIDTH = A_HEADS * A_DV
CHUNK = 64
B_HEADS = 8
B_KV_HEADS = 2
B_HEAD_DIM = 64
B_GROUP = B_HEADS // B_KV_HEADS
B_WIDTH = B_HEADS * B_HEAD_DIM
B_KV = B_KV_HEADS * B_HEAD_DIM
WINDOW = 128
BLOCK = 128
MIX_WIDTH = A_WIDTH + B_WIDTH
D_FF = 2816
IN_COLS = 4 * A_QK + B_WIDTH + 2 * B_KV
EPS = 1e-6
NEG_BIG = -1e30
LANES = 128

F32 = jnp.float32
BF16 = jnp.bfloat16

V7X_VMEM_LIMIT_BYTES = 56 * 1024 * 1024

TM_PROJ = 512
TS_MIX = 512
FF_CHUNK = 256
HGRN_GROUP_CHUNKS = 2
PAIR = 2 * CHUNK
CAST_ROWS = 128
LOG2E = 1.4426950408889634

NT_DIMS = (((1,), (1,)), ((), ()))
TN_DIMS = (((0,), (0,)), ((), ()))


def _rms_scale(v):
    return lax.rsqrt(jnp.mean(v * v, axis=-1, keepdims=True) + EPS)


def _silu(v):
    return v * (1.0 / (1.0 + jnp.exp(-v)))


def _sigmoid(v):
    return 1.0 / (1.0 + jnp.exp(-v))


def _in_proj_kernel(x_ref, g_ref, w32_ref, lbl_ref, hng_ref, wo32_ref, wg32_ref, wu32_ref, wd32_ref,
                    qd_ref, kit_ref, ks_ref, hv_ref, gate_ref, dec_ref, bq_ref, kp_ref, vt_ref,
                    wo16_ref, wg16_ref, wu16_ref, wd16_ref, w_ref, *, n_down_slabs):
    tm = x_ref.shape[0]

    @pl.when(pl.program_id(0) == 0)
    def _():
        for r0 in range(0, D_MODEL, CAST_ROWS):
            rows = slice(r0, r0 + CAST_ROWS)
            w_ref[rows, :] = (w32_ref[rows, :] * g_ref[rows, :]).astype(BF16)

    @pl.when(pl.program_id(0) < n_down_slabs)
    def _():
        wd16_ref[...] = wd32_ref[...].astype(BF16)

    x = x_ref[...]
    u = x.astype(BF16)
    r_x = _rms_scale(x)

    lbl = lbl_ref[...]
    lmax = jnp.max(lbl, axis=0, keepdims=True)
    le = jnp.exp(lbl - lmax)
    lb = le[0:1, :] / jnp.sum(le, axis=0, keepdims=True)

    r = lax.broadcasted_iota(jnp.int32, (CHUNK, CHUNK), 0)
    c = lax.broadcasted_iota(jnp.int32, (CHUNK, CHUNK), 1)
    tri = jnp.where(r >= c, 1.0, 0.0).astype(BF16)

    pf = jnp.dot(u, w_ref[:, A_QK:2 * A_QK], preferred_element_type=F32) * r_x
    pq = jnp.dot(u, w_ref[:, 0:A_QK], preferred_element_type=F32) * r_x
    k_pair = []
    for ci in range(tm // CHUNK):
        rows = slice(ci * CHUNK, (ci + 1) * CHUNK)
        forget = lb + (1.0 - lb) * _sigmoid(pf[rows])
        key = 1.0 - forget
        logf = jnp.log(forget)
        p0 = logf.astype(BF16)
        p1 = (logf - p0.astype(F32)).astype(BF16)
        bc2 = jnp.dot(tri, jnp.concatenate([p0, p1], axis=-1), preferred_element_type=F32)
        bcum = bc2[:, :A_QK] + bc2[:, A_QK:]
        decay = jnp.exp(bcum[CHUNK - 1:CHUNK, :])
        k_intra = key * jnp.exp(-bcum)
        qd_ref[rows, :] = (_silu(pq[rows]) * jnp.exp(bcum)).astype(BF16)
        ks_ref[rows, :] = (k_intra * decay).astype(BF16)
        dec_ref[ci:ci + 1, :] = decay
        k_pair.append(k_intra)
        if len(k_pair) == PAIR // CHUNK:
            p0_row = (ci + 1) * CHUNK - PAIR
            kit_ref[:, p0_row:p0_row + PAIR] = jnp.concatenate(k_pair, axis=0).T.astype(BF16)
            k_pair = []

    wo16_ref[...] = wo32_ref[...].astype(BF16)
    wg16_ref[...] = wg32_ref[...].astype(BF16)
    wu16_ref[...] = wu32_ref[...].astype(BF16)

    c0 = 2 * A_QK + 2 * A_WIDTH
    pa = jnp.dot(u, w_ref[:, c0:], preferred_element_type=F32) * r_x
    bq_ref[...] = (pa[:, :B_WIDTH] * (B_HEAD_DIM ** -0.5 * LOG2E)).astype(BF16)
    lo = lax.broadcasted_iota(jnp.int32, (tm, LANES), 1) < B_HEAD_DIM
    k_all = pa[:, B_WIDTH:B_WIDTH + B_KV]
    k_rot = pltpu.roll(k_all, B_HEAD_DIM, axis=1)
    kp_ref[0] = jnp.where(lo, k_all, 0.0).astype(BF16)
    kp_ref[1] = jnp.where(lo, 0.0, k_rot).astype(BF16)
    kp_ref[2] = jnp.where(lo, k_rot, 0.0).astype(BF16)
    kp_ref[3] = jnp.where(lo, 0.0, k_all).astype(BF16)
    vt_ref[...] = pa[:, B_WIDTH + B_KV:].T.astype(BF16)

    pg = jnp.dot(u, w_ref[:, 2 * A_QK + A_WIDTH:2 * A_QK + 2 * A_WIDTH], preferred_element_type=F32) * r_x
    gate_ref[...] = (_silu(pg) * hng_ref[...]).astype(BF16)
    pi = jnp.dot(u, w_ref[:, 2 * A_QK:2 * A_QK + A_WIDTH], preferred_element_type=F32) * r_x
    hv_ref[...] = pi.astype(BF16)


def _in_proj(x2d, ln1_g, w_in, lb_logits, hgrn_norm_g, w_out, w_gate, w_up, w_down):
    t = x2d.shape[0]
    tm = TM_PROJ
    n_steps = t // tm
    row_blk = lambda width: pl.BlockSpec((tm, width), lambda i: (i, 0))
    const = lambda shape, **kw: pl.BlockSpec(shape, lambda i: (0, 0), **kw)
    bf = lambda shape: jax.ShapeDtypeStruct(shape, BF16)

    def slab(w, rows):
        n_slabs = w.shape[0] // rows
        assert w.shape[0] % rows == 0 and n_slabs <= n_steps, (w.shape, rows)
        return pl.BlockSpec((rows, w.shape[1]), lambda i: (jnp.minimum(i, n_slabs - 1), 0))

    cast_weights = (w_out, w_gate, w_up, w_down)
    even_rows = D_MODEL // n_steps
    slab_rows = (even_rows, even_rows, even_rows, CAST_ROWS)
    slab_specs = lambda: [slab(w, rows) for w, rows in zip(cast_weights, slab_rows)]
    return pl.pallas_call(
        functools.partial(_in_proj_kernel, n_down_slabs=D_FF // CAST_ROWS),
        grid=(n_steps,),
        in_specs=[
            row_blk(D_MODEL),
            const((D_MODEL, 1)),
            const((D_MODEL, IN_COLS), pipeline_mode=pl.Buffered(1)),
            const((2, A_QK)),
            const((1, A_WIDTH)),
        ] + slab_specs(),
        out_specs=[
            row_blk(A_QK), pl.BlockSpec((A_QK, tm), lambda i: (0, i)), row_blk(A_QK),
            row_blk(A_WIDTH), row_blk(A_WIDTH),
            pl.BlockSpec((tm // CHUNK, A_QK), lambda i: (i, 0)),
            row_blk(B_WIDTH),
            pl.BlockSpec((4, tm, LANES), lambda i: (0, i, 0)),
            pl.BlockSpec((LANES, tm), lambda i: (0, i)),
        ] + slab_specs(),
        out_shape=[
            bf((t, A_QK)), bf((A_QK, t)), bf((t, A_QK)), bf((t, A_WIDTH)), bf((t, A_WIDTH)),
            jax.ShapeDtypeStruct((t // CHUNK, A_QK), F32),
            bf((t, B_WIDTH)),
            bf((4, t, LANES)),
            bf((LANES, t)),
        ] + [bf(w.shape) for w in cast_weights],
        scratch_shapes=[pltpu.VMEM((D_MODEL, IN_COLS), BF16)],
        compiler_params=pltpu.CompilerParams(
            dimension_semantics=("arbitrary",), vmem_limit_bytes=V7X_VMEM_LIMIT_BYTES),
        name="in_proj",
    )(x2d, ln1_g, w_in, lb_logits, hgrn_norm_g, *cast_weights)


def _bias_table_init(bias_ref):
    kj = lax.broadcasted_iota(jnp.int32, (2 * BLOCK, BLOCK), 0)
    qi = lax.broadcasted_iota(jnp.int32, (2 * BLOCK, BLOCK), 1)
    dist = qi + BLOCK - kj
    valid = (dist >= 0) & (dist < WINDOW)
    valid_first = valid & (kj >= BLOCK)
    distf = dist.astype(F32)
    for h in range(B_HEADS):
        slope = 2.0 ** (-(8.0 / B_HEADS) * (h + 1))
        sc = (-slope * LOG2E) * distf
        bias_ref[0, h] = jnp.where(valid, sc, NEG_BIG)
        bias_ref[1, h] = jnp.where(valid_first, sc, NEG_BIG)


def _mixer_stages(first, sinks_ref, qd_ref, kit_ref, ks_ref, hv_ref, gate_ref, dec_ref, bq_ref,
                  kp_ref, kph_ref, vt_ref, vth_ref, ang_ref, y_ref,
                  state_ref, upd_ref, sprev_ref, bias_ref):
    ts = qd_ref.shape[0]
    n_chunk = ts // CHUNK
    n_blk = ts // BLOCK
    head_cols = [slice(h * A_DK, (h + 1) * A_DK) for h in range(A_HEADS)]

    def chunk_units(c_lo, c_hi):
        return [(ci, slice(ci * CHUNK, (ci + 1) * CHUNK), head_cols[h], h)
                for ci in range(c_lo, c_hi) for h in range(A_HEADS)]

    def hgrn_updates(c_lo, c_hi):
        units = chunk_units(c_lo, c_hi)
        upds = [lax.dot_general(hv_ref[rows, cs], ks_ref[rows, cs], TN_DIMS,
                                preferred_element_type=F32) for (_, rows, cs, _) in units]
        for upd, (ci, _, _, h) in zip(upds, units):
            upd_ref[ci, h] = upd

    def hgrn_scan(h):
        st = state_ref[h]
        for ci in range(n_chunk):
            sprev_ref[ci, h] = st.T.astype(BF16)
            st = st * dec_ref[ci:ci + 1, head_cols[h]] + upd_ref[ci, h]
        state_ref[h] = st

    def hgrn_scores(p, box):
        rows = slice(p * PAIR, (p + 1) * PAIR)
        box["a"] = [jnp.dot(qd_ref[rows, cs], kit_ref[cs, rows], preferred_element_type=F32)
                    for cs in head_cols]
        box["o_inter"] = [
            jnp.concatenate(
                [jnp.dot(qd_ref[ci * CHUNK:(ci + 1) * CHUNK, cs], sprev_ref[ci, h],
                         preferred_element_type=F32)
                 for ci in range(p * PAIR // CHUNK, (p + 1) * PAIR // CHUNK)], axis=0)
            for h, cs in enumerate(head_cols)]

    def hgrn_values(p, box):
        rows = slice(p * PAIR, (p + 1) * PAIR)
        r = lax.broadcasted_iota(jnp.int32, (PAIR, PAIR), 0)
        c = lax.broadcasted_iota(jnp.int32, (PAIR, PAIR), 1)
        same_chunk_causal = (c <= r) & (c >= (r // CHUNK) * CHUNK)
        a_s = [jnp.where(same_chunk_causal, a, 0.0).astype(BF16) for a in box["a"]]
        box["o"] = [jnp.dot(a, hv_ref[rows, cs], preferred_element_type=F32) + oi
                    for a, oi, cs in zip(a_s, box["o_inter"], head_cols)]

    def hgrn_store(p, box):
        rows = slice(p * PAIR, (p + 1) * PAIR)
        for o, cs in zip(box["o"], head_cols):
            gate = gate_ref[rows, cs].astype(F32)
            y_ref[rows, cs] = (o * _rms_scale(o) * gate).astype(y_ref.dtype)

    swa_units = [(kvh, par) for kvh in range(B_KV_HEADS) for par in range(2)]

    def swa_scores(j, box):
        s_ts = []
        for kvh, par in swa_units:
            rows = slice(j * BLOCK, (j + 1) * BLOCK)
            q_pairs = jnp.concatenate(
                [bq_ref[rows, (2 * kvh) * LANES:(2 * kvh + 1) * LANES],
                 bq_ref[rows, (2 * kvh + 1) * LANES:(2 * kvh + 2) * LANES]], axis=0)
            idx = 2 * kvh + par
            if j == 0:
                kk = jnp.concatenate([kph_ref[idx], kp_ref[idx, 0:BLOCK, :]], axis=0)
            else:
                kk = kp_ref[idx, (j - 1) * BLOCK:(j + 1) * BLOCK, :]
            s_ts.append(lax.dot_general(kk, q_pairs, NT_DIMS, preferred_element_type=F32))
        box["s"] = s_ts

    def swa_softmax(j, box):
        p_ts, inv_ls = [], []
        for (kvh, par), s_t in zip(swa_units, box["s"]):
            p_halves = []
            for half, h in enumerate((4 * kvh + par, 4 * kvh + 2 + par)):
                bias = bias_ref[first, h] if j == 0 else bias_ref[0, h]
                s = s_t[:, half * BLOCK:(half + 1) * BLOCK] + bias
                sink = sinks_ref[h] * LOG2E
                m = jnp.maximum(jnp.max(s, axis=0, keepdims=True), sink)
                p = jnp.exp2(s - m)
                inv_ls.append(1.0 / (jnp.sum(p, axis=0, keepdims=True) + jnp.exp2(sink - m)))
                p_halves.append(p.astype(BF16))
            p_ts.append(jnp.concatenate(p_halves, axis=1))
        box["p"], box["inv_l"] = p_ts, inv_ls

    def swa_values(j, box):
        o_t = [None] * B_HEADS
        for ui, ((kvh, par), p_t) in enumerate(zip(swa_units, box["p"])):
            drow = slice(kvh * B_HEAD_DIM, (kvh + 1) * B_HEAD_DIM)
            if j == 0:
                vt_h = jnp.concatenate([vth_ref[drow, :], vt_ref[drow, 0:BLOCK]], axis=1)
            else:
                vt_h = vt_ref[drow, (j - 1) * BLOCK:(j + 1) * BLOCK]
            pv = jnp.dot(vt_h, p_t, preferred_element_type=F32)
            for half, h in enumerate((4 * kvh + par, 4 * kvh + 2 + par)):
                o_t[h] = pv[:, half * BLOCK:(half + 1) * BLOCK] * box["inv_l"][2 * ui + half]
        box["o_t"] = o_t

    def swa_store(j, box):
        o_all_t = jnp.concatenate(box["o_t"], axis=0)
        scale = lax.rsqrt(jnp.sum(o_all_t * o_all_t, axis=0, keepdims=True) * (1.0 / B_WIDTH) + EPS)
        o_all = (o_all_t * scale).T
        y_ref[j * BLOCK:(j + 1) * BLOCK, A_WIDTH:] = (o_all * ang_ref[...]).astype(y_ref.dtype)

    group = HGRN_GROUP_CHUNKS
    hgrn = [functools.partial(hgrn_updates, c, c + group) for c in range(0, n_chunk, group)]
    hgrn += [functools.partial(hgrn_scan, h) for h in range(A_HEADS)]
    for p in range(ts // PAIR):
        box = {}
        hgrn += [functools.partial(f, p, box) for f in (hgrn_scores, hgrn_values, hgrn_store)]
    swa = []
    for j in range(n_blk):
        box = {}
        swa += [functools.partial(f, j, box) for f in (swa_scores, swa_softmax, swa_values, swa_store)]
    stages = []
    for k in range(max(len(hgrn), len(swa))):
        stages += hgrn[k:k + 1] + swa[k:k + 1]
    return stages


def _ffn_stages(x_ref, y_ref, wo_ref, g2_ref, wg_ref, wu_ref, wd_ref, gf_ref, o_ref,
                h_ref, act_ref):
    box = {}

    def head():
        h = x_ref[...] + jnp.dot(y_ref[...], wo_ref[...], preferred_element_type=F32)
        box["u"] = (h * _rms_scale(h) * g2_ref[...]).astype(BF16)
        h_ref[...] = h

    def ff_up(c0):
        cs = slice(c0, c0 + FF_CHUNK)
        box["gate"] = jnp.dot(box["u"], wg_ref[:, cs], preferred_element_type=F32)
        box["up"] = jnp.dot(box["u"], wu_ref[:, cs], preferred_element_type=F32)

    def ff_act(c0):
        act_ref[:, c0:c0 + FF_CHUNK] = (_silu(box["gate"]) * box["up"]).astype(BF16)

    def tail():
        h2 = h_ref[...] + jnp.dot(act_ref[...], wd_ref[...], preferred_element_type=F32)
        o_ref[...] = h2 * _rms_scale(h2) * gf_ref[...]

    stages = [head]
    for c0 in range(0, D_FF, FF_CHUNK):
        stages += [functools.partial(ff_up, c0), functools.partial(ff_act, c0)]
    return stages + [tail]


def _run_interleaved(major, minor):
    span = max(len(major) - 2, 1)
    done = 0
    for k, stage in enumerate(major):
        stage()
        want = min(len(minor), -(-(k + 1) * len(minor) // span))
        while done < want:
            minor[done]()
            done += 1


def _mixer_ffn_kernel(sinks_ref, qd_ref, kit_ref, ks_ref, hv_ref, gate_ref, dec_ref, bq_ref,
                      kp_ref, kph_ref, vt_ref, vth_ref, ang_ref,
                      x_ref, wo_ref, g2_ref, wg_ref, wu_ref, wd_ref, gf_ref, o_ref,
                      state_ref, upd_ref, sprev_ref, bias_ref, y_cur, y_prev, h_ref, act_ref,
                      *, n_tiles, n_s):
    i = pl.program_id(0)
    seq_start = lax.rem(i, n_s) == 0

    @pl.when(i == 0)
    def _():
        _bias_table_init(bias_ref)

    @pl.when(seq_start)
    def _():
        state_ref[...] = jnp.zeros_like(state_ref)

    first = jnp.where(seq_start, 1, 0)
    mixer_args = (first, sinks_ref, qd_ref, kit_ref, ks_ref, hv_ref, gate_ref, dec_ref, bq_ref,
                  kp_ref, kph_ref, vt_ref, vth_ref, ang_ref, y_cur,
                  state_ref, upd_ref, sprev_ref, bias_ref)
    ffn_args = (x_ref, y_prev, wo_ref, g2_ref, wg_ref, wu_ref, wd_ref, gf_ref, o_ref,
                h_ref, act_ref)

    @pl.when(i == 0)
    def _():
        _run_interleaved(_mixer_stages(*mixer_args), [])
        y_prev[...] = y_cur[...]

    @pl.when((i > 0) & (i < n_tiles))
    def _():
        _run_interleaved(_ffn_stages(*ffn_args), _mixer_stages(*mixer_args))
        y_prev[...] = y_cur[...]

    @pl.when(i == n_tiles)
    def _():
        _run_interleaved(_ffn_stages(*ffn_args), [])


def _mixer_ffn(prep, sinks, attn_norm_g, x2d, w_out, ln2_g, w_gate, w_up, w_down, final_g, seq):
    qd, kit, ks, hv, gate, dec, bq, kp, vt = prep
    t = x2d.shape[0]
    ts = TS_MIX
    n_tiles = t // ts
    n_s = seq // ts
    n_chunk = ts // CHUNK
    n_blk = ts // BLOCK
    cur = lambda i: jnp.minimum(i, n_tiles - 1)
    prev = lambda i: jnp.maximum(i - 1, 0)
    halo = lambda i: jnp.maximum(cur(i) * n_blk - 1, 0)
    row_blk = lambda width: pl.BlockSpec((ts, width), lambda i: (cur(i), 0))
    const = lambda shape: pl.BlockSpec(shape, lambda i: (0, 0), pipeline_mode=pl.Buffered(1))
    return pl.pallas_call(
        functools.partial(_mixer_ffn_kernel, n_tiles=n_tiles, n_s=n_s),
        grid=(n_tiles + 1,),
        in_specs=[
            pl.BlockSpec(memory_space=pltpu.SMEM),
            row_blk(A_QK), pl.BlockSpec((A_QK, ts), lambda i: (0, cur(i))), row_blk(A_QK),
            row_blk(A_WIDTH), row_blk(A_WIDTH),
            pl.BlockSpec((n_chunk, A_QK), lambda i: (cur(i), 0)),
            row_blk(B_WIDTH),
            pl.BlockSpec((4, ts, LANES), lambda i: (0, cur(i), 0)),
            pl.BlockSpec((4, BLOCK, LANES), lambda i: (0, halo(i), 0)),
            pl.BlockSpec((LANES, ts), lambda i: (0, cur(i))),
            pl.BlockSpec((LANES, BLOCK), lambda i: (0, halo(i))),
            pl.BlockSpec((1, B_WIDTH), lambda i: (0, 0)),
            pl.BlockSpec((ts, D_MODEL), lambda i: (prev(i), 0)),
            const((MIX_WIDTH, D_MODEL)),
            const((1, D_MODEL)),
            const((D_MODEL, D_FF)),
            const((D_MODEL, D_FF)),
            const((D_FF, D_MODEL)),
            const((1, D_MODEL)),
        ],
        out_specs=pl.BlockSpec((ts, D_MODEL), lambda i: (prev(i), 0)),
        out_shape=jax.ShapeDtypeStruct((t, D_MODEL), F32),
        scratch_shapes=[
            pltpu.VMEM((A_HEADS, A_DV, A_DK), F32),
            pltpu.VMEM((n_chunk, A_HEADS, A_DV, A_DK), F32),
            pltpu.VMEM((n_chunk, A_HEADS, A_DV, A_DK), BF16),
            pltpu.VMEM((2, B_HEADS, 2 * BLOCK, BLOCK), F32),
            pltpu.VMEM((ts, MIX_WIDTH), BF16),
            pltpu.VMEM((ts, MIX_WIDTH), BF16),
            pltpu.VMEM((ts, D_MODEL), F32),
            pltpu.VMEM((ts, D_FF), BF16),
        ],
        compiler_params=pltpu.CompilerParams(
            dimension_semantics=("arbitrary",), vmem_limit_bytes=V7X_VMEM_LIMIT_BYTES),
        name="mixer_ffn",
    )(sinks, qd, kit, ks, hv, gate, dec, bq, kp, kp, vt, vt, attn_norm_g,
      x2d, w_out, ln2_g, w_gate, w_up, w_down, final_g)


def kernel(x, ln1_g, w_in, lb_logits, hgrn_norm_g, attn_sinks, attn_norm_g, w_out, ln2_g,
           w_gate, w_up, w_down, final_g):
    bsz, seq, d = x.shape
    x2d = x.reshape(bsz * seq, d)
    *prep, wo16, wg16, wu16, wd16 = _in_proj(x2d, ln1_g[0].reshape(d, 1), w_in[0], lb_logits, hgrn_norm_g[0:1],
                                             w_out[0], w_gate[0], w_up[0], w_down[0])
    out = _mixer_ffn(prep, attn_sinks[0], attn_norm_g[0:1], x2d, wo16, ln2_g[0:1],
                     wg16, wu16, wd16, final_g.reshape(1, d), seq)
    return out.reshape(bsz, seq, d)
```

```python
import functools

import jax
import jax.numpy as jnp
from jax import lax
from jax.experimental import pallas as pl
from jax.experimental.pallas import tpu as pltpu

D_MODEL = 1024
A_HEADS = 4
A_DK = 128
A_DV = 128
A_QK = A_HEADS * A_DK
A_WIDTH = A_HEADS * A_DV
CHUNK = 64
B_HEADS = 8
B_KV_HEADS = 2
B_HEAD_DIM = 64
B_GROUP = B_HEADS // B_KV_HEADS
B_WIDTH = B_HEADS * B_HEAD_DIM
B_KV = B_KV_HEADS * B_HEAD_DIM
WINDOW = 128
BLOCK = 128
MIX_WIDTH = A_WIDTH + B_WIDTH
D_FF = 2816
IN_COLS = 4 * A_QK + B_WIDTH + 2 * B_KV
EPS = 1e-6
NEG_BIG = -1e30
LANES = 128

F32 = jnp.float32
BF16 = jnp.bfloat16

V7X_VMEM_LIMIT_BYTES = 56 * 1024 * 1024

TM_PROJ = 512
TS_MIX = 512
FF_CHUNK = 256
HGRN_GROUP_CHUNKS = 2
PAIR = 2 * CHUNK
CAST_ROWS = 128
LOG2E = 1.4426950408889634

NT_DIMS = (((1,), (1,)), ((), ()))
TN_DIMS = (((0,), (0,)), ((), ()))


def _rms_scale(v):
    return lax.rsqrt(jnp.mean(v * v, axis=-1, keepdims=True) + EPS)


def _silu(v):
    return v * (1.0 / (1.0 + jnp.exp(-v)))


def _sigmoid(v):
    return 1.0 / (1.0 + jnp.exp(-v))


def _in_proj_kernel(x_ref, g_ref, w32_ref, lbl_ref, hng_ref, wo32_ref, wg32_ref, wu32_ref, wd32_ref,
                    qd_ref, kit_ref, ks_ref, hv_ref, gate_ref, dec_ref, bq_ref, kp_ref, vt_ref,
                    wo16_ref, wg16_ref, wu16_ref, wd16_ref, w_ref, *, n_down_slabs):
    tm = x_ref.shape[0]

    @pl.when(pl.program_id(0) == 0)
    def _():
        for r0 in range(0, D_MODEL, CAST_ROWS):
            w_ref[r0:r0 + CAST_ROWS, :] = w32_ref[r0:r0 + CAST_ROWS, :].astype(BF16)

    @pl.when(pl.program_id(0) < n_down_slabs)
    def _():
        wd16_ref[...] = wd32_ref[...].astype(BF16)

    x = x_ref[...]
    u = (x * _rms_scale(x) * g_ref[...]).astype(BF16)

    lbl = lbl_ref[...]
    lmax = jnp.max(lbl, axis=0, keepdims=True)
    le = jnp.exp(lbl - lmax)
    lb = le[0:1, :] / jnp.sum(le, axis=0, keepdims=True)

    r = lax.broadcasted_iota(jnp.int32, (CHUNK, CHUNK), 0)
    c = lax.broadcasted_iota(jnp.int32, (CHUNK, CHUNK), 1)
    tri = jnp.where(r >= c, 1.0, 0.0).astype(BF16)

    pf = jnp.dot(u, w_ref[:, A_QK:2 * A_QK], preferred_element_type=F32)
    pq = jnp.dot(u, w_ref[:, 0:A_QK], preferred_element_type=F32)
    k_pair = []
    for ci in range(tm // CHUNK):
        rows = slice(ci * CHUNK, (ci + 1) * CHUNK)
        forget = lb + (1.0 - lb) * _sigmoid(pf[rows])
        key = 1.0 - forget
        logf = jnp.log(forget)
        p0 = logf.astype(BF16)
        r1 = logf - p0.astype(F32)
        p1 = r1.astype(BF16)
        p2 = (r1 - p1.astype(F32)).astype(BF16)
        bc3 = jnp.dot(tri, jnp.concatenate([p0, p1, p2], axis=-1), preferred_element_type=F32)
        bcum = bc3[:, :A_QK] + bc3[:, A_QK:2 * A_QK] + bc3[:, 2 * A_QK:]
        decay = jnp.exp(bcum[CHUNK - 1:CHUNK, :])
        k_intra = key * jnp.exp(-bcum)
        qd_ref[rows, :] = (_silu(pq[rows]) * jnp.exp(bcum)).astype(BF16)
        ks_ref[rows, :] = (k_intra * decay).astype(BF16)
        dec_ref[ci:ci + 1, :] = decay
        k_pair.append(k_intra)
        if len(k_pair) == PAIR // CHUNK:
            p0_row = (ci + 1) * CHUNK - PAIR
            kit_ref[:, p0_row:p0_row + PAIR] = jnp.concatenate(k_pair, axis=0).T.astype(BF16)
            k_pair = []

    wo16_ref[...] = wo32_ref[...].astype(BF16)
    wg16_ref[...] = wg32_ref[...].astype(BF16)
    wu16_ref[...] = wu32_ref[...].astype(BF16)

    c0 = 2 * A_QK + 2 * A_WIDTH
    pa = jnp.dot(u, w_ref[:, c0:], preferred_element_type=F32)
    bq_ref[...] = (pa[:, :B_WIDTH] * (B_HEAD_DIM ** -0.5 * LOG2E)).astype(BF16)
    lo = lax.broadcasted_iota(jnp.int32, (tm, LANES), 1) < B_HEAD_DIM
    k_all = pa[:, B_WIDTH:B_WIDTH + B_KV]
    k_rot = pltpu.roll(k_all, B_HEAD_DIM, axis=1)
    kp_ref[0] = jnp.where(lo, k_all, 0.0).astype(BF16)
    kp_ref[1] = jnp.where(lo, 0.0, k_rot).astype(BF16)
    kp_ref[2] = jnp.where(lo, k_rot, 0.0).astype(BF16)
    kp_ref[3] = jnp.where(lo, 0.0, k_all).astype(BF16)
    vt_ref[...] = pa[:, B_WIDTH + B_KV:].T.astype(BF16)

    pg = jnp.dot(u, w_ref[:, 2 * A_QK + A_WIDTH:2 * A_QK + 2 * A_WIDTH], preferred_element_type=F32)
    gate_ref[...] = (_silu(pg) * hng_ref[...]).astype(BF16)
    pi = jnp.dot(u, w_ref[:, 2 * A_QK:2 * A_QK + A_WIDTH], preferred_element_type=F32)
    hv_ref[...] = pi.astype(BF16)


def _in_proj(x2d, ln1_g, w_in, lb_logits, hgrn_norm_g, w_out, w_gate, w_up, w_down):
    t = x2d.shape[0]
    tm = TM_PROJ
    n_steps = t // tm
    row_blk = lambda width: pl.BlockSpec((tm, width), lambda i: (i, 0))
    const = lambda shape, **kw: pl.BlockSpec(shape, lambda i: (0, 0), **kw)
    bf = lambda shape: jax.ShapeDtypeStruct(shape, BF16)

    def slab(w, rows):
        n_slabs = w.shape[0] // rows
        assert w.shape[0] % rows == 0 and n_slabs <= n_steps, (w.shape, rows)
        return pl.BlockSpec((rows, w.shape[1]), lambda i: (jnp.minimum(i, n_slabs - 1), 0))

    cast_weights = (w_out, w_gate, w_up, w_down)
    even_rows = D_MODEL // n_steps
    slab_rows = (even_rows, even_rows, even_rows, CAST_ROWS)
    slab_specs = lambda: [slab(w, rows) for w, rows in zip(cast_weights, slab_rows)]
    return pl.pallas_call(
        functools.partial(_in_proj_kernel, n_down_slabs=D_FF // CAST_ROWS),
        grid=(n_steps,),
        in_specs=[
            row_blk(D_MODEL),
            const((1, D_MODEL)),
            const((D_MODEL, IN_COLS), pipeline_mode=pl.Buffered(1)),
            const((2, A_QK)),
            const((1, A_WIDTH)),
        ] + slab_specs(),
        out_specs=[
            row_blk(A_QK), pl.BlockSpec((A_QK, tm), lambda i: (0, i)), row_blk(A_QK),
            row_blk(A_WIDTH), row_blk(A_WIDTH),
            pl.BlockSpec((tm // CHUNK, A_QK), lambda i: (i, 0)),
            row_blk(B_WIDTH),
            pl.BlockSpec((4, tm, LANES), lambda i: (0, i, 0)),
            pl.BlockSpec((LANES, tm), lambda i: (0, i)),
        ] + slab_specs(),
        out_shape=[
            bf((t, A_QK)), bf((A_QK, t)), bf((t, A_QK)), bf((t, A_WIDTH)), bf((t, A_WIDTH)),
            jax.ShapeDtypeStruct((t // CHUNK, A_QK), F32),
            bf((t, B_WIDTH)),
            bf((4, t, LANES)),
            bf((LANES, t)),
        ] + [bf(w.shape) for w in cast_weights],
        scratch_shapes=[pltpu.VMEM((D_MODEL, IN_COLS), BF16)],
        compiler_params=pltpu.CompilerParams(
            dimension_semantics=("arbitrary",), vmem_limit_bytes=V7X_VMEM_LIMIT_BYTES),
        name="in_proj",
    )(x2d, ln1_g, w_in, lb_logits, hgrn_norm_g, *cast_weights)


def _bias_table_init(bias_ref):
    kj = lax.broadcasted_iota(jnp.int32, (2 * BLOCK, BLOCK), 0)
    qi = lax.broadcasted_iota(jnp.int32, (2 * BLOCK, BLOCK), 1)
    dist = qi + BLOCK - kj
    valid = (dist >= 0) & (dist < WINDOW)
    valid_first = valid & (kj >= BLOCK)
    distf = dist.astype(F32)
    for h in range(B_HEADS):
        slope = 2.0 ** (-(8.0 / B_HEADS) * (h + 1))
        sc = (-slope * LOG2E) * distf
        bias_ref[0, h] = jnp.where(valid, sc, NEG_BIG)
        bias_ref[1, h] = jnp.where(valid_first, sc, NEG_BIG)


def _mixer_stages(first, sinks_ref, qd_ref, kit_ref, ks_ref, hv_ref, gate_ref, dec_ref, bq_ref,
                  kp_ref, kph_ref, vt_ref, vth_ref, ang_ref, y_ref,
                  state_ref, upd_ref, sprev_ref, bias_ref):
    ts = qd_ref.shape[0]
    n_chunk = ts // CHUNK
    n_blk = ts // BLOCK
    head_cols = [slice(h * A_DK, (h + 1) * A_DK) for h in range(A_HEADS)]

    def chunk_units(c_lo, c_hi):
        return [(ci, slice(ci * CHUNK, (ci + 1) * CHUNK), head_cols[h], h)
                for ci in range(c_lo, c_hi) for h in range(A_HEADS)]

    def hgrn_updates(c_lo, c_hi):
        units = chunk_units(c_lo, c_hi)
        upds = [lax.dot_general(hv_ref[rows, cs], ks_ref[rows, cs], TN_DIMS,
                                preferred_element_type=F32) for (_, rows, cs, _) in units]
        for upd, (ci, _, _, h) in zip(upds, units):
            upd_ref[ci, h] = upd

    def hgrn_scan(h):
        st = state_ref[h]
        for ci in range(n_chunk):
            sprev_ref[ci, h] = st.T.astype(BF16)
            st = st * dec_ref[ci:ci + 1, head_cols[h]] + upd_ref[ci, h]
        state_ref[h] = st

    def hgrn_scores(p, box):
        rows = slice(p * PAIR, (p + 1) * PAIR)
        box["a"] = [jnp.dot(qd_ref[rows, cs], kit_ref[cs, rows], preferred_element_type=F32)
                    for cs in head_cols]
        box["o_inter"] = [
            jnp.concatenate(
                [jnp.dot(qd_ref[ci * CHUNK:(ci + 1) * CHUNK, cs], sprev_ref[ci, h],
                         preferred_element_type=F32)
                 for ci in range(p * PAIR // CHUNK, (p + 1) * PAIR // CHUNK)], axis=0)
            for h, cs in enumerate(head_cols)]

    def hgrn_values(p, box):
        rows = slice(p * PAIR, (p + 1) * PAIR)
        r = lax.broadcasted_iota(jnp.int32, (PAIR, PAIR), 0)
        c = lax.broadcasted_iota(jnp.int32, (PAIR, PAIR), 1)
        same_chunk_causal = (c <= r) & (c >= (r // CHUNK) * CHUNK)
        a_s = [jnp.where(same_chunk_causal, a, 0.0).astype(BF16) for a in box["a"]]
        box["o"] = [jnp.dot(a, hv_ref[rows, cs], preferred_element_type=F32) + oi
                    for a, oi, cs in zip(a_s, box["o_inter"], head_cols)]

    def hgrn_store(p, box):
        rows = slice(p * PAIR, (p + 1) * PAIR)
        for o, cs in zip(box["o"], head_cols):
            gate = gate_ref[rows, cs].astype(F32)
            y_ref[rows, cs] = (o * _rms_scale(o) * gate).astype(y_ref.dtype)

    swa_units = [(kvh, par) for kvh in range(B_KV_HEADS) for par in range(2)]

    def swa_scores(j, box):
        s_ts = []
        for kvh, par in swa_units:
            rows = slice(j * BLOCK, (j + 1) * BLOCK)
            q_pairs = jnp.concatenate(
                [bq_ref[rows, (2 * kvh) * LANES:(2 * kvh + 1) * LANES],
                 bq_ref[rows, (2 * kvh + 1) * LANES:(2 * kvh + 2) * LANES]], axis=0)
            idx = 2 * kvh + par
            if j == 0:
                kk = jnp.concatenate([kph_ref[idx], kp_ref[idx, 0:BLOCK, :]], axis=0)
            else:
                kk = kp_ref[idx, (j - 1) * BLOCK:(j + 1) * BLOCK, :]
            s_ts.append(lax.dot_general(kk, q_pairs, NT_DIMS, preferred_element_type=F32))
        box["s"] = s_ts

    def swa_softmax(j, box):
        p_ts, inv_ls = [], []
        for (kvh, par), s_t in zip(swa_units, box["s"]):
            p_halves = []
            for half, h in enumerate((4 * kvh + par, 4 * kvh + 2 + par)):
                bias = bias_ref[first, h] if j == 0 else bias_ref[0, h]
                s = s_t[:, half * BLOCK:(half + 1) * BLOCK] + bias
                sink = sinks_ref[h] * LOG2E
                m = jnp.maximum(jnp.max(s, axis=0, keepdims=True), sink)
                p = jnp.exp2(s - m)
                inv_ls.append(1.0 / (jnp.sum(p, axis=0, keepdims=True) + jnp.exp2(sink - m)))
                p_halves.append(p.astype(BF16))
            p_ts.append(jnp.concatenate(p_halves, axis=1))
        box["p"], box["inv_l"] = p_ts, inv_ls

    def swa_values(j, box):
        o_t = [None] * B_HEADS
        for ui, ((kvh, par), p_t) in enumerate(zip(swa_units, box["p"])):
            drow = slice(kvh * B_HEAD_DIM, (kvh + 1) * B_HEAD_DIM)
            if j == 0:
                vt_h = jnp.concatenate([vth_ref[drow, :], vt_ref[drow, 0:BLOCK]], axis=1)
            else:
                vt_h = vt_ref[drow, (j - 1) * BLOCK:(j + 1) * BLOCK]
            pv = jnp.dot(vt_h, p_t, preferred_element_type=F32)
            for half, h in enumerate((4 * kvh + par, 4 * kvh + 2 + par)):
                o_t[h] = pv[:, half * BLOCK:(half + 1) * BLOCK] * box["inv_l"][2 * ui + half]
        box["o_t"] = o_t

    def swa_store(j, box):
        o_all_t = jnp.concatenate(box["o_t"], axis=0)
        scale = lax.rsqrt(jnp.sum(o_all_t * o_all_t, axis=0, keepdims=True) * (1.0 / B_WIDTH) + EPS)
        o_all = (o_all_t * scale).T
        y_ref[j * BLOCK:(j + 1) * BLOCK, A_WIDTH:] = (o_all * ang_ref[...]).astype(y_ref.dtype)

    group = HGRN_GROUP_CHUNKS
    hgrn = [functools.partial(hgrn_updates, c, c + group) for c in range(0, n_chunk, group)]
    hgrn += [functools.partial(hgrn_scan, h) for h in range(A_HEADS)]
    for p in range(ts // PAIR):
        box = {}
        hgrn += [functools.partial(f, p, box) for f in (hgrn_scores, hgrn_values, hgrn_store)]
    swa = []
    for j in range(n_blk):
        box = {}
        swa += [functools.partial(f, j, box) for f in (swa_scores, swa_softmax, swa_values, swa_store)]
    stages = []
    for k in range(max(len(hgrn), len(swa))):
        stages += hgrn[k:k + 1] + swa[k:k + 1]
    return stages


def _ffn_stages(x_ref, y_ref, wo_ref, g2_ref, wg_ref, wu_ref, wd_ref, gf_ref, o_ref,
                h_ref, act_ref):
    box = {}

    def head():
        h = x_ref[...] + jnp.dot(y_ref[...], wo_ref[...], preferred_element_type=F32)
        box["u"] = (h * _rms_scale(h) * g2_ref[...]).astype(BF16)
        h_ref[...] = h

    def ff_up(c0):
        cs = slice(c0, c0 + FF_CHUNK)
        box["gate"] = jnp.dot(box["u"], wg_ref[:, cs], preferred_element_type=F32)
        box["up"] = jnp.dot(box["u"], wu_ref[:, cs], preferred_element_type=F32)

    def ff_act(c0):
        act_ref[:, c0:c0 + FF_CHUNK] = (_silu(box["gate"]) * box["up"]).astype(BF16)

    def tail():
        h2 = h_ref[...] + jnp.dot(act_ref[...], wd_ref[...], preferred_element_type=F32)
        o_ref[...] = h2 * _rms_scale(h2) * gf_ref[...]

    stages = [head]
    for c0 in range(0, D_FF, FF_CHUNK):
        stages += [functools.partial(ff_up, c0), functools.partial(ff_act, c0)]
    return stages + [tail]


def _run_interleaved(major, minor):
    span = max(len(major) - 2, 1)
    done = 0
    for k, stage in enumerate(major):
        stage()
        want = min(len(minor), -(-(k + 1) * len(minor) // span))
        while done < want:
            minor[done]()
            done += 1


def _mixer_ffn_kernel(sinks_ref, qd_ref, kit_ref, ks_ref, hv_ref, gate_ref, dec_ref, bq_ref,
                      kp_ref, kph_ref, vt_ref, vth_ref, ang_ref,
                      x_ref, wo_hbm, g2_ref, wg_hbm, wu_hbm, wd_hbm, gf_ref, o_ref,
                      state_ref, upd_ref, sprev_ref, bias_ref, y_cur, y_prev, h_ref, act_ref,
                      wo_ref, wg_ref, wu_ref, wd_ref, w_sems, *, n_tiles, n_s):
    i = pl.program_id(0)
    seq_start = lax.rem(i, n_s) == 0
    weight_copies = [
        pltpu.make_async_copy(src, dst, w_sems.at[k])
        for k, (src, dst) in enumerate(((wo_hbm, wo_ref), (wg_hbm, wg_ref),
                                        (wu_hbm, wu_ref), (wd_hbm, wd_ref)))]

    @pl.when(i == 0)
    def _():
        for copy in weight_copies:
            copy.start()
        _bias_table_init(bias_ref)

    @pl.when(i == 1)
    def _():
        for copy in weight_copies:
            copy.wait()

    @pl.when(seq_start)
    def _():
        state_ref[...] = jnp.zeros_like(state_ref)

    first = jnp.where(seq_start, 1, 0)
    mixer_args = (first, sinks_ref, qd_ref, kit_ref, ks_ref, hv_ref, gate_ref, dec_ref, bq_ref,
                  kp_ref, kph_ref, vt_ref, vth_ref, ang_ref, y_cur,
                  state_ref, upd_ref, sprev_ref, bias_ref)
    ffn_args = (x_ref, y_prev, wo_ref, g2_ref, wg_ref, wu_ref, wd_ref, gf_ref, o_ref,
                h_ref, act_ref)

    @pl.when(i == 0)
    def _():
        _run_interleaved(_mixer_stages(*mixer_args), [])
        y_prev[...] = y_cur[...]

    @pl.when((i > 0) & (i < n_tiles))
    def _():
        _run_interleaved(_ffn_stages(*ffn_args), _mixer_stages(*mixer_args))
        y_prev[...] = y_cur[...]

    @pl.when(i == n_tiles)
    def _():
        _run_interleaved(_ffn_stages(*ffn_args), [])


def _mixer_ffn(prep, sinks, attn_norm_g, x2d, w_out, ln2_g, w_gate, w_up, w_down, final_g, seq):
    qd, kit, ks, hv, gate, dec, bq, kp, vt = prep
    t = x2d.shape[0]
    ts = TS_MIX
    n_tiles = t // ts
    n_s = seq // ts
    n_chunk = ts // CHUNK
    n_blk = ts // BLOCK
    cur = lambda i: jnp.minimum(i, n_tiles - 1)
    prev = lambda i: jnp.maximum(i - 1, 0)
    halo = lambda i: jnp.maximum(cur(i) * n_blk - 1, 0)
    row_blk = lambda width: pl.BlockSpec((ts, width), lambda i: (cur(i), 0))
    const = lambda shape: pl.BlockSpec(shape, lambda i: (0, 0), pipeline_mode=pl.Buffered(1))
    in_hbm = pl.BlockSpec(memory_space=pl.ANY)
    assert n_tiles >= 1
    return pl.pallas_call(
        functools.partial(_mixer_ffn_kernel, n_tiles=n_tiles, n_s=n_s),
        grid=(n_tiles + 1,),
        in_specs=[
            pl.BlockSpec(memory_space=pltpu.SMEM),
            row_blk(A_QK), pl.BlockSpec((A_QK, ts), lambda i: (0, cur(i))), row_blk(A_QK),
            row_blk(A_WIDTH), row_blk(A_WIDTH),
            pl.BlockSpec((n_chunk, A_QK), lambda i: (cur(i), 0)),
            row_blk(B_WIDTH),
            pl.BlockSpec((4, ts, LANES), lambda i: (0, cur(i), 0)),
            pl.BlockSpec((4, BLOCK, LANES), lambda i: (0, halo(i), 0)),
            pl.BlockSpec((LANES, ts), lambda i: (0, cur(i))),
            pl.BlockSpec((LANES, BLOCK), lambda i: (0, halo(i))),
            pl.BlockSpec((1, B_WIDTH), lambda i: (0, 0)),
            pl.BlockSpec((ts, D_MODEL), lambda i: (prev(i), 0)),
            in_hbm,
            const((1, D_MODEL)),
            in_hbm, in_hbm, in_hbm,
            const((1, D_MODEL)),
        ],
        out_specs=pl.BlockSpec((ts, D_MODEL), lambda i: (prev(i), 0)),
        out_shape=jax.ShapeDtypeStruct((t, D_MODEL), F32),
        scratch_shapes=[
            pltpu.VMEM((A_HEADS, A_DV, A_DK), F32),
            pltpu.VMEM((n_chunk, A_HEADS, A_DV, A_DK), F32),
            pltpu.VMEM((n_chunk, A_HEADS, A_DV, A_DK), BF16),
            pltpu.VMEM((2, B_HEADS, 2 * BLOCK, BLOCK), F32),
            pltpu.VMEM((ts, MIX_WIDTH), BF16),
            pltpu.VMEM((ts, MIX_WIDTH), BF16),
            pltpu.VMEM((ts, D_MODEL), F32),
            pltpu.VMEM((ts, D_FF), BF16),
            pltpu.VMEM((MIX_WIDTH, D_MODEL), BF16),
            pltpu.VMEM((D_MODEL, D_FF), BF16),
            pltpu.VMEM((D_MODEL, D_FF), BF16),
            pltpu.VMEM((D_FF, D_MODEL), BF16),
            pltpu.SemaphoreType.DMA((4,)),
        ],
        compiler_params=pltpu.CompilerParams(
            dimension_semantics=("arbitrary",), vmem_limit_bytes=V7X_VMEM_LIMIT_BYTES),
        name="mixer_ffn",
    )(sinks, qd, kit, ks, hv, gate, dec, bq, kp, kp, vt, vt, attn_norm_g,
      x2d, w_out, ln2_g, w_gate, w_up, w_down, final_g)


def kernel(x, ln1_g, w_in, lb_logits, hgrn_norm_g, attn_sinks, attn_norm_g, w_out, ln2_g,
           w_gate, w_up, w_down, final_g):
    bsz, seq, d = x.shape
    x2d = x.reshape(bsz * seq, d)
    *prep, wo16, wg16, wu16, wd16 = _in_proj(x2d, ln1_g[0:1], w_in[0], lb_logits, hgrn_norm_g[0:1],
                                             w_out[0], w_gate[0], w_up[0], w_down[0])
    out = _mixer_ffn(prep, attn_sinks[0], attn_norm_g[0:1], x2d, wo16, ln2_g[0:1],
                     wg16, wu16, wd16, final_g.reshape(1, d), seq)
    return out.reshape(bsz, seq, d)
```

```python
import functools

import jax
import jax.numpy as jnp
from jax import lax
from jax.experimental import pallas as pl
from jax.experimental.pallas import tpu as pltpu

D_MODEL = 1024
A_HEADS = 4
A_DK = 128
A_DV = 128
A_QK = A_HEADS * A_DK
A_WIDTH = A_HEADS * A_DV
CHUNK = 64
B_HEADS = 8
B_KV_HEADS = 2
B_HEAD_DIM = 64
B_GROUP = B_HEADS // B_KV_HEADS
B_WIDTH = B_HEADS * B_HEAD_DIM
B_KV = B_KV_HEADS * B_HEAD_DIM
WINDOW = 128
BLOCK = 128
MIX_WIDTH = A_WIDTH + B_WIDTH
D_FF = 2816
IN_COLS = 4 * A_QK + B_WIDTH + 2 * B_KV
EPS = 1e-6
NEG_BIG = -1e30
LANES = 128

F32 = jnp.float32
BF16 = jnp.bfloat16

V7X_VMEM_LIMIT_BYTES = 56 * 1024 * 1024

TM_PROJ = 512
TS_MIX = 512
FF_CHUNK = 256
HGRN_GROUP_CHUNKS = 2
PAIR = 2 * CHUNK
CAST_ROWS = 128
LOG2E = 1.4426950408889634

NT_DIMS = (((1,), (1,)), ((), ()))
TN_DIMS = (((0,), (0,)), ((), ()))


def _rms_scale(v):
    return lax.rsqrt(jnp.mean(v * v, axis=-1, keepdims=True) + EPS)


def _silu(v):
    return v * (1.0 / (1.0 + jnp.exp(-v)))


def _sigmoid(v):
    return 1.0 / (1.0 + jnp.exp(-v))


def _in_proj_kernel(x_ref, g_ref, w32_ref, lbl_ref, hng_ref, wo32_ref, wg32_ref, wu32_ref, wd32_ref,
                    qd_ref, kit_ref, ks_ref, hv_ref, gate_ref, dec_ref, bq_ref, kp_ref, vt_ref,
                    wo16_ref, wg16_ref, wu16_ref, wd16_ref, w_ref, *, n_down_slabs):
    tm = x_ref.shape[0]

    @pl.when(pl.program_id(0) == 0)
    def _():
        for r0 in range(0, D_MODEL, CAST_ROWS):
            w_ref[r0:r0 + CAST_ROWS, :] = w32_ref[r0:r0 + CAST_ROWS, :].astype(BF16)

    @pl.when(pl.program_id(0) < n_down_slabs)
    def _():
        wd16_ref[...] = wd32_ref[...].astype(BF16)

    x = x_ref[...]
    u = (x * _rms_scale(x) * g_ref[...]).astype(BF16)

    lbl = lbl_ref[...]
    lmax = jnp.max(lbl, axis=0, keepdims=True)
    le = jnp.exp(lbl - lmax)
    lb = le[0:1, :] / jnp.sum(le, axis=0, keepdims=True)

    r = lax.broadcasted_iota(jnp.int32, (CHUNK, CHUNK), 0)
    c = lax.broadcasted_iota(jnp.int32, (CHUNK, CHUNK), 1)
    tri = jnp.where(r >= c, 1.0, 0.0).astype(BF16)

    pf = jnp.dot(u, w_ref[:, A_QK:2 * A_QK], preferred_element_type=F32)
    pq = jnp.dot(u, w_ref[:, 0:A_QK], preferred_element_type=F32)
    k_pair = []
    for ci in range(tm // CHUNK):
        rows = slice(ci * CHUNK, (ci + 1) * CHUNK)
        forget = lb + (1.0 - lb) * _sigmoid(pf[rows])
        key = 1.0 - forget
        logf = jnp.log(forget)
        p0 = logf.astype(BF16)
        p1 = (logf - p0.astype(F32)).astype(BF16)
        bc2 = jnp.dot(tri, jnp.concatenate([p0, p1], axis=-1), preferred_element_type=F32)
        bcum = bc2[:, :A_QK] + bc2[:, A_QK:]
        decay = jnp.exp(bcum[CHUNK - 1:CHUNK, :])
        k_intra = key * jnp.exp(-bcum)
        qd_ref[rows, :] = (_silu(pq[rows]) * jnp.exp(bcum)).astype(BF16)
        ks_ref[rows, :] = (k_intra * decay).astype(BF16)
        dec_ref[ci:ci + 1, :] = decay
        k_pair.append(k_intra)
        if len(k_pair) == PAIR // CHUNK:
            p0_row = (ci + 1) * CHUNK - PAIR
            kit_ref[:, p0_row:p0_row + PAIR] = jnp.concatenate(k_pair, axis=0).T.astype(BF16)
            k_pair = []

    wo16_ref[...] = wo32_ref[...].astype(BF16)
    wg16_ref[...] = wg32_ref[...].astype(BF16)
    wu16_ref[...] = wu32_ref[...].astype(BF16)

    c0 = 2 * A_QK + 2 * A_WIDTH
    pa = jnp.dot(u, w_ref[:, c0:], preferred_element_type=F32)
    bq_ref[...] = (pa[:, :B_WIDTH] * (B_HEAD_DIM ** -0.5 * LOG2E)).astype(BF16)
    lo = lax.broadcasted_iota(jnp.int32, (tm, LANES), 1) < B_HEAD_DIM
    k_all = pa[:, B_WIDTH:B_WIDTH + B_KV]
    k_rot = pltpu.roll(k_all, B_HEAD_DIM, axis=1)
    kp_ref[0] = jnp.where(lo, k_all, 0.0).astype(BF16)
    kp_ref[1] = jnp.where(lo, 0.0, k_rot).astype(BF16)
    kp_ref[2] = jnp.where(lo, k_rot, 0.0).astype(BF16)
    kp_ref[3] = jnp.where(lo, 0.0, k_all).astype(BF16)
    vt_ref[...] = pa[:, B_WIDTH + B_KV:].T.astype(BF16)

    pg = jnp.dot(u, w_ref[:, 2 * A_QK + A_WIDTH:2 * A_QK + 2 * A_WIDTH], preferred_element_type=F32)
    gate_ref[...] = (_silu(pg) * hng_ref[...]).astype(BF16)
    pi = jnp.dot(u, w_ref[:, 2 * A_QK:2 * A_QK + A_WIDTH], preferred_element_type=F32)
    hv_ref[...] = pi.astype(BF16)


def _in_proj(x2d, ln1_g, w_in, lb_logits, hgrn_norm_g, w_out, w_gate, w_up, w_down):
    t = x2d.shape[0]
    tm = TM_PROJ
    n_steps = t // tm
    row_blk = lambda width: pl.BlockSpec((tm, width), lambda i: (i, 0))
    const = lambda shape, **kw: pl.BlockSpec(shape, lambda i: (0, 0), **kw)
    bf = lambda shape: jax.ShapeDtypeStruct(shape, BF16)

    def slab(w, rows):
        n_slabs = w.shape[0] // rows
        assert w.shape[0] % rows == 0 and n_slabs <= n_steps, (w.shape, rows)
        return pl.BlockSpec((rows, w.shape[1]), lambda i: (jnp.minimum(i, n_slabs - 1), 0))

    cast_weights = (w_out, w_gate, w_up, w_down)
    even_rows = D_MODEL // n_steps
    slab_rows = (even_rows, even_rows, even_rows, CAST_ROWS)
    slab_specs = lambda: [slab(w, rows) for w, rows in zip(cast_weights, slab_rows)]
    return pl.pallas_call(
        functools.partial(_in_proj_kernel, n_down_slabs=D_FF // CAST_ROWS),
        grid=(n_steps,),
        in_specs=[
            row_blk(D_MODEL),
            const((1, D_MODEL)),
            const((D_MODEL, IN_COLS), pipeline_mode=pl.Buffered(1)),
            const((2, A_QK)),
            const((1, A_WIDTH)),
        ] + slab_specs(),
        out_specs=[
            row_blk(A_QK), pl.BlockSpec((A_QK, tm), lambda i: (0, i)), row_blk(A_QK),
            row_blk(A_WIDTH), row_blk(A_WIDTH),
            pl.BlockSpec((tm // CHUNK, A_QK), lambda i: (i, 0)),
            row_blk(B_WIDTH),
            pl.BlockSpec((4, tm, LANES), lambda i: (0, i, 0)),
            pl.BlockSpec((LANES, tm), lambda i: (0, i)),
        ] + slab_specs(),
        out_shape=[
            bf((t, A_QK)), bf((A_QK, t)), bf((t, A_QK)), bf((t, A_WIDTH)), bf((t, A_WIDTH)),
            jax.ShapeDtypeStruct((t // CHUNK, A_QK), F32),
            bf((t, B_WIDTH)),
            bf((4, t, LANES)),
            bf((LANES, t)),
        ] + [bf(w.shape) for w in cast_weights],
        scratch_shapes=[pltpu.VMEM((D_MODEL, IN_COLS), BF16)],
        compiler_params=pltpu.CompilerParams(
            dimension_semantics=("arbitrary",), vmem_limit_bytes=V7X_VMEM_LIMIT_BYTES),
        name="in_proj",
    )(x2d, ln1_g, w_in, lb_logits, hgrn_norm_g, *cast_weights)


def _bias_table_init(bias_ref):
    kj = lax.broadcasted_iota(jnp.int32, (2 * BLOCK, BLOCK), 0)
    qi = lax.broadcasted_iota(jnp.int32, (2 * BLOCK, BLOCK), 1)
    dist = qi + BLOCK - kj
    valid = (dist >= 0) & (dist < WINDOW)
    valid_first = valid & (kj >= BLOCK)
    distf = dist.astype(F32)
    for h in range(B_HEADS):
        slope = 2.0 ** (-(8.0 / B_HEADS) * (h + 1))
        sc = (-slope * LOG2E) * distf
        bias_ref[0, h] = jnp.where(valid, sc, NEG_BIG)
        bias_ref[1, h] = jnp.where(valid_first, sc, NEG_BIG)


def _mixer_stages(first, sinks_ref, qd_ref, kit_ref, ks_ref, hv_ref, gate_ref, dec_ref, bq_ref,
                  kp_ref, kph_ref, vt_ref, vth_ref, ang_ref, y_ref,
                  state_ref, upd_ref, sprev_ref, bias_ref):
    ts = qd_ref.shape[0]
    n_chunk = ts // CHUNK
    n_blk = ts // BLOCK
    head_cols = [slice(h * A_DK, (h + 1) * A_DK) for h in range(A_HEADS)]

    def chunk_units(c_lo, c_hi):
        return [(ci, slice(ci * CHUNK, (ci + 1) * CHUNK), head_cols[h], h)
                for ci in range(c_lo, c_hi) for h in range(A_HEADS)]

    def hgrn_updates(c_lo, c_hi):
        units = chunk_units(c_lo, c_hi)
        upds = [lax.dot_general(hv_ref[rows, cs], ks_ref[rows, cs], TN_DIMS,
                                preferred_element_type=F32) for (_, rows, cs, _) in units]
        for upd, (ci, _, _, h) in zip(upds, units):
            upd_ref[ci, h] = upd

    def hgrn_scan(h):
        st = state_ref[h]
        for ci in range(n_chunk):
            sprev_ref[ci, h] = st.T.astype(BF16)
            st = st * dec_ref[ci:ci + 1, head_cols[h]] + upd_ref[ci, h]
        state_ref[h] = st

    def hgrn_scores(p, box):
        rows = slice(p * PAIR, (p + 1) * PAIR)
        box["a"] = [jnp.dot(qd_ref[rows, cs], kit_ref[cs, rows], preferred_element_type=F32)
                    for cs in head_cols]
        box["o_inter"] = [
            jnp.concatenate(
                [jnp.dot(qd_ref[ci * CHUNK:(ci + 1) * CHUNK, cs], sprev_ref[ci, h],
                         preferred_element_type=F32)
                 for ci in range(p * PAIR // CHUNK, (p + 1) * PAIR // CHUNK)], axis=0)
            for h, cs in enumerate(head_cols)]

    def hgrn_values(p, box):
        rows = slice(p * PAIR, (p + 1) * PAIR)
        r = lax.broadcasted_iota(jnp.int32, (PAIR, PAIR), 0)
        c = lax.broadcasted_iota(jnp.int32, (PAIR, PAIR), 1)
        same_chunk_causal = (c <= r) & (c >= (r // CHUNK) * CHUNK)
        a_s = [jnp.where(same_chunk_causal, a, 0.0).astype(BF16) for a in box["a"]]
        box["o"] = [jnp.dot(a, hv_ref[rows, cs], preferred_element_type=F32) + oi
                    for a, oi, cs in zip(a_s, box["o_inter"], head_cols)]

    def hgrn_store(p, box):
        rows = slice(p * PAIR, (p + 1) * PAIR)
        for o, cs in zip(box["o"], head_cols):
            gate = gate_ref[rows, cs].astype(F32)
            y_ref[rows, cs] = (o * _rms_scale(o) * gate).astype(y_ref.dtype)

    swa_units = [(kvh, par) for kvh in range(B_KV_HEADS) for par in range(2)]

    def swa_scores(j, box):
        s_ts = []
        for kvh, par in swa_units:
            rows = slice(j * BLOCK, (j + 1) * BLOCK)
            q_pairs = jnp.concatenate(
                [bq_ref[rows, (2 * kvh) * LANES:(2 * kvh + 1) * LANES],
                 bq_ref[rows, (2 * kvh + 1) * LANES:(2 * kvh + 2) * LANES]], axis=0)
            idx = 2 * kvh + par
            if j == 0:
                kk = jnp.concatenate([kph_ref[idx], kp_ref[idx, 0:BLOCK, :]], axis=0)
            else:
                kk = kp_ref[idx, (j - 1) * BLOCK:(j + 1) * BLOCK, :]
            s_ts.append(lax.dot_general(kk, q_pairs, NT_DIMS, preferred_element_type=F32))
        box["s"] = s_ts

    def swa_softmax(j, box):
        p_ts, inv_ls = [], []
        for (kvh, par), s_t in zip(swa_units, box["s"]):
            p_halves = []
            for half, h in enumerate((4 * kvh + par, 4 * kvh + 2 + par)):
                bias = bias_ref[first, h] if j == 0 else bias_ref[0, h]
                s = s_t[:, half * BLOCK:(half + 1) * BLOCK] + bias
                sink = sinks_ref[h] * LOG2E
                m = jnp.maximum(jnp.max(s, axis=0, keepdims=True), sink)
                p = jnp.exp2(s - m)
                inv_ls.append(1.0 / (jnp.sum(p, axis=0, keepdims=True) + jnp.exp2(sink - m)))
                p_halves.append(p.astype(BF16))
            p_ts.append(jnp.concatenate(p_halves, axis=1))
        box["p"], box["inv_l"] = p_ts, inv_ls

    def swa_values(j, box):
        o_t = [None] * B_HEADS
        for ui, ((kvh, par), p_t) in enumerate(zip(swa_units, box["p"])):
            drow = slice(kvh * B_HEAD_DIM, (kvh + 1) * B_HEAD_DIM)
            if j == 0:
                vt_h = jnp.concatenate([vth_ref[drow, :], vt_ref[drow, 0:BLOCK]], axis=1)
            else:
                vt_h = vt_ref[drow, (j - 1) * BLOCK:(j + 1) * BLOCK]
            pv = jnp.dot(vt_h, p_t, preferred_element_type=F32)
            for half, h in enumerate((4 * kvh + par, 4 * kvh + 2 + par)):
                o_t[h] = pv[:, half * BLOCK:(half + 1) * BLOCK] * box["inv_l"][2 * ui + half]
        box["o_t"] = o_t

    def swa_store(j, box):
        o_all_t = jnp.concatenate(box["o_t"], axis=0)
        scale = lax.rsqrt(jnp.sum(o_all_t * o_all_t, axis=0, keepdims=True) * (1.0 / B_WIDTH) + EPS)
        o_all = (o_all_t * scale).T
        y_ref[j * BLOCK:(j + 1) * BLOCK, A_WIDTH:] = (o_all * ang_ref[...]).astype(y_ref.dtype)

    group = HGRN_GROUP_CHUNKS
    hgrn = [functools.partial(hgrn_updates, c, c + group) for c in range(0, n_chunk, group)]
    hgrn += [functools.partial(hgrn_scan, h) for h in range(A_HEADS)]
    for p in range(ts // PAIR):
        box = {}
        hgrn += [functools.partial(f, p, box) for f in (hgrn_scores, hgrn_values, hgrn_store)]
    swa = []
    for j in range(n_blk):
        box = {}
        swa += [functools.partial(f, j, box) for f in (swa_scores, swa_softmax, swa_values, swa_store)]
    stages = []
    for k in range(max(len(hgrn), len(swa))):
        stages += hgrn[k:k + 1] + swa[k:k + 1]
    return stages


def _ffn_stages(x_ref, y_ref, wo_ref, g2_ref, wg_ref, wu_ref, wd_ref, gf_ref, o_ref,
                h_ref, act_ref):
    box = {}

    def head():
        h = x_ref[...] + jnp.dot(y_ref[...], wo_ref[...], preferred_element_type=F32)
        box["u"] = (h * _rms_scale(h) * g2_ref[...]).astype(BF16)
        h_ref[...] = h

    def ff_up(c0):
        cs = slice(c0, c0 + FF_CHUNK)
        box["gate"] = jnp.dot(box["u"], wg_ref[:, cs], preferred_element_type=F32)
        box["up"] = jnp.dot(box["u"], wu_ref[:, cs], preferred_element_type=F32)

    def ff_act(c0):
        act_ref[:, c0:c0 + FF_CHUNK] = (_silu(box["gate"]) * box["up"]).astype(BF16)

    def tail():
        h2 = h_ref[...] + jnp.dot(act_ref[...], wd_ref[...], preferred_element_type=F32)
        o_ref[...] = h2 * _rms_scale(h2) * gf_ref[...]

    stages = [head]
    for c0 in range(0, D_FF, FF_CHUNK):
        stages += [functools.partial(ff_up, c0), functools.partial(ff_act, c0)]
    return stages + [tail]


def _run_interleaved(major, minor):
    span = max(len(major) - 2, 1)
    done = 0
    for k, stage in enumerate(major):
        stage()
        want = min(len(minor), -(-(k + 1) * len(minor) // span))
        while done < want:
            minor[done]()
            done += 1


def _mixer_ffn_kernel(sinks_ref, qd_ref, kit_ref, ks_ref, hv_ref, gate_ref, dec_ref, bq_ref,
                      kp_ref, kph_ref, vt_ref, vth_ref, ang_ref,
                      x_ref, wo_hbm, g2_ref, wg_hbm, wu_hbm, wd_hbm, gf_ref, o_ref,
                      state_ref, upd_ref, sprev_ref, bias_ref, y_cur, y_prev, h_ref, act_ref,
                      wo_ref, wg_ref, wu_ref, wd_ref, w_sems, *, n_tiles, n_s):
    i = pl.program_id(0)
    seq_start = lax.rem(i, n_s) == 0
    weight_copies = [
        pltpu.make_async_copy(src, dst, w_sems.at[k])
        for k, (src, dst) in enumerate(((wo_hbm, wo_ref), (wg_hbm, wg_ref),
                                        (wu_hbm, wu_ref), (wd_hbm, wd_ref)))]

    @pl.when(i == 0)
    def _():
        for copy in weight_copies:
            copy.start()
        _bias_table_init(bias_ref)

    @pl.when(i == 1)
    def _():
        for copy in weight_copies:
            copy.wait()

    @pl.when(seq_start)
    def _():
        state_ref[...] = jnp.zeros_like(state_ref)

    first = jnp.where(seq_start, 1, 0)
    mixer_args = (first, sinks_ref, qd_ref, kit_ref, ks_ref, hv_ref, gate_ref, dec_ref, bq_ref,
                  kp_ref, kph_ref, vt_ref, vth_ref, ang_ref, y_cur,
                  state_ref, upd_ref, sprev_ref, bias_ref)
    ffn_args = (x_ref, y_prev, wo_ref, g2_ref, wg_ref, wu_ref, wd_ref, gf_ref, o_ref,
                h_ref, act_ref)

    @pl.when(i == 0)
    def _():
        _run_interleaved(_mixer_stages(*mixer_args), [])
        y_prev[...] = y_cur[...]

    @pl.when((i > 0) & (i < n_tiles))
    def _():
        _run_interleaved(_ffn_stages(*ffn_args), _mixer_stages(*mixer_args))
        y_prev[...] = y_cur[...]

    @pl.when(i == n_tiles)
    def _():
        _run_interleaved(_ffn_stages(*ffn_args), [])


def _mixer_ffn(prep, sinks, attn_norm_g, x2d, w_out, ln2_g, w_gate, w_up, w_down, final_g, seq):
    qd, kit, ks, hv, gate, dec, bq, kp, vt = prep
    t = x2d.shape[0]
    ts = TS_MIX
    n_tiles = t // ts
    n_s = seq // ts
    n_chunk = ts // CHUNK
    n_blk = ts // BLOCK
    cur = lambda i: jnp.minimum(i, n_tiles - 1)
    prev = lambda i: jnp.maximum(i - 1, 0)
    halo = lambda i: jnp.maximum(cur(i) * n_blk - 1, 0)
    row_blk = lambda width: pl.BlockSpec((ts, width), lambda i: (cur(i), 0))
    const = lambda shape: pl.BlockSpec(shape, lambda i: (0, 0), pipeline_mode=pl.Buffered(1))
    in_hbm = pl.BlockSpec(memory_space=pl.ANY)
    assert n_tiles >= 1
    return pl.pallas_call(
        functools.partial(_mixer_ffn_kernel, n_tiles=n_tiles, n_s=n_s),
        grid=(n_tiles + 1,),
        in_specs=[
            pl.BlockSpec(memory_space=pltpu.SMEM),
            row_blk(A_QK), pl.BlockSpec((A_QK, ts), lambda i: (0, cur(i))), row_blk(A_QK),
            row_blk(A_WIDTH), row_blk(A_WIDTH),
            pl.BlockSpec((n_chunk, A_QK), lambda i: (cur(i), 0)),
            row_blk(B_WIDTH),
            pl.BlockSpec((4, ts, LANES), lambda i: (0, cur(i), 0)),
            pl.BlockSpec((4, BLOCK, LANES), lambda i: (0, halo(i), 0)),
            pl.BlockSpec((LANES, ts), lambda i: (0, cur(i))),
            pl.BlockSpec((LANES, BLOCK), lambda i: (0, halo(i))),
            pl.BlockSpec((1, B_WIDTH), lambda i: (0, 0)),
            pl.BlockSpec((ts, D_MODEL), lambda i: (prev(i), 0)),
            in_hbm,
            const((1, D_MODEL)),
            in_hbm, in_hbm, in_hbm,
            const((1, D_MODEL)),
        ],
        out_specs=pl.BlockSpec((ts, D_MODEL), lambda i: (prev(i), 0)),
        out_shape=jax.ShapeDtypeStruct((t, D_MODEL), F32),
        scratch_shapes=[
            pltpu.VMEM((A_HEADS, A_DV, A_DK), F32),
            pltpu.VMEM((n_chunk, A_HEADS, A_DV, A_DK), F32),
            pltpu.VMEM((n_chunk, A_HEADS, A_DV, A_DK), BF16),
            pltpu.VMEM((2, B_HEADS, 2 * BLOCK, BLOCK), F32),
            pltpu.VMEM((ts, MIX_WIDTH), BF16),
            pltpu.VMEM((ts, MIX_WIDTH), BF16),
            pltpu.VMEM((ts, D_MODEL), F32),
            pltpu.VMEM((ts, D_FF), BF16),
            pltpu.VMEM((MIX_WIDTH, D_MODEL), BF16),
            pltpu.VMEM((D_MODEL, D_FF), BF16),
            pltpu.VMEM((D_MODEL, D_FF), BF16),
            pltpu.VMEM((D_FF, D_MODEL), BF16),
            pltpu.SemaphoreType.DMA((4,)),
        ],
        compiler_params=pltpu.CompilerParams(
            dimension_semantics=("arbitrary",), vmem_limit_bytes=V7X_VMEM_LIMIT_BYTES),
        name="mixer_ffn",
    )(sinks, qd, kit, ks, hv, gate, dec, bq, kp, kp, vt, vt, attn_norm_g,
      x2d, w_out, ln2_g, w_gate, w_up, w_down, final_g)


def kernel(x, ln1_g, w_in, lb_logits, hgrn_norm_g, attn_sinks, attn_norm_g, w_out, ln2_g,
           w_gate, w_up, w_down, final_g):
    bsz, seq, d = x.shape
    x2d = x.reshape(bsz * seq, d)
    *prep, wo16, wg16, wu16, wd16 = _in_proj(x2d, ln1_g[0:1], w_in[0], lb_logits, hgrn_norm_g[0:1],
                                             w_out[0], w_gate[0], w_up[0], w_down[0])
    out = _mixer_ffn(prep, attn_sinks[0], attn_norm_g[0:1], x2d, wo16, ln2_g[0:1],
                     wg16, wu16, wd16, final_g.reshape(1, d), seq)
    return out.reshape(bsz, seq, d)
```

```python
import functools

import jax
import jax.numpy as jnp
from jax import lax
from jax.experimental import pallas as pl
from jax.experimental.pallas import tpu as pltpu

D_MODEL = 1024
A_HEADS = 4
A_DK = 128
A_DV = 128
A_QK = A_HEADS * A_DK
A_WIDTH = A_HEADS * A_DV
CHUNK = 64
B_HEADS = 8
B_KV_HEADS = 2
B_HEAD_DIM = 64
B_WIDTH = B_HEADS * B_HEAD_DIM
B_KV = B_KV_HEADS * B_HEAD_DIM
WINDOW = 128
BLOCK = 128
MIX_WIDTH = A_WIDTH + B_WIDTH
D_FF = 2816
IN_COLS = 4 * A_QK + B_WIDTH + 2 * B_KV
EPS = 1e-6
NEG_BIG = -1e30
LANES = 128
KEY_VARIANTS = 2 * B_KV_HEADS
N_FFN_WEIGHTS = 4

F32 = jnp.float32
BF16 = jnp.bfloat16

V7X_VMEM_LIMIT_BYTES = 56 * 1024 * 1024

TM_PROJ = 512
TS_MIX = 512
FF_CHUNK = 256
HGRN_GROUP_CHUNKS = 2
PAIR = 2 * CHUNK
CAST_ROWS = 128
LOG2E = 1.4426950408889634

NT_DIMS = (((1,), (1,)), ((), ()))
TN_DIMS = (((0,), (0,)), ((), ()))


def _rms_scale(v):
    return lax.rsqrt(jnp.mean(v * v, axis=-1, keepdims=True) + EPS)


def _silu(v):
    return v * (1.0 / (1.0 + jnp.exp(-v)))


def _sigmoid(v):
    return 1.0 / (1.0 + jnp.exp(-v))


def _in_proj_kernel(x_ref, g_ref, w32_ref, lbl_ref, hng_ref, wo32_ref, wg32_ref, wu32_ref, wd32_ref,
                    qd_ref, kit_ref, ks_ref, hv_ref, gate_ref, dec_ref, bq_ref, kp_ref, vt_ref,
                    wo16_ref, wg16_ref, wu16_ref, wd16_ref, w_ref, *, n_down_slabs):
    tm = x_ref.shape[0]

    @pl.when(pl.program_id(0) == 0)
    def _():
        for r0 in range(0, D_MODEL, CAST_ROWS):
            w_ref[r0:r0 + CAST_ROWS, :] = w32_ref[r0:r0 + CAST_ROWS, :].astype(BF16)

    @pl.when(pl.program_id(0) < n_down_slabs)
    def _():
        wd16_ref[...] = wd32_ref[...].astype(BF16)

    x = x_ref[...]
    u = (x * _rms_scale(x) * g_ref[...]).astype(BF16)

    lbl = lbl_ref[...]
    lmax = jnp.max(lbl, axis=0, keepdims=True)
    le = jnp.exp(lbl - lmax)
    lb = le[0:1, :] / jnp.sum(le, axis=0, keepdims=True)

    r = lax.broadcasted_iota(jnp.int32, (CHUNK, CHUNK), 0)
    c = lax.broadcasted_iota(jnp.int32, (CHUNK, CHUNK), 1)
    tri = jnp.where(r >= c, 1.0, 0.0).astype(BF16)

    pf = jnp.dot(u, w_ref[:, A_QK:2 * A_QK], preferred_element_type=F32)
    pq = jnp.dot(u, w_ref[:, 0:A_QK], preferred_element_type=F32)
    k_pair = []
    for ci in range(tm // CHUNK):
        rows = slice(ci * CHUNK, (ci + 1) * CHUNK)
        forget = lb + (1.0 - lb) * _sigmoid(pf[rows])
        key = 1.0 - forget
        logf = jnp.log(forget)
        p0 = logf.astype(BF16)
        r1 = logf - p0.astype(F32)
        p1 = r1.astype(BF16)
        p2 = (r1 - p1.astype(F32)).astype(BF16)
        bc3 = jnp.dot(tri, jnp.concatenate([p0, p1, p2], axis=-1), preferred_element_type=F32)
        bcum = bc3[:, :A_QK] + bc3[:, A_QK:2 * A_QK] + bc3[:, 2 * A_QK:]
        decay = jnp.exp(bcum[CHUNK - 1:CHUNK, :])
        k_intra = key * jnp.exp(-bcum)
        qd_ref[rows, :] = (_silu(pq[rows]) * jnp.exp(bcum)).astype(BF16)
        ks_ref[rows, :] = (k_intra * decay).astype(BF16)
        dec_ref[ci:ci + 1, :] = decay
        k_pair.append(k_intra)
        if len(k_pair) == PAIR // CHUNK:
            p0_row = (ci + 1) * CHUNK - PAIR
            kit_ref[:, p0_row:p0_row + PAIR] = jnp.concatenate(k_pair, axis=0).T.astype(BF16)
            k_pair = []

    wo16_ref[...] = wo32_ref[...].astype(BF16)
    wg16_ref[...] = wg32_ref[...].astype(BF16)
    wu16_ref[...] = wu32_ref[...].astype(BF16)

    c0 = 2 * A_QK + 2 * A_WIDTH
    pa = jnp.dot(u, w_ref[:, c0:], preferred_element_type=F32)
    bq_ref[...] = (pa[:, :B_WIDTH] * (B_HEAD_DIM ** -0.5 * LOG2E)).astype(BF16)
    lo = lax.broadcasted_iota(jnp.int32, (tm, LANES), 1) < B_HEAD_DIM
    k_all = pa[:, B_WIDTH:B_WIDTH + B_KV]
    k_rot = pltpu.roll(k_all, B_HEAD_DIM, axis=1)
    kp_ref[0] = jnp.where(lo, k_all, 0.0).astype(BF16)
    kp_ref[1] = jnp.where(lo, 0.0, k_rot).astype(BF16)
    kp_ref[2] = jnp.where(lo, k_rot, 0.0).astype(BF16)
    kp_ref[3] = jnp.where(lo, 0.0, k_all).astype(BF16)
    vt_ref[...] = pa[:, B_WIDTH + B_KV:].T.astype(BF16)

    pg = jnp.dot(u, w_ref[:, 2 * A_QK + A_WIDTH:2 * A_QK + 2 * A_WIDTH], preferred_element_type=F32)
    gate_ref[...] = (_silu(pg) * hng_ref[...]).astype(BF16)
    pi = jnp.dot(u, w_ref[:, 2 * A_QK:2 * A_QK + A_WIDTH], preferred_element_type=F32)
    hv_ref[...] = pi.astype(BF16)


def _in_proj(x2d, ln1_g, w_in, lb_logits, hgrn_norm_g, w_out, w_gate, w_up, w_down):
    t = x2d.shape[0]
    tm = TM_PROJ
    n_steps = t // tm
    row_blk = lambda width: pl.BlockSpec((tm, width), lambda i: (i, 0))
    const = lambda shape, **kw: pl.BlockSpec(shape, lambda i: (0, 0), **kw)
    bf = lambda shape: jax.ShapeDtypeStruct(shape, BF16)

    def slab(w, rows):
        n_slabs = w.shape[0] // rows
        assert w.shape[0] % rows == 0 and n_slabs <= n_steps, (w.shape, rows)
        return pl.BlockSpec((rows, w.shape[1]), lambda i: (jnp.minimum(i, n_slabs - 1), 0))

    cast_weights = (w_out, w_gate, w_up, w_down)
    even_rows = D_MODEL // n_steps
    slab_rows = (even_rows, even_rows, even_rows, CAST_ROWS)
    slab_specs = lambda: [slab(w, rows) for w, rows in zip(cast_weights, slab_rows)]
    return pl.pallas_call(
        functools.partial(_in_proj_kernel, n_down_slabs=D_FF // CAST_ROWS),
        grid=(n_steps,),
        in_specs=[
            row_blk(D_MODEL),
            const((1, D_MODEL)),
            const((D_MODEL, IN_COLS), pipeline_mode=pl.Buffered(1)),
            const((2, A_QK)),
            const((1, A_WIDTH)),
        ] + slab_specs(),
        out_specs=[
            row_blk(A_QK), pl.BlockSpec((A_QK, tm), lambda i: (0, i)), row_blk(A_QK),
            row_blk(A_WIDTH), row_blk(A_WIDTH),
            pl.BlockSpec((tm // CHUNK, A_QK), lambda i: (i, 0)),
            row_blk(B_WIDTH),
            pl.BlockSpec((KEY_VARIANTS, tm, LANES), lambda i: (0, i, 0)),
            pl.BlockSpec((LANES, tm), lambda i: (0, i)),
        ] + slab_specs(),
        out_shape=[
            bf((t, A_QK)), bf((A_QK, t)), bf((t, A_QK)), bf((t, A_WIDTH)), bf((t, A_WIDTH)),
            jax.ShapeDtypeStruct((t // CHUNK, A_QK), F32),
            bf((t, B_WIDTH)),
            bf((KEY_VARIANTS, t, LANES)),
            bf((LANES, t)),
        ] + [bf(w.shape) for w in cast_weights],
        scratch_shapes=[pltpu.VMEM((D_MODEL, IN_COLS), BF16)],
        compiler_params=pltpu.CompilerParams(
            dimension_semantics=("arbitrary",), vmem_limit_bytes=V7X_VMEM_LIMIT_BYTES),
        name="in_proj",
    )(x2d, ln1_g, w_in, lb_logits, hgrn_norm_g, *cast_weights)


def _bias_table_init(bias_ref):
    kj = lax.broadcasted_iota(jnp.int32, (2 * BLOCK, BLOCK), 0)
    qi = lax.broadcasted_iota(jnp.int32, (2 * BLOCK, BLOCK), 1)
    dist = qi + BLOCK - kj
    valid = (dist >= 0) & (dist < WINDOW)
    valid_first = valid & (kj >= BLOCK)
    distf = dist.astype(F32)
    for h in range(B_HEADS):
        slope = 2.0 ** (-(8.0 / B_HEADS) * (h + 1))
        sc = (-slope * LOG2E) * distf
        bias_ref[0, h] = jnp.where(valid, sc, NEG_BIG)
        bias_ref[1, h] = jnp.where(valid_first, sc, NEG_BIG)


def _mixer_stages(first, sinks_ref, qd_ref, kit_ref, ks_ref, hv_ref, gate_ref, dec_ref, bq_ref,
                  kp_ref, kph_ref, vt_ref, vth_ref, ang_ref, y_ref,
                  state_ref, upd_ref, sprev_ref, bias_ref):
    ts = qd_ref.shape[0]
    n_chunk = ts // CHUNK
    n_blk = ts // BLOCK
    head_cols = [slice(h * A_DK, (h + 1) * A_DK) for h in range(A_HEADS)]

    def chunk_units(c_lo, c_hi):
        return [(ci, slice(ci * CHUNK, (ci + 1) * CHUNK), head_cols[h], h)
                for ci in range(c_lo, c_hi) for h in range(A_HEADS)]

    def hgrn_updates(c_lo, c_hi):
        units = chunk_units(c_lo, c_hi)
        upds = [lax.dot_general(hv_ref[rows, cs], ks_ref[rows, cs], TN_DIMS,
                                preferred_element_type=F32) for (_, rows, cs, _) in units]
        for upd, (ci, _, _, h) in zip(upds, units):
            upd_ref[ci, h] = upd

    def hgrn_scan(h):
        st = state_ref[h]
        for ci in range(n_chunk):
            sprev_ref[ci, h] = st.T.astype(BF16)
            st = st * dec_ref[ci:ci + 1, head_cols[h]] + upd_ref[ci, h]
        state_ref[h] = st

    def hgrn_scores(p, box):
        rows = slice(p * PAIR, (p + 1) * PAIR)
        box["a"] = [jnp.dot(qd_ref[rows, cs], kit_ref[cs, rows], preferred_element_type=F32)
                    for cs in head_cols]
        box["o_inter"] = [
            jnp.concatenate(
                [jnp.dot(qd_ref[ci * CHUNK:(ci + 1) * CHUNK, cs], sprev_ref[ci, h],
                         preferred_element_type=F32)
                 for ci in range(p * PAIR // CHUNK, (p + 1) * PAIR // CHUNK)], axis=0)
            for h, cs in enumerate(head_cols)]

    def hgrn_values(p, box):
        rows = slice(p * PAIR, (p + 1) * PAIR)
        r = lax.broadcasted_iota(jnp.int32, (PAIR, PAIR), 0)
        c = lax.broadcasted_iota(jnp.int32, (PAIR, PAIR), 1)
        same_chunk_causal = (c <= r) & (c >= (r // CHUNK) * CHUNK)
        a_s = [jnp.where(same_chunk_causal, a, 0.0).astype(BF16) for a in box["a"]]
        box["o"] = [jnp.dot(a, hv_ref[rows, cs], preferred_element_type=F32) + oi
                    for a, oi, cs in zip(a_s, box["o_inter"], head_cols)]

    def hgrn_store(p, box):
        rows = slice(p * PAIR, (p + 1) * PAIR)
        for o, cs in zip(box["o"], head_cols):
            gate = gate_ref[rows, cs].astype(F32)
            y_ref[rows, cs] = (o * _rms_scale(o) * gate).astype(y_ref.dtype)

    swa_units = [(kvh, par) for kvh in range(B_KV_HEADS) for par in range(2)]

    def swa_scores(j, box):
        s_ts = []
        for kvh, par in swa_units:
            rows = slice(j * BLOCK, (j + 1) * BLOCK)
            q_pairs = jnp.concatenate(
                [bq_ref[rows, (2 * kvh) * LANES:(2 * kvh + 1) * LANES],
                 bq_ref[rows, (2 * kvh + 1) * LANES:(2 * kvh + 2) * LANES]], axis=0)
            idx = 2 * kvh + par
            if j == 0:
                kk = jnp.concatenate([kph_ref[idx], kp_ref[idx, 0:BLOCK, :]], axis=0)
            else:
                kk = kp_ref[idx, (j - 1) * BLOCK:(j + 1) * BLOCK, :]
            s_ts.append(lax.dot_general(kk, q_pairs, NT_DIMS, preferred_element_type=F32))
        box["s"] = s_ts

    def swa_softmax(j, box):
        p_ts, inv_ls = [], []
        for (kvh, par), s_t in zip(swa_units, box["s"]):
            p_halves = []
            for half, h in enumerate((4 * kvh + par, 4 * kvh + 2 + par)):
                bias = bias_ref[first, h] if j == 0 else bias_ref[0, h]
                s = s_t[:, half * BLOCK:(half + 1) * BLOCK] + bias
                sink = sinks_ref[h] * LOG2E
                m = jnp.maximum(jnp.max(s, axis=0, keepdims=True), sink)
                p = jnp.exp2(s - m)
                inv_ls.append(1.0 / (jnp.sum(p, axis=0, keepdims=True) + jnp.exp2(sink - m)))
                p_halves.append(p.astype(BF16))
            p_ts.append(jnp.concatenate(p_halves, axis=1))
        box["p"], box["inv_l"] = p_ts, inv_ls

    def swa_values(j, box):
        o_t = [None] * B_HEADS
        for ui, ((kvh, par), p_t) in enumerate(zip(swa_units, box["p"])):
            drow = slice(kvh * B_HEAD_DIM, (kvh + 1) * B_HEAD_DIM)
            if j == 0:
                vt_h = jnp.concatenate([vth_ref[drow, :], vt_ref[drow, 0:BLOCK]], axis=1)
            else:
                vt_h = vt_ref[drow, (j - 1) * BLOCK:(j + 1) * BLOCK]
            pv = jnp.dot(vt_h, p_t, preferred_element_type=F32)
            for half, h in enumerate((4 * kvh + par, 4 * kvh + 2 + par)):
                o_t[h] = pv[:, half * BLOCK:(half + 1) * BLOCK] * box["inv_l"][2 * ui + half]
        box["o_t"] = o_t

    def swa_store(j, box):
        o_all_t = jnp.concatenate(box["o_t"], axis=0)
        scale = lax.rsqrt(jnp.sum(o_all_t * o_all_t, axis=0, keepdims=True) * (1.0 / B_WIDTH) + EPS)
        o_all = (o_all_t * scale).T
        y_ref[j * BLOCK:(j + 1) * BLOCK, A_WIDTH:] = (o_all * ang_ref[...]).astype(y_ref.dtype)

    group = HGRN_GROUP_CHUNKS
    hgrn = [functools.partial(hgrn_updates, c, c + group) for c in range(0, n_chunk, group)]
    hgrn += [functools.partial(hgrn_scan, h) for h in range(A_HEADS)]
    for p in range(ts // PAIR):
        box = {}
        hgrn += [functools.partial(f, p, box) for f in (hgrn_scores, hgrn_values, hgrn_store)]
    swa = []
    for j in range(n_blk):
        box = {}
        swa += [functools.partial(f, j, box) for f in (swa_scores, swa_softmax, swa_values, swa_store)]
    stages = []
    for k in range(max(len(hgrn), len(swa))):
        stages += hgrn[k:k + 1] + swa[k:k + 1]
    return stages


def _ffn_stages(x_ref, y_ref, wo_ref, g2_ref, wg_ref, wu_ref, wd_ref, h_ref, act_ref, h2_ref,
                before_down=()):
    box = {}

    def head():
        h = x_ref[...] + jnp.dot(y_ref[...], wo_ref[...], preferred_element_type=F32)
        box["u"] = (h * _rms_scale(h) * g2_ref[...]).astype(BF16)
        h_ref[...] = h

    def ff_up(c0):
        cs = slice(c0, c0 + FF_CHUNK)
        box["gate"] = jnp.dot(box["u"], wg_ref[:, cs], preferred_element_type=F32)
        box["up"] = jnp.dot(box["u"], wu_ref[:, cs], preferred_element_type=F32)

    def ff_act(c0):
        act_ref[:, c0:c0 + FF_CHUNK] = (_silu(box["gate"]) * box["up"]).astype(BF16)

    def down():
        h2_ref[...] = h_ref[...] + jnp.dot(act_ref[...], wd_ref[...], preferred_element_type=F32)

    stages = [head]
    for c0 in range(0, D_FF, FF_CHUNK):
        stages += [functools.partial(ff_up, c0), functools.partial(ff_act, c0)]
    return stages + list(before_down) + [down]


def _final_norm(h2_ref, gf_ref, o_ref):
    h2 = h2_ref[...]
    o_ref[...] = h2 * _rms_scale(h2) * gf_ref[...]


def _run_interleaved(major, minor):
    span = max(len(major) - 2, 1)
    done = 0
    for k, stage in enumerate(major):
        stage()
        want = min(len(minor), -(-(k + 1) * len(minor) // span))
        while done < want:
            minor[done]()
            done += 1


def _mixer_ffn_kernel(sinks_ref, qd_ref, kit_ref, ks_ref, hv_ref, gate_ref, dec_ref, bq_ref,
                      kp_ref, kph_ref, vt_ref, vth_ref, ang_ref,
                      x_ref, wo_hbm, g2_ref, wg_hbm, wu_hbm, wd_hbm, gf_ref, o_ref,
                      state_ref, upd_ref, sprev_ref, bias_ref, y_cur, y_prev, h_ref, act_ref, h2_ref,
                      wo_ref, wg_ref, wu_ref, wd_ref, w_sems, *, n_tiles, n_s):
    i = pl.program_id(0)
    seq_start = lax.rem(i, n_s) == 0
    weight_copies = [
        pltpu.make_async_copy(src, dst, w_sems.at[k])
        for k, (src, dst) in enumerate(((wo_hbm, wo_ref), (wg_hbm, wg_ref),
                                        (wu_hbm, wu_ref), (wd_hbm, wd_ref)))]

    @pl.when(i == 0)
    def _():
        for copy in weight_copies:
            copy.start()
        _bias_table_init(bias_ref)

    @pl.when(i == 1)
    def _():
        for copy in weight_copies:
            copy.wait()

    @pl.when(seq_start)
    def _():
        state_ref[...] = jnp.zeros_like(state_ref)

    first = jnp.where(seq_start, 1, 0)
    mixer_args = (first, sinks_ref, qd_ref, kit_ref, ks_ref, hv_ref, gate_ref, dec_ref, bq_ref,
                  kp_ref, kph_ref, vt_ref, vth_ref, ang_ref, y_cur,
                  state_ref, upd_ref, sprev_ref, bias_ref)
    ffn_args = (x_ref, y_prev, wo_ref, g2_ref, wg_ref, wu_ref, wd_ref, h_ref, act_ref, h2_ref)
    final_norm = functools.partial(_final_norm, h2_ref, gf_ref, o_ref)

    @pl.when(i == 0)
    def _():
        _run_interleaved(_mixer_stages(*mixer_args), [])
        y_prev[...] = y_cur[...]

    @pl.when(i == 1)
    def _():
        _run_interleaved(_ffn_stages(*ffn_args), _mixer_stages(*mixer_args))
        y_prev[...] = y_cur[...]

    @pl.when((i > 1) & (i < n_tiles))
    def _():
        _run_interleaved(_ffn_stages(*ffn_args, before_down=[final_norm]),
                         _mixer_stages(*mixer_args))
        y_prev[...] = y_cur[...]

    @pl.when(i == n_tiles)
    def _():
        _run_interleaved(_ffn_stages(*ffn_args, before_down=[final_norm]), [])

    @pl.when(i == n_tiles + 1)
    def _():
        final_norm()


def _mixer_ffn(prep, sinks, attn_norm_g, x2d, w_out, ln2_g, w_gate, w_up, w_down, final_g, seq):
    qd, kit, ks, hv, gate, dec, bq, kp, vt = prep
    t = x2d.shape[0]
    ts = TS_MIX
    n_tiles = t // ts
    n_s = seq // ts
    n_chunk = ts // CHUNK
    n_blk = ts // BLOCK
    cur = lambda i: jnp.minimum(i, n_tiles - 1)
    prev = lambda i: jnp.clip(i - 1, 0, n_tiles - 1)
    last = lambda i: jnp.clip(i - 2, 0, n_tiles - 1)
    halo = lambda i: jnp.maximum(cur(i) * n_blk - 1, 0)
    row_blk = lambda width: pl.BlockSpec((ts, width), lambda i: (cur(i), 0))
    const = lambda shape: pl.BlockSpec(shape, lambda i: (0, 0), pipeline_mode=pl.Buffered(1))
    in_hbm = pl.BlockSpec(memory_space=pl.ANY)
    assert n_tiles >= 2
    return pl.pallas_call(
        functools.partial(_mixer_ffn_kernel, n_tiles=n_tiles, n_s=n_s),
        grid=(n_tiles + 2,),
        in_specs=[
            pl.BlockSpec(memory_space=pltpu.SMEM),
            row_blk(A_QK), pl.BlockSpec((A_QK, ts), lambda i: (0, cur(i))), row_blk(A_QK),
            row_blk(A_WIDTH), row_blk(A_WIDTH),
            pl.BlockSpec((n_chunk, A_QK), lambda i: (cur(i), 0)),
            row_blk(B_WIDTH),
            pl.BlockSpec((KEY_VARIANTS, ts, LANES), lambda i: (0, cur(i), 0)),
            pl.BlockSpec((KEY_VARIANTS, BLOCK, LANES), lambda i: (0, halo(i), 0)),
            pl.BlockSpec((LANES, ts), lambda i: (0, cur(i))),
            pl.BlockSpec((LANES, BLOCK), lambda i: (0, halo(i))),
            pl.BlockSpec((1, B_WIDTH), lambda i: (0, 0)),
            pl.BlockSpec((ts, D_MODEL), lambda i: (prev(i), 0)),
            in_hbm,
            const((1, D_MODEL)),
            in_hbm, in_hbm, in_hbm,
            const((1, D_MODEL)),
        ],
        out_specs=pl.BlockSpec((ts, D_MODEL), lambda i: (last(i), 0)),
        out_shape=jax.ShapeDtypeStruct((t, D_MODEL), F32),
        scratch_shapes=[
            pltpu.VMEM((A_HEADS, A_DV, A_DK), F32),
            pltpu.VMEM((n_chunk, A_HEADS, A_DV, A_DK), F32),
            pltpu.VMEM((n_chunk, A_HEADS, A_DV, A_DK), BF16),
            pltpu.VMEM((2, B_HEADS, 2 * BLOCK, BLOCK), F32),
            pltpu.VMEM((ts, MIX_WIDTH), BF16),
            pltpu.VMEM((ts, MIX_WIDTH), BF16),
            pltpu.VMEM((ts, D_MODEL), F32),
            pltpu.VMEM((ts, D_FF), BF16),
            pltpu.VMEM((ts, D_MODEL), F32),
            pltpu.VMEM((MIX_WIDTH, D_MODEL), BF16),
            pltpu.VMEM((D_MODEL, D_FF), BF16),
            pltpu.VMEM((D_MODEL, D_FF), BF16),
            pltpu.VMEM((D_FF, D_MODEL), BF16),
            pltpu.SemaphoreType.DMA((N_FFN_WEIGHTS,)),
        ],
        compiler_params=pltpu.CompilerParams(
            dimension_semantics=("arbitrary",), vmem_limit_bytes=V7X_VMEM_LIMIT_BYTES),
        name="mixer_ffn",
    )(sinks, qd, kit, ks, hv, gate, dec, bq, kp, kp, vt, vt, attn_norm_g,
      x2d, w_out, ln2_g, w_gate, w_up, w_down, final_g)


def kernel(x, ln1_g, w_in, lb_logits, hgrn_norm_g, attn_sinks, attn_norm_g, w_out, ln2_g,
           w_gate, w_up, w_down, final_g):
    bsz, seq, d = x.shape
    x2d = x.reshape(bsz * seq, d)
    *prep, wo16, wg16, wu16, wd16 = _in_proj(x2d, ln1_g[0:1], w_in[0], lb_logits, hgrn_norm_g[0:1],
                                             w_out[0], w_gate[0], w_up[0], w_down[0])
    out = _mixer_ffn(prep, attn_sinks[0], attn_norm_g[0:1], x2d, wo16, ln2_g[0:1],
                     wg16, wu16, wd16, final_g.reshape(1, d), seq)
    return out.reshape(bsz, seq, d)
```

```python
import functools

import jax
import jax.numpy as jnp
from jax import lax
from jax.experimental import pallas as pl
from jax.experimental.pallas import tpu as pltpu

D_MODEL = 1024
A_HEADS = 4
A_DK = 128
A_DV = 128
A_QK = A_HEADS * A_DK
A_WIDTH = A_HEADS * A_DV
CHUNK = 64
B_HEADS = 8
B_KV_HEADS = 2
B_HEAD_DIM = 64
B_WIDTH = B_HEADS * B_HEAD_DIM
B_KV = B_KV_HEADS * B_HEAD_DIM
WINDOW = 128
BLOCK = 128
MIX_WIDTH = A_WIDTH + B_WIDTH
D_FF = 2816
IN_COLS = 4 * A_QK + B_WIDTH + 2 * B_KV
EPS = 1e-6
NEG_BIG = -1e30
LANES = 128
KEY_VARIANTS = 2 * B_KV_HEADS
N_FFN_WEIGHTS = 4

F32 = jnp.float32
BF16 = jnp.bfloat16

V7X_VMEM_LIMIT_BYTES = 56 * 1024 * 1024

TM_PROJ = 1024
TS_MIX = 512
FF_CHUNK = 256
HGRN_GROUP_CHUNKS = 2
PAIR = 2 * CHUNK
CAST_ROWS = 256
LOG2E = 1.4426950408889634

NT_DIMS = (((1,), (1,)), ((), ()))
TN_DIMS = (((0,), (0,)), ((), ()))


def _rms_scale(v):
    return lax.rsqrt(jnp.mean(v * v, axis=-1, keepdims=True) + EPS)


def _silu(v):
    return v * (1.0 / (1.0 + jnp.exp(-v)))


def _sigmoid(v):
    return 1.0 / (1.0 + jnp.exp(-v))


def _in_proj_kernel(x_ref, g_ref, w32_ref, lbl_ref, hng_ref, wo32_ref, wg32_ref, wu32_ref, wd32_ref,
                    qd_ref, kit_ref, ks_ref, hv_ref, gate_ref, dec_ref, bq_ref, kp_ref, vt_ref,
                    wo16_ref, wg16_ref, wu16_ref, wd16_ref, w_ref, *, n_down_slabs):
    tm = x_ref.shape[0]

    @pl.when(pl.program_id(0) == 0)
    def _():
        for r0 in range(0, D_MODEL, CAST_ROWS):
            w_ref[r0:r0 + CAST_ROWS, :] = w32_ref[r0:r0 + CAST_ROWS, :].astype(BF16)

    @pl.when(pl.program_id(0) < n_down_slabs)
    def _():
        wd16_ref[...] = wd32_ref[...].astype(BF16)

    x = x_ref[...]
    u = (x * _rms_scale(x) * g_ref[...]).astype(BF16)

    lbl = lbl_ref[...]
    lmax = jnp.max(lbl, axis=0, keepdims=True)
    le = jnp.exp(lbl - lmax)
    lb = le[0:1, :] / jnp.sum(le, axis=0, keepdims=True)

    r = lax.broadcasted_iota(jnp.int32, (CHUNK, CHUNK), 0)
    c = lax.broadcasted_iota(jnp.int32, (CHUNK, CHUNK), 1)
    tri = jnp.where(r >= c, 1.0, 0.0).astype(BF16)

    pf = jnp.dot(u, w_ref[:, A_QK:2 * A_QK], preferred_element_type=F32)
    pq = jnp.dot(u, w_ref[:, 0:A_QK], preferred_element_type=F32)
    k_pair = []
    for ci in range(tm // CHUNK):
        rows = slice(ci * CHUNK, (ci + 1) * CHUNK)
        forget = lb + (1.0 - lb) * _sigmoid(pf[rows])
        key = 1.0 - forget
        logf = jnp.log(forget)
        p0 = logf.astype(BF16)
        r1 = logf - p0.astype(F32)
        p1 = r1.astype(BF16)
        p2 = (r1 - p1.astype(F32)).astype(BF16)
        bc3 = jnp.dot(tri, jnp.concatenate([p0, p1, p2], axis=-1), preferred_element_type=F32)
        bcum = bc3[:, :A_QK] + bc3[:, A_QK:2 * A_QK] + bc3[:, 2 * A_QK:]
        decay = jnp.exp(bcum[CHUNK - 1:CHUNK, :])
        k_intra = key * jnp.exp(-bcum)
        qd_ref[rows, :] = (_silu(pq[rows]) * jnp.exp(bcum)).astype(BF16)
        ks_ref[rows, :] = (k_intra * decay).astype(BF16)
        dec_ref[ci:ci + 1, :] = decay
        k_pair.append(k_intra)
        if len(k_pair) == PAIR // CHUNK:
            p0_row = (ci + 1) * CHUNK - PAIR
            kit_ref[:, p0_row:p0_row + PAIR] = jnp.concatenate(k_pair, axis=0).T.astype(BF16)
            k_pair = []

    wo16_ref[...] = wo32_ref[...].astype(BF16)
    wg16_ref[...] = wg32_ref[...].astype(BF16)
    wu16_ref[...] = wu32_ref[...].astype(BF16)

    c0 = 2 * A_QK + 2 * A_WIDTH
    pa = jnp.dot(u, w_ref[:, c0:], preferred_element_type=F32)
    bq_ref[...] = (pa[:, :B_WIDTH] * (B_HEAD_DIM ** -0.5 * LOG2E)).astype(BF16)
    lo = lax.broadcasted_iota(jnp.int32, (tm, LANES), 1) < B_HEAD_DIM
    k_all = pa[:, B_WIDTH:B_WIDTH + B_KV]
    k_rot = pltpu.roll(k_all, B_HEAD_DIM, axis=1)
    kp_ref[0] = jnp.where(lo, k_all, 0.0).astype(BF16)
    kp_ref[1] = jnp.where(lo, 0.0, k_rot).astype(BF16)
    kp_ref[2] = jnp.where(lo, k_rot, 0.0).astype(BF16)
    kp_ref[3] = jnp.where(lo, 0.0, k_all).astype(BF16)
    vt_ref[...] = pa[:, B_WIDTH + B_KV:].T.astype(BF16)

    pg = jnp.dot(u, w_ref[:, 2 * A_QK + A_WIDTH:2 * A_QK + 2 * A_WIDTH], preferred_element_type=F32)
    gate_ref[...] = (_silu(pg) * hng_ref[...]).astype(BF16)
    pi = jnp.dot(u, w_ref[:, 2 * A_QK:2 * A_QK + A_WIDTH], preferred_element_type=F32)
    hv_ref[...] = pi.astype(BF16)


def _in_proj(x2d, ln1_g, w_in, lb_logits, hgrn_norm_g, w_out, w_gate, w_up, w_down):
    t = x2d.shape[0]
    tm = TM_PROJ
    n_steps = t // tm
    row_blk = lambda width: pl.BlockSpec((tm, width), lambda i: (i, 0))
    const = lambda shape, **kw: pl.BlockSpec(shape, lambda i: (0, 0), **kw)
    bf = lambda shape: jax.ShapeDtypeStruct(shape, BF16)

    def slab(w, rows):
        n_slabs = w.shape[0] // rows
        assert w.shape[0] % rows == 0 and n_slabs <= n_steps, (w.shape, rows)
        return pl.BlockSpec((rows, w.shape[1]), lambda i: (jnp.minimum(i, n_slabs - 1), 0))

    cast_weights = (w_out, w_gate, w_up, w_down)
    even_rows = D_MODEL // n_steps
    slab_rows = (even_rows, even_rows, even_rows, CAST_ROWS)
    slab_specs = lambda: [slab(w, rows) for w, rows in zip(cast_weights, slab_rows)]
    return pl.pallas_call(
        functools.partial(_in_proj_kernel, n_down_slabs=D_FF // CAST_ROWS),
        grid=(n_steps,),
        in_specs=[
            row_blk(D_MODEL),
            const((1, D_MODEL)),
            const((D_MODEL, IN_COLS), pipeline_mode=pl.Buffered(1)),
            const((2, A_QK)),
            const((1, A_WIDTH)),
        ] + slab_specs(),
        out_specs=[
            row_blk(A_QK), pl.BlockSpec((A_QK, tm), lambda i: (0, i)), row_blk(A_QK),
            row_blk(A_WIDTH), row_blk(A_WIDTH),
            pl.BlockSpec((tm // CHUNK, A_QK), lambda i: (i, 0)),
            row_blk(B_WIDTH),
            pl.BlockSpec((KEY_VARIANTS, tm, LANES), lambda i: (0, i, 0)),
            pl.BlockSpec((LANES, tm), lambda i: (0, i)),
        ] + slab_specs(),
        out_shape=[
            bf((t, A_QK)), bf((A_QK, t)), bf((t, A_QK)), bf((t, A_WIDTH)), bf((t, A_WIDTH)),
            jax.ShapeDtypeStruct((t // CHUNK, A_QK), F32),
            bf((t, B_WIDTH)),
            bf((KEY_VARIANTS, t, LANES)),
            bf((LANES, t)),
        ] + [bf(w.shape) for w in cast_weights],
        scratch_shapes=[pltpu.VMEM((D_MODEL, IN_COLS), BF16)],
        compiler_params=pltpu.CompilerParams(
            dimension_semantics=("arbitrary",), vmem_limit_bytes=V7X_VMEM_LIMIT_BYTES),
        name="in_proj",
    )(x2d, ln1_g, w_in, lb_logits, hgrn_norm_g, *cast_weights)


def _bias_table_init(bias_ref):
    kj = lax.broadcasted_iota(jnp.int32, (2 * BLOCK, BLOCK), 0)
    qi = lax.broadcasted_iota(jnp.int32, (2 * BLOCK, BLOCK), 1)
    dist = qi + BLOCK - kj
    valid = (dist >= 0) & (dist < WINDOW)
    valid_first = valid & (kj >= BLOCK)
    distf = dist.astype(F32)
    for h in range(B_HEADS):
        slope = 2.0 ** (-(8.0 / B_HEADS) * (h + 1))
        sc = (-slope * LOG2E) * distf
        bias_ref[0, h] = jnp.where(valid, sc, NEG_BIG)
        bias_ref[1, h] = jnp.where(valid_first, sc, NEG_BIG)


def _mixer_stages(first, sinks_ref, qd_ref, kit_ref, ks_ref, hv_ref, gate_ref, dec_ref, bq_ref,
                  kp_ref, kph_ref, vt_ref, vth_ref, ang_ref, y_ref,
                  state_ref, upd_ref, sprev_ref, bias_ref):
    ts = qd_ref.shape[0]
    n_chunk = ts // CHUNK
    n_blk = ts // BLOCK
    head_cols = [slice(h * A_DK, (h + 1) * A_DK) for h in range(A_HEADS)]

    def chunk_units(c_lo, c_hi):
        return [(ci, slice(ci * CHUNK, (ci + 1) * CHUNK), head_cols[h], h)
                for ci in range(c_lo, c_hi) for h in range(A_HEADS)]

    def hgrn_updates(c_lo, c_hi):
        units = chunk_units(c_lo, c_hi)
        upds = [lax.dot_general(hv_ref[rows, cs], ks_ref[rows, cs], TN_DIMS,
                                preferred_element_type=F32) for (_, rows, cs, _) in units]
        for upd, (ci, _, _, h) in zip(upds, units):
            upd_ref[ci, h] = upd

    def hgrn_scan(h):
        st = state_ref[h]
        for ci in range(n_chunk):
            sprev_ref[ci, h] = st.T.astype(BF16)
            st = st * dec_ref[ci:ci + 1, head_cols[h]] + upd_ref[ci, h]
        state_ref[h] = st

    def hgrn_scores(p, box):
        rows = slice(p * PAIR, (p + 1) * PAIR)
        box["a"] = [jnp.dot(qd_ref[rows, cs], kit_ref[cs, rows], preferred_element_type=F32)
                    for cs in head_cols]
        box["o_inter"] = [
            jnp.concatenate(
                [jnp.dot(qd_ref[ci * CHUNK:(ci + 1) * CHUNK, cs], sprev_ref[ci, h],
                         preferred_element_type=F32)
                 for ci in range(p * PAIR // CHUNK, (p + 1) * PAIR // CHUNK)], axis=0)
            for h, cs in enumerate(head_cols)]

    def hgrn_values(p, box):
        rows = slice(p * PAIR, (p + 1) * PAIR)
        r = lax.broadcasted_iota(jnp.int32, (PAIR, PAIR), 0)
        c = lax.broadcasted_iota(jnp.int32, (PAIR, PAIR), 1)
        same_chunk_causal = (c <= r) & (c >= (r // CHUNK) * CHUNK)
        a_s = [jnp.where(same_chunk_causal, a, 0.0).astype(BF16) for a in box["a"]]
        box["o"] = [jnp.dot(a, hv_ref[rows, cs], preferred_element_type=F32) + oi
                    for a, oi, cs in zip(a_s, box["o_inter"], head_cols)]

    def hgrn_store(p, box):
        rows = slice(p * PAIR, (p + 1) * PAIR)
        for o, cs in zip(box["o"], head_cols):
            gate = gate_ref[rows, cs].astype(F32)
            y_ref[rows, cs] = (o * _rms_scale(o) * gate).astype(y_ref.dtype)

    swa_units = [(kvh, par) for kvh in range(B_KV_HEADS) for par in range(2)]

    def swa_scores(j, box):
        s_ts = []
        for kvh, par in swa_units:
            rows = slice(j * BLOCK, (j + 1) * BLOCK)
            q_pairs = jnp.concatenate(
                [bq_ref[rows, (2 * kvh) * LANES:(2 * kvh + 1) * LANES],
                 bq_ref[rows, (2 * kvh + 1) * LANES:(2 * kvh + 2) * LANES]], axis=0)
            idx = 2 * kvh + par
            if j == 0:
                kk = jnp.concatenate([kph_ref[idx], kp_ref[idx, 0:BLOCK, :]], axis=0)
            else:
                kk = kp_ref[idx, (j - 1) * BLOCK:(j + 1) * BLOCK, :]
            s_ts.append(lax.dot_general(kk, q_pairs, NT_DIMS, preferred_element_type=F32))
        box["s"] = s_ts

    def swa_softmax(j, box):
        p_ts, inv_ls = [], []
        for (kvh, par), s_t in zip(swa_units, box["s"]):
            p_halves = []
            for half, h in enumerate((4 * kvh + par, 4 * kvh + 2 + par)):
                bias = bias_ref[first, h] if j == 0 else bias_ref[0, h]
                s = s_t[:, half * BLOCK:(half + 1) * BLOCK] + bias
                sink = sinks_ref[h] * LOG2E
                m = jnp.maximum(jnp.max(s, axis=0, keepdims=True), sink)
                p = jnp.exp2(s - m)
                inv_ls.append(1.0 / (jnp.sum(p, axis=0, keepdims=True) + jnp.exp2(sink - m)))
                p_halves.append(p.astype(BF16))
            p_ts.append(jnp.concatenate(p_halves, axis=1))
        box["p"], box["inv_l"] = p_ts, inv_ls

    def swa_values(j, box):
        o_t = [None] * B_HEADS
        for ui, ((kvh, par), p_t) in enumerate(zip(swa_units, box["p"])):
            drow = slice(kvh * B_HEAD_DIM, (kvh + 1) * B_HEAD_DIM)
            if j == 0:
                vt_h = jnp.concatenate([vth_ref[drow, :], vt_ref[drow, 0:BLOCK]], axis=1)
            else:
                vt_h = vt_ref[drow, (j - 1) * BLOCK:(j + 1) * BLOCK]
            pv = jnp.dot(vt_h, p_t, preferred_element_type=F32)
            for half, h in enumerate((4 * kvh + par, 4 * kvh + 2 + par)):
                o_t[h] = pv[:, half * BLOCK:(half + 1) * BLOCK] * box["inv_l"][2 * ui + half]
        box["o_t"] = o_t

    def swa_store(j, box):
        o_all_t = jnp.concatenate(box["o_t"], axis=0)
        scale = lax.rsqrt(jnp.sum(o_all_t * o_all_t, axis=0, keepdims=True) * (1.0 / B_WIDTH) + EPS)
        o_all = (o_all_t * scale).T
        y_ref[j * BLOCK:(j + 1) * BLOCK, A_WIDTH:] = (o_all * ang_ref[...]).astype(y_ref.dtype)

    group = HGRN_GROUP_CHUNKS
    hgrn = [functools.partial(hgrn_updates, c, c + group) for c in range(0, n_chunk, group)]
    hgrn += [functools.partial(hgrn_scan, h) for h in range(A_HEADS)]
    for p in range(ts // PAIR):
        box = {}
        hgrn += [functools.partial(f, p, box) for f in (hgrn_scores, hgrn_values, hgrn_store)]
    swa = []
    for j in range(n_blk):
        box = {}
        swa += [functools.partial(f, j, box) for f in (swa_scores, swa_softmax, swa_values, swa_store)]
    stages = []
    for k in range(max(len(hgrn), len(swa))):
        stages += hgrn[k:k + 1] + swa[k:k + 1]
    return stages


def _ffn_stages(x_ref, y_ref, wo_ref, g2_ref, wg_ref, wu_ref, wd_ref, h_ref, act_ref, h2_ref,
                before_down=()):
    box = {}

    def head():
        h = x_ref[...] + jnp.dot(y_ref[...], wo_ref[...], preferred_element_type=F32)
        box["u"] = (h * _rms_scale(h) * g2_ref[...]).astype(BF16)
        h_ref[...] = h

    def ff_up(c0):
        cs = slice(c0, c0 + FF_CHUNK)
        box["gate"] = jnp.dot(box["u"], wg_ref[:, cs], preferred_element_type=F32)
        box["up"] = jnp.dot(box["u"], wu_ref[:, cs], preferred_element_type=F32)

    def ff_act(c0):
        act_ref[:, c0:c0 + FF_CHUNK] = (_silu(box["gate"]) * box["up"]).astype(BF16)

    def down():
        h2_ref[...] = h_ref[...] + jnp.dot(act_ref[...], wd_ref[...], preferred_element_type=F32)

    stages = [head]
    for c0 in range(0, D_FF, FF_CHUNK):
        stages += [functools.partial(ff_up, c0), functools.partial(ff_act, c0)]
    return stages + list(before_down) + [down]


def _final_norm(h2_ref, gf_ref, o_ref):
    h2 = h2_ref[...]
    o_ref[...] = h2 * _rms_scale(h2) * gf_ref[...]


def _run_interleaved(major, minor):
    span = max(len(major) - 2, 1)
    done = 0
    for k, stage in enumerate(major):
        stage()
        want = min(len(minor), -(-(k + 1) * len(minor) // span))
        while done < want:
            minor[done]()
            done += 1


def _mixer_ffn_kernel(sinks_ref, qd_ref, kit_ref, ks_ref, hv_ref, gate_ref, dec_ref, bq_ref,
                      kp_ref, kph_ref, vt_ref, vth_ref, ang_ref,
                      x_ref, wo_hbm, g2_ref, wg_hbm, wu_hbm, wd_hbm, gf_ref, o_ref,
                      state_ref, upd_ref, sprev_ref, bias_ref, y_cur, y_prev, h_ref, act_ref, h2_ref,
                      wo_ref, wg_ref, wu_ref, wd_ref, w_sems, *, n_tiles, n_s):
    i = pl.program_id(0)
    seq_start = lax.rem(i, n_s) == 0
    weight_copies = [
        pltpu.make_async_copy(src, dst, w_sems.at[k])
        for k, (src, dst) in enumerate(((wo_hbm, wo_ref), (wg_hbm, wg_ref),
                                        (wu_hbm, wu_ref), (wd_hbm, wd_ref)))]

    @pl.when(i == 0)
    def _():
        for copy in weight_copies:
            copy.start()
        _bias_table_init(bias_ref)

    @pl.when(i == 1)
    def _():
        for copy in weight_copies:
            copy.wait()

    @pl.when(seq_start)
    def _():
        state_ref[...] = jnp.zeros_like(state_ref)

    first = jnp.where(seq_start, 1, 0)
    mixer_args = (first, sinks_ref, qd_ref, kit_ref, ks_ref, hv_ref, gate_ref, dec_ref, bq_ref,
                  kp_ref, kph_ref, vt_ref, vth_ref, ang_ref, y_cur,
                  state_ref, upd_ref, sprev_ref, bias_ref)
    ffn_args = (x_ref, y_prev, wo_ref, g2_ref, wg_ref, wu_ref, wd_ref, h_ref, act_ref, h2_ref)
    final_norm = functools.partial(_final_norm, h2_ref, gf_ref, o_ref)

    @pl.when(i == 0)
    def _():
        _run_interleaved(_mixer_stages(*mixer_args), [])
        y_prev[...] = y_cur[...]

    @pl.when(i == 1)
    def _():
        _run_interleaved(_ffn_stages(*ffn_args), _mixer_stages(*mixer_args))
        y_prev[...] = y_cur[...]

    @pl.when((i > 1) & (i < n_tiles))
    def _():
        _run_interleaved(_ffn_stages(*ffn_args, before_down=[final_norm]),
                         _mixer_stages(*mixer_args))
        y_prev[...] = y_cur[...]

    @pl.when(i == n_tiles)
    def _():
        _run_interleaved(_ffn_stages(*ffn_args, before_down=[final_norm]), [])

    @pl.when(i == n_tiles + 1)
    def _():
        final_norm()


def _mixer_ffn(prep, sinks, attn_norm_g, x2d, w_out, ln2_g, w_gate, w_up, w_down, final_g, seq):
    qd, kit, ks, hv, gate, dec, bq, kp, vt = prep
    t = x2d.shape[0]
    ts = TS_MIX
    n_tiles = t // ts
    n_s = seq // ts
    n_chunk = ts // CHUNK
    n_blk = ts // BLOCK
    cur = lambda i: jnp.minimum(i, n_tiles - 1)
    prev = lambda i: jnp.clip(i - 1, 0, n_tiles - 1)
    last = lambda i: jnp.clip(i - 2, 0, n_tiles - 1)
    halo = lambda i: jnp.maximum(cur(i) * n_blk - 1, 0)
    row_blk = lambda width: pl.BlockSpec((ts, width), lambda i: (cur(i), 0))
    const = lambda shape: pl.BlockSpec(shape, lambda i: (0, 0), pipeline_mode=pl.Buffered(1))
    in_hbm = pl.BlockSpec(memory_space=pl.ANY)
    assert n_tiles >= 2
    return pl.pallas_call(
        functools.partial(_mixer_ffn_kernel, n_tiles=n_tiles, n_s=n_s),
        grid=(n_tiles + 2,),
        in_specs=[
            pl.BlockSpec(memory_space=pltpu.SMEM),
            row_blk(A_QK), pl.BlockSpec((A_QK, ts), lambda i: (0, cur(i))), row_blk(A_QK),
            row_blk(A_WIDTH), row_blk(A_WIDTH),
            pl.BlockSpec((n_chunk, A_QK), lambda i: (cur(i), 0)),
            row_blk(B_WIDTH),
            pl.BlockSpec((KEY_VARIANTS, ts, LANES), lambda i: (0, cur(i), 0)),
            pl.BlockSpec((KEY_VARIANTS, BLOCK, LANES), lambda i: (0, halo(i), 0)),
            pl.BlockSpec((LANES, ts), lambda i: (0, cur(i))),
            pl.BlockSpec((LANES, BLOCK), lambda i: (0, halo(i))),
            pl.BlockSpec((1, B_WIDTH), lambda i: (0, 0)),
            pl.BlockSpec((ts, D_MODEL), lambda i: (prev(i), 0)),
            in_hbm,
            const((1, D_MODEL)),
            in_hbm, in_hbm, in_hbm,
            const((1, D_MODEL)),
        ],
        out_specs=pl.BlockSpec((ts, D_MODEL), lambda i: (last(i), 0)),
        out_shape=jax.ShapeDtypeStruct((t, D_MODEL), F32),
        scratch_shapes=[
            pltpu.VMEM((A_HEADS, A_DV, A_DK), F32),
            pltpu.VMEM((n_chunk, A_HEADS, A_DV, A_DK), F32),
            pltpu.VMEM((n_chunk, A_HEADS, A_DV, A_DK), BF16),
            pltpu.VMEM((2, B_HEADS, 2 * BLOCK, BLOCK), F32),
            pltpu.VMEM((ts, MIX_WIDTH), BF16),
            pltpu.VMEM((ts, MIX_WIDTH), BF16),
            pltpu.VMEM((ts, D_MODEL), F32),
            pltpu.VMEM((ts, D_FF), BF16),
            pltpu.VMEM((ts, D_MODEL), F32),
            pltpu.VMEM((MIX_WIDTH, D_MODEL), BF16),
            pltpu.VMEM((D_MODEL, D_FF), BF16),
            pltpu.VMEM((D_MODEL, D_FF), BF16),
            pltpu.VMEM((D_FF, D_MODEL), BF16),
            pltpu.SemaphoreType.DMA((N_FFN_WEIGHTS,)),
        ],
        compiler_params=pltpu.CompilerParams(
            dimension_semantics=("arbitrary",), vmem_limit_bytes=V7X_VMEM_LIMIT_BYTES),
        name="mixer_ffn",
    )(sinks, qd, kit, ks, hv, gate, dec, bq, kp, kp, vt, vt, attn_norm_g,
      x2d, w_out, ln2_g, w_gate, w_up, w_down, final_g)


def kernel(x, ln1_g, w_in, lb_logits, hgrn_norm_g, attn_sinks, attn_norm_g, w_out, ln2_g,
           w_gate, w_up, w_down, final_g):
    bsz, seq, d = x.shape
    x2d = x.reshape(bsz * seq, d)
    *prep, wo16, wg16, wu16, wd16 = _in_proj(x2d, ln1_g[0:1], w_in[0], lb_logits, hgrn_norm_g[0:1],
                                             w_out[0], w_gate[0], w_up[0], w_down[0])
    out = _mixer_ffn(prep, attn_sinks[0], attn_norm_g[0:1], x2d, wo16, ln2_g[0:1],
                     wg16, wu16, wd16, final_g.reshape(1, d), seq)
    return out.reshape(bsz, seq, d)
```

```python
import functools

import jax
import jax.numpy as jnp
from jax import lax
from jax.experimental import pallas as pl
from jax.experimental.pallas import tpu as pltpu

D_MODEL = 1024
A_HEADS = 4
A_DK = 128
A_DV = 128
A_QK = A_HEADS * A_DK
A_WIDTH = A_HEADS * A_DV
CHUNK = 64
B_HEADS = 8
B_KV_HEADS = 2
B_HEAD_DIM = 64
B_WIDTH = B_HEADS * B_HEAD_DIM
B_KV = B_KV_HEADS * B_HEAD_DIM
WINDOW = 128
BLOCK = 128
MIX_WIDTH = A_WIDTH + B_WIDTH
D_FF = 2816
IN_COLS = 4 * A_QK + B_WIDTH + 2 * B_KV
EPS = 1e-6
NEG_BIG = -1e30
LANES = 128
KEY_VARIANTS = 2 * B_KV_HEADS
N_FFN_WEIGHTS = 4

F32 = jnp.float32
BF16 = jnp.bfloat16

V7X_VMEM_LIMIT_BYTES = 56 * 1024 * 1024

TM_PROJ = 512
TS_MIX = 512
FF_CHUNK = 256
HGRN_GROUP_PAIRS = 2
PAIR = 2 * CHUNK
CAST_ROWS = 128
LOG2E = 1.4426950408889634

NT_DIMS = (((1,), (1,)), ((), ()))
TN_DIMS = (((0,), (0,)), ((), ()))


def _rms_scale(v):
    return lax.rsqrt(jnp.mean(v * v, axis=-1, keepdims=True) + EPS)


def _silu(v):
    return v * (1.0 / (1.0 + jnp.exp(-v)))


def _sigmoid(v):
    return 1.0 / (1.0 + jnp.exp(-v))


def _in_proj_kernel(x_ref, g_ref, w32_ref, lbl_ref, hng_ref, wo32_ref, wg32_ref, wu32_ref, wd32_ref,
                    qa_ref, kat_ref, qs_ref, ku_ref, hv_ref, gate_ref, dec_ref, bq_ref, kp_ref, vt_ref,
                    wo16_ref, wg16_ref, wu16_ref, wd16_ref, w_ref, *, n_down_slabs):
    tm = x_ref.shape[0]

    @pl.when(pl.program_id(0) == 0)
    def _():
        for r0 in range(0, D_MODEL, CAST_ROWS):
            w_ref[r0:r0 + CAST_ROWS, :] = w32_ref[r0:r0 + CAST_ROWS, :].astype(BF16)

    @pl.when(pl.program_id(0) < n_down_slabs)
    def _():
        wd16_ref[...] = wd32_ref[...].astype(BF16)

    x = x_ref[...]
    u = (x * _rms_scale(x) * g_ref[...]).astype(BF16)

    lbl = lbl_ref[...]
    lmax = jnp.max(lbl, axis=0, keepdims=True)
    le = jnp.exp(lbl - lmax)
    lb = le[0:1, :] / jnp.sum(le, axis=0, keepdims=True)

    r = lax.broadcasted_iota(jnp.int32, (CHUNK, CHUNK), 0)
    c = lax.broadcasted_iota(jnp.int32, (CHUNK, CHUNK), 1)
    tri = jnp.where(r >= c, 1.0, 0.0).astype(BF16)

    pf = jnp.dot(u, w_ref[:, A_QK:2 * A_QK], preferred_element_type=F32)
    pq = jnp.dot(u, w_ref[:, 0:A_QK], preferred_element_type=F32)

    def chunk_terms(rows):
        forget = lb + (1.0 - lb) * _sigmoid(pf[rows])
        logf = jnp.log(forget)
        p0 = logf.astype(BF16)
        r1 = logf - p0.astype(F32)
        p1 = r1.astype(BF16)
        p2 = (r1 - p1.astype(F32)).astype(BF16)
        bc3 = jnp.dot(tri, jnp.concatenate([p0, p1, p2], axis=-1), preferred_element_type=F32)
        bcum = bc3[:, :A_QK] + bc3[:, A_QK:2 * A_QK] + bc3[:, 2 * A_QK:]
        q_dec = _silu(pq[rows]) * jnp.exp(bcum)
        k_grow = (1.0 - forget) * jnp.exp(-bcum)
        return q_dec, k_grow, bcum[CHUNK - 1:CHUNK, :]

    for p in range(tm // PAIR):
        r0 = slice(p * PAIR, p * PAIR + CHUNK)
        r1 = slice(p * PAIR + CHUNK, (p + 1) * PAIR)
        qd0, kg0, bl0 = chunk_terms(r0)
        qd1, kg1, bl1 = chunk_terms(r1)
        d0, d1 = jnp.exp(bl0), jnp.exp(bl1)
        ka0 = kg0 * d0
        qa_ref[r0, :] = (qd0 * jnp.exp(-bl0)).astype(BF16)
        qa_ref[r1, :] = qd1.astype(BF16)
        qs_ref[r0, :] = qd0.astype(BF16)
        qs_ref[r1, :] = (qd1 * d0).astype(BF16)
        ku_ref[r0, :] = (ka0 * d1).astype(BF16)
        ku_ref[r1, :] = (kg1 * d1).astype(BF16)
        kat_ref[:, p * PAIR:(p + 1) * PAIR] = jnp.concatenate([ka0, kg1], axis=0).T.astype(BF16)
        dec_ref[2 * p:2 * p + 2, :] = jnp.broadcast_to(jnp.exp(bl0 + bl1), (PAIR // CHUNK, A_QK))

    wo16_ref[...] = wo32_ref[...].astype(BF16)
    wg16_ref[...] = wg32_ref[...].astype(BF16)
    wu16_ref[...] = wu32_ref[...].astype(BF16)

    c0 = 2 * A_QK + 2 * A_WIDTH
    pa = jnp.dot(u, w_ref[:, c0:], preferred_element_type=F32)
    bq_ref[...] = (pa[:, :B_WIDTH] * (B_HEAD_DIM ** -0.5 * LOG2E)).astype(BF16)
    lo = lax.broadcasted_iota(jnp.int32, (tm, LANES), 1) < B_HEAD_DIM
    k_all = pa[:, B_WIDTH:B_WIDTH + B_KV]
    k_rot = pltpu.roll(k_all, B_HEAD_DIM, axis=1)
    kp_ref[0] = jnp.where(lo, k_all, 0.0).astype(BF16)
    kp_ref[1] = jnp.where(lo, 0.0, k_rot).astype(BF16)
    kp_ref[2] = jnp.where(lo, k_rot, 0.0).astype(BF16)
    kp_ref[3] = jnp.where(lo, 0.0, k_all).astype(BF16)
    vt_ref[...] = pa[:, B_WIDTH + B_KV:].T.astype(BF16)

    pg = jnp.dot(u, w_ref[:, 2 * A_QK + A_WIDTH:2 * A_QK + 2 * A_WIDTH], preferred_element_type=F32)
    gate_ref[...] = (_silu(pg) * hng_ref[...]).astype(BF16)
    pi = jnp.dot(u, w_ref[:, 2 * A_QK:2 * A_QK + A_WIDTH], preferred_element_type=F32)
    hv_ref[...] = pi.astype(BF16)


def _in_proj(x2d, ln1_g, w_in, lb_logits, hgrn_norm_g, w_out, w_gate, w_up, w_down):
    t = x2d.shape[0]
    tm = TM_PROJ
    n_steps = t // tm
    row_blk = lambda width: pl.BlockSpec((tm, width), lambda i: (i, 0))
    const = lambda shape, **kw: pl.BlockSpec(shape, lambda i: (0, 0), **kw)
    bf = lambda shape: jax.ShapeDtypeStruct(shape, BF16)

    def slab(w, rows):
        n_slabs = w.shape[0] // rows
        assert w.shape[0] % rows == 0 and n_slabs <= n_steps, (w.shape, rows)
        return pl.BlockSpec((rows, w.shape[1]), lambda i: (jnp.minimum(i, n_slabs - 1), 0))

    cast_weights = (w_out, w_gate, w_up, w_down)
    even_rows = D_MODEL // n_steps
    slab_rows = (even_rows, even_rows, even_rows, CAST_ROWS)
    slab_specs = lambda: [slab(w, rows) for w, rows in zip(cast_weights, slab_rows)]
    return pl.pallas_call(
        functools.partial(_in_proj_kernel, n_down_slabs=D_FF // CAST_ROWS),
        grid=(n_steps,),
        in_specs=[
            row_blk(D_MODEL),
            const((1, D_MODEL)),
            const((D_MODEL, IN_COLS), pipeline_mode=pl.Buffered(1)),
            const((2, A_QK)),
            const((1, A_WIDTH)),
        ] + slab_specs(),
        out_specs=[
            row_blk(A_QK), pl.BlockSpec((A_QK, tm), lambda i: (0, i)), row_blk(A_QK), row_blk(A_QK),
            row_blk(A_WIDTH), row_blk(A_WIDTH),
            pl.BlockSpec((tm // CHUNK, A_QK), lambda i: (i, 0)),
            row_blk(B_WIDTH),
            pl.BlockSpec((KEY_VARIANTS, tm, LANES), lambda i: (0, i, 0)),
            pl.BlockSpec((LANES, tm), lambda i: (0, i)),
        ] + slab_specs(),
        out_shape=[
            bf((t, A_QK)), bf((A_QK, t)), bf((t, A_QK)), bf((t, A_QK)), bf((t, A_WIDTH)), bf((t, A_WIDTH)),
            jax.ShapeDtypeStruct((t // CHUNK, A_QK), F32),
            bf((t, B_WIDTH)),
            bf((KEY_VARIANTS, t, LANES)),
            bf((LANES, t)),
        ] + [bf(w.shape) for w in cast_weights],
        scratch_shapes=[pltpu.VMEM((D_MODEL, IN_COLS), BF16)],
        compiler_params=pltpu.CompilerParams(
            dimension_semantics=("arbitrary",), vmem_limit_bytes=V7X_VMEM_LIMIT_BYTES),
        name="in_proj",
    )(x2d, ln1_g, w_in, lb_logits, hgrn_norm_g, *cast_weights)


def _bias_table_init(bias_ref):
    kj = lax.broadcasted_iota(jnp.int32, (2 * BLOCK, BLOCK), 0)
    qi = lax.broadcasted_iota(jnp.int32, (2 * BLOCK, BLOCK), 1)
    dist = qi + BLOCK - kj
    valid = (dist >= 0) & (dist < WINDOW)
    valid_first = valid & (kj >= BLOCK)
    distf = dist.astype(F32)
    for h in range(B_HEADS):
        slope = 2.0 ** (-(8.0 / B_HEADS) * (h + 1))
        sc = (-slope * LOG2E) * distf
        bias_ref[0, h] = jnp.where(valid, sc, NEG_BIG)
        bias_ref[1, h] = jnp.where(valid_first, sc, NEG_BIG)


def _mixer_stages(first, sinks_ref, qa_ref, kat_ref, qs_ref, ku_ref, hv_ref, gate_ref, dec_ref, bq_ref,
                  kp_ref, kph_ref, vt_ref, vth_ref, ang_ref, y_ref,
                  state_ref, upd_ref, sprev_ref, bias_ref):
    ts = qa_ref.shape[0]
    n_pair = ts // PAIR
    n_blk = ts // BLOCK
    head_cols = [slice(h * A_DK, (h + 1) * A_DK) for h in range(A_HEADS)]
    pair_rows = [slice(p * PAIR, (p + 1) * PAIR) for p in range(n_pair)]

    def hgrn_updates(p_lo, p_hi):
        units = [(p, h) for p in range(p_lo, p_hi) for h in range(A_HEADS)]
        upds = [lax.dot_general(hv_ref[pair_rows[p], head_cols[h]], ku_ref[pair_rows[p], head_cols[h]],
                                TN_DIMS, preferred_element_type=F32) for p, h in units]
        for upd, (p, h) in zip(upds, units):
            upd_ref[p, h] = upd

    def hgrn_scan(h):
        st = state_ref[h]
        for p in range(n_pair):
            sprev_ref[p, h] = st.T.astype(BF16)
            st = st * dec_ref[2 * p:2 * p + 1, head_cols[h]] + upd_ref[p, h]
        state_ref[h] = st

    def hgrn_scores(p, box):
        rows = pair_rows[p]
        box["a"] = [jnp.dot(qa_ref[rows, cs], kat_ref[cs, rows], preferred_element_type=F32)
                    for cs in head_cols]
        box["o_inter"] = [jnp.dot(qs_ref[rows, cs], sprev_ref[p, h], preferred_element_type=F32)
                          for h, cs in enumerate(head_cols)]

    def hgrn_values(p, box):
        r = lax.broadcasted_iota(jnp.int32, (PAIR, PAIR), 0)
        c = lax.broadcasted_iota(jnp.int32, (PAIR, PAIR), 1)
        a_s = [jnp.where(c <= r, a, 0.0).astype(BF16) for a in box["a"]]
        box["o"] = [jnp.dot(a, hv_ref[pair_rows[p], cs], preferred_element_type=F32) + oi
                    for a, oi, cs in zip(a_s, box["o_inter"], head_cols)]

    def hgrn_store(p, box):
        rows = pair_rows[p]
        for o, cs in zip(box["o"], head_cols):
            gate = gate_ref[rows, cs].astype(F32)
            y_ref[rows, cs] = (o * _rms_scale(o) * gate).astype(y_ref.dtype)

    swa_units = [(kvh, par) for kvh in range(B_KV_HEADS) for par in range(2)]

    def swa_scores(j, box):
        s_ts = []
        for kvh, par in swa_units:
            rows = slice(j * BLOCK, (j + 1) * BLOCK)
            q_pairs = jnp.concatenate(
                [bq_ref[rows, (2 * kvh) * LANES:(2 * kvh + 1) * LANES],
                 bq_ref[rows, (2 * kvh + 1) * LANES:(2 * kvh + 2) * LANES]], axis=0)
            idx = 2 * kvh + par
            if j == 0:
                kk = jnp.concatenate([kph_ref[idx], kp_ref[idx, 0:BLOCK, :]], axis=0)
            else:
                kk = kp_ref[idx, (j - 1) * BLOCK:(j + 1) * BLOCK, :]
            s_ts.append(lax.dot_general(kk, q_pairs, NT_DIMS, preferred_element_type=F32))
        box["s"] = s_ts

    def swa_softmax(j, box):
        p_ts, inv_ls = [], []
        for (kvh, par), s_t in zip(swa_units, box["s"]):
            p_halves = []
            for half, h in enumerate((4 * kvh + par, 4 * kvh + 2 + par)):
                bias = bias_ref[first, h] if j == 0 else bias_ref[0, h]
                s = s_t[:, half * BLOCK:(half + 1) * BLOCK] + bias
                sink = sinks_ref[h] * LOG2E
                m = jnp.maximum(jnp.max(s, axis=0, keepdims=True), sink)
                p = jnp.exp2(s - m)
                inv_ls.append(1.0 / (jnp.sum(p, axis=0, keepdims=True) + jnp.exp2(sink - m)))
                p_halves.append(p.astype(BF16))
            p_ts.append(jnp.concatenate(p_halves, axis=1))
        box["p"], box["inv_l"] = p_ts, inv_ls

    def swa_values(j, box):
        o_t = [None] * B_HEADS
        for ui, ((kvh, par), p_t) in enumerate(zip(swa_units, box["p"])):
            drow = slice(kvh * B_HEAD_DIM, (kvh + 1) * B_HEAD_DIM)
            if j == 0:
                vt_h = jnp.concatenate([vth_ref[drow, :], vt_ref[drow, 0:BLOCK]], axis=1)
            else:
                vt_h = vt_ref[drow, (j - 1) * BLOCK:(j + 1) * BLOCK]
            pv = jnp.dot(vt_h, p_t, preferred_element_type=F32)
            for half, h in enumerate((4 * kvh + par, 4 * kvh + 2 + par)):
                o_t[h] = pv[:, half * BLOCK:(half + 1) * BLOCK] * box["inv_l"][2 * ui + half]
        box["o_t"] = o_t

    def swa_store(j, box):
        o_all_t = jnp.concatenate(box["o_t"], axis=0)
        scale = lax.rsqrt(jnp.sum(o_all_t * o_all_t, axis=0, keepdims=True) * (1.0 / B_WIDTH) + EPS)
        o_all = (o_all_t * scale).T
        y_ref[j * BLOCK:(j + 1) * BLOCK, A_WIDTH:] = (o_all * ang_ref[...]).astype(y_ref.dtype)

    group = HGRN_GROUP_PAIRS
    hgrn = [functools.partial(hgrn_updates, p, p + group) for p in range(0, n_pair, group)]
    hgrn += [functools.partial(hgrn_scan, h) for h in range(A_HEADS)]
    for p in range(n_pair):
        box = {}
        hgrn += [functools.partial(f, p, box) for f in (hgrn_scores, hgrn_values, hgrn_store)]
    swa = []
    for j in range(n_blk):
        box = {}
        swa += [functools.partial(f, j, box) for f in (swa_scores, swa_softmax, swa_values, swa_store)]
    stages = []
    for k in range(max(len(hgrn), len(swa))):
        stages += hgrn[k:k + 1] + swa[k:k + 1]
    return stages


def _ffn_stages(x_ref, y_ref, wo_ref, g2_ref, wg_ref, wu_ref, wd_ref, h_ref, act_ref, h2_ref,
                before_down=()):
    box = {}

    def head():
        h = x_ref[...] + jnp.dot(y_ref[...], wo_ref[...], preferred_element_type=F32)
        box["u"] = (h * _rms_scale(h) * g2_ref[...]).astype(BF16)
        h_ref[...] = h

    def ff_up(c0):
        cs = slice(c0, c0 + FF_CHUNK)
        box["gate"] = jnp.dot(box["u"], wg_ref[:, cs], preferred_element_type=F32)
        box["up"] = jnp.dot(box["u"], wu_ref[:, cs], preferred_element_type=F32)

    def ff_act(c0):
        act_ref[:, c0:c0 + FF_CHUNK] = (_silu(box["gate"]) * box["up"]).astype(BF16)

    def down():
        h2_ref[...] = h_ref[...] + jnp.dot(act_ref[...], wd_ref[...], preferred_element_type=F32)

    stages = [head]
    for c0 in range(0, D_FF, FF_CHUNK):
        stages += [functools.partial(ff_up, c0), functools.partial(ff_act, c0)]
    return stages + list(before_down) + [down]


def _final_norm(h2_ref, gf_ref, o_ref):
    h2 = h2_ref[...]
    o_ref[...] = h2 * _rms_scale(h2) * gf_ref[...]


def _run_interleaved(major, minor):
    span = max(len(major) - 2, 1)
    done = 0
    for k, stage in enumerate(major):
        stage()
        want = min(len(minor), -(-(k + 1) * len(minor) // span))
        while done < want:
            minor[done]()
            done += 1


def _mixer_ffn_kernel(sinks_ref, qa_ref, kat_ref, qs_ref, ku_ref, hv_ref, gate_ref, dec_ref, bq_ref,
                      kp_ref, kph_ref, vt_ref, vth_ref, ang_ref,
                      x_ref, wo_hbm, g2_ref, wg_hbm, wu_hbm, wd_hbm, gf_ref, o_ref,
                      state_ref, upd_ref, sprev_ref, bias_ref, y_cur, y_prev, h_ref, act_ref, h2_ref,
                      wo_ref, wg_ref, wu_ref, wd_ref, w_sems, *, n_tiles, n_s):
    i = pl.program_id(0)
    seq_start = lax.rem(i, n_s) == 0
    weight_copies = [
        pltpu.make_async_copy(src, dst, w_sems.at[k])
        for k, (src, dst) in enumerate(((wo_hbm, wo_ref), (wg_hbm, wg_ref),
                                        (wu_hbm, wu_ref), (wd_hbm, wd_ref)))]

    @pl.when(i == 0)
    def _():
        for copy in weight_copies:
            copy.start()
        _bias_table_init(bias_ref)

    @pl.when(i == 1)
    def _():
        for copy in weight_copies:
            copy.wait()

    @pl.when(seq_start)
    def _():
        state_ref[...] = jnp.zeros_like(state_ref)

    first = jnp.where(seq_start, 1, 0)
    mixer_args = (first, sinks_ref, qa_ref, kat_ref, qs_ref, ku_ref, hv_ref, gate_ref, dec_ref, bq_ref,
                  kp_ref, kph_ref, vt_ref, vth_ref, ang_ref, y_cur,
                  state_ref, upd_ref, sprev_ref, bias_ref)
    ffn_args = (x_ref, y_prev, wo_ref, g2_ref, wg_ref, wu_ref, wd_ref, h_ref, act_ref, h2_ref)
    final_norm = functools.partial(_final_norm, h2_ref, gf_ref, o_ref)

    @pl.when(i == 0)
    def _():
        _run_interleaved(_mixer_stages(*mixer_args), [])
        y_prev[...] = y_cur[...]

    @pl.when(i == 1)
    def _():
        _run_interleaved(_ffn_stages(*ffn_args), _mixer_stages(*mixer_args))
        y_prev[...] = y_cur[...]

    @pl.when((i > 1) & (i < n_tiles))
    def _():
        _run_interleaved(_ffn_stages(*ffn_args, before_down=[final_norm]),
                         _mixer_stages(*mixer_args))
        y_prev[...] = y_cur[...]

    @pl.when(i == n_tiles)
    def _():
        _run_interleaved(_ffn_stages(*ffn_args, before_down=[final_norm]), [])

    @pl.when(i == n_tiles + 1)
    def _():
        final_norm()


def _mixer_ffn(prep, sinks, attn_norm_g, x2d, w_out, ln2_g, w_gate, w_up, w_down, final_g, seq):
    qa, kat, qs, ku, hv, gate, dec, bq, kp, vt = prep
    t = x2d.shape[0]
    ts = TS_MIX
    n_tiles = t // ts
    n_s = seq // ts
    n_chunk = ts // CHUNK
    n_pair = ts // PAIR
    n_blk = ts // BLOCK
    cur = lambda i: jnp.minimum(i, n_tiles - 1)
    prev = lambda i: jnp.clip(i - 1, 0, n_tiles - 1)
    last = lambda i: jnp.clip(i - 2, 0, n_tiles - 1)
    halo = lambda i: jnp.maximum(cur(i) * n_blk - 1, 0)
    row_blk = lambda width: pl.BlockSpec((ts, width), lambda i: (cur(i), 0))
    const = lambda shape: pl.BlockSpec(shape, lambda i: (0, 0), pipeline_mode=pl.Buffered(1))
    in_hbm = pl.BlockSpec(memory_space=pl.ANY)
    assert n_tiles >= 2
    return pl.pallas_call(
        functools.partial(_mixer_ffn_kernel, n_tiles=n_tiles, n_s=n_s),
        grid=(n_tiles + 2,),
        in_specs=[
            pl.BlockSpec(memory_space=pltpu.SMEM),
            row_blk(A_QK), pl.BlockSpec((A_QK, ts), lambda i: (0, cur(i))), row_blk(A_QK), row_blk(A_QK),
            row_blk(A_WIDTH), row_blk(A_WIDTH),
            pl.BlockSpec((n_chunk, A_QK), lambda i: (cur(i), 0)),
            row_blk(B_WIDTH),
            pl.BlockSpec((KEY_VARIANTS, ts, LANES), lambda i: (0, cur(i), 0)),
            pl.BlockSpec((KEY_VARIANTS, BLOCK, LANES), lambda i: (0, halo(i), 0)),
            pl.BlockSpec((LANES, ts), lambda i: (0, cur(i))),
            pl.BlockSpec((LANES, BLOCK), lambda i: (0, halo(i))),
            pl.BlockSpec((1, B_WIDTH), lambda i: (0, 0)),
            pl.BlockSpec((ts, D_MODEL), lambda i: (prev(i), 0)),
            in_hbm,
            const((1, D_MODEL)),
            in_hbm, in_hbm, in_hbm,
            const((1, D_MODEL)),
        ],
        out_specs=pl.BlockSpec((ts, D_MODEL), lambda i: (last(i), 0)),
        out_shape=jax.ShapeDtypeStruct((t, D_MODEL), F32),
        scratch_shapes=[
            pltpu.VMEM((A_HEADS, A_DV, A_DK), F32),
            pltpu.VMEM((n_pair, A_HEADS, A_DV, A_DK), F32),
            pltpu.VMEM((n_pair, A_HEADS, A_DV, A_DK), BF16),
            pltpu.VMEM((2, B_HEADS, 2 * BLOCK, BLOCK), F32),
            pltpu.VMEM((ts, MIX_WIDTH), BF16),
            pltpu.VMEM((ts, MIX_WIDTH), BF16),
            pltpu.VMEM((ts, D_MODEL), F32),
            pltpu.VMEM((ts, D_FF), BF16),
            pltpu.VMEM((ts, D_MODEL), F32),
            pltpu.VMEM((MIX_WIDTH, D_MODEL), BF16),
            pltpu.VMEM((D_MODEL, D_FF), BF16),
            pltpu.VMEM((D_MODEL, D_FF), BF16),
            pltpu.VMEM((D_FF, D_MODEL), BF16),
            pltpu.SemaphoreType.DMA((N_FFN_WEIGHTS,)),
        ],
        compiler_params=pltpu.CompilerParams(
            dimension_semantics=("arbitrary",), vmem_limit_bytes=V7X_VMEM_LIMIT_BYTES),
        name="mixer_ffn",
    )(sinks, qa, kat, qs, ku, hv, gate, dec, bq, kp, kp, vt, vt, attn_norm_g,
      x2d, w_out, ln2_g, w_gate, w_up, w_down, final_g)


def kernel(x, ln1_g, w_in, lb_logits, hgrn_norm_g, attn_sinks, attn_norm_g, w_out, ln2_g,
           w_gate, w_up, w_down, final_g):
    bsz, seq, d = x.shape
    x2d = x.reshape(bsz * seq, d)
    *prep, wo16, wg16, wu16, wd16 = _in_proj(x2d, ln1_g[0:1], w_in[0], lb_logits, hgrn_norm_g[0:1],
                                             w_out[0], w_gate[0], w_up[0], w_down[0])
    out = _mixer_ffn(prep, attn_sinks[0], attn_norm_g[0:1], x2d, wo16, ln2_g[0:1],
                     wg16, wu16, wd16, final_g.reshape(1, d), seq)
    return out.reshape(bsz, seq, d)
```

```python
import functools

import jax
import jax.numpy as jnp
from jax import lax
from jax.experimental import pallas as pl
from jax.experimental.pallas import tpu as pltpu

D_MODEL = 1024
A_HEADS = 4
A_DK = 128
A_DV = 128
A_QK = A_HEADS * A_DK
A_WIDTH = A_HEADS * A_DV
CHUNK = 64
B_HEADS = 8
B_KV_HEADS = 2
B_HEAD_DIM = 64
B_WIDTH = B_HEADS * B_HEAD_DIM
B_KV = B_KV_HEADS * B_HEAD_DIM
WINDOW = 128
BLOCK = 128
MIX_WIDTH = A_WIDTH + B_WIDTH
D_FF = 2816
IN_COLS = 4 * A_QK + B_WIDTH + 2 * B_KV
EPS = 1e-6
NEG_BIG = -1e30
LANES = 128
KEY_VARIANTS = 2 * B_KV_HEADS
N_FFN_WEIGHTS = 4

F32 = jnp.float32
BF16 = jnp.bfloat16

V7X_VMEM_LIMIT_BYTES = 56 * 1024 * 1024

TM_PROJ = 512
TS_MIX = 512
FF_CHUNK = 256
HGRN_GROUP_CHUNKS = 2
PAIR = 2 * CHUNK
CAST_ROWS = 128
LOG2E = 1.4426950408889634

NT_DIMS = (((1,), (1,)), ((), ()))
TN_DIMS = (((0,), (0,)), ((), ()))


def _rms_scale(v):
    return lax.rsqrt(jnp.mean(v * v, axis=-1, keepdims=True) + EPS)


def _silu(v):
    return v * (1.0 / (1.0 + jnp.exp(-v)))


def _sigmoid(v):
    return 1.0 / (1.0 + jnp.exp(-v))


def _in_proj_kernel(x_ref, g_ref, w32_ref, lbl_ref, hng_ref, wo32_ref, wg32_ref, wu32_ref, wd32_ref,
                    qd_ref, kit_ref, ks_ref, hv_ref, gate_ref, dec_ref, bq_ref, kp_ref, vt_ref,
                    wo16_ref, wg16_ref, wu16_ref, wd16_ref, w_ref, *, n_down_slabs):
    tm = x_ref.shape[0]

    @pl.when(pl.program_id(0) == 0)
    def _():
        for r0 in range(0, D_MODEL, CAST_ROWS):
            w_ref[r0:r0 + CAST_ROWS, :] = w32_ref[r0:r0 + CAST_ROWS, :].astype(BF16)

    @pl.when(pl.program_id(0) < n_down_slabs)
    def _():
        wd16_ref[...] = wd32_ref[...].astype(BF16)

    x = x_ref[...]
    u = (x * _rms_scale(x) * g_ref[...]).astype(BF16)

    lbl = lbl_ref[...]
    lmax = jnp.max(lbl, axis=0, keepdims=True)
    le = jnp.exp(lbl - lmax)
    lb = le[0:1, :] / jnp.sum(le, axis=0, keepdims=True)

    r = lax.broadcasted_iota(jnp.int32, (CHUNK, CHUNK), 0)
    c = lax.broadcasted_iota(jnp.int32, (CHUNK, CHUNK), 1)
    tri = jnp.where(r >= c, 1.0, 0.0).astype(BF16)

    pf = jnp.dot(u, w_ref[:, A_QK:2 * A_QK], preferred_element_type=F32)
    pq = jnp.dot(u, w_ref[:, 0:A_QK], preferred_element_type=F32)
    k_pair = []
    for ci in range(tm // CHUNK):
        rows = slice(ci * CHUNK, (ci + 1) * CHUNK)
        forget = lb + (1.0 - lb) * _sigmoid(pf[rows])
        key = 1.0 - forget
        logf = jnp.log(forget)
        p0 = logf.astype(BF16)
        r1 = logf - p0.astype(F32)
        p1 = r1.astype(BF16)
        p2 = (r1 - p1.astype(F32)).astype(BF16)
        bc3 = jnp.dot(tri, jnp.concatenate([p0, p1, p2], axis=-1), preferred_element_type=F32)
        bcum = bc3[:, :A_QK] + bc3[:, A_QK:2 * A_QK] + bc3[:, 2 * A_QK:]
        decay = jnp.exp(bcum[CHUNK - 1:CHUNK, :])
        k_intra = key * jnp.exp(-bcum)
        qd_ref[rows, :] = (_silu(pq[rows]) * jnp.exp(bcum)).astype(BF16)
        ks_ref[rows, :] = (k_intra * decay).astype(BF16)
        dec_ref[ci:ci + 1, :] = decay
        k_pair.append(k_intra)
        if len(k_pair) == PAIR // CHUNK:
            p0_row = (ci + 1) * CHUNK - PAIR
            kit_ref[:, p0_row:p0_row + PAIR] = jnp.concatenate(k_pair, axis=0).T.astype(BF16)
            k_pair = []

    wo16_ref[...] = wo32_ref[...].astype(BF16)
    wg16_ref[...] = wg32_ref[...].astype(BF16)
    wu16_ref[...] = wu32_ref[...].astype(BF16)

    c0 = 2 * A_QK + 2 * A_WIDTH
    pa = jnp.dot(u, w_ref[:, c0:], preferred_element_type=F32)
    bq_ref[...] = (pa[:, :B_WIDTH] * (B_HEAD_DIM ** -0.5 * LOG2E)).astype(BF16)
    lo = lax.broadcasted_iota(jnp.int32, (tm, LANES), 1) < B_HEAD_DIM
    k_all = pa[:, B_WIDTH:B_WIDTH + B_KV]
    k_rot = pltpu.roll(k_all, B_HEAD_DIM, axis=1)
    kp_ref[0] = jnp.where(lo, k_all, 0.0).astype(BF16)
    kp_ref[1] = jnp.where(lo, 0.0, k_rot).astype(BF16)
    kp_ref[2] = jnp.where(lo, k_rot, 0.0).astype(BF16)
    kp_ref[3] = jnp.where(lo, 0.0, k_all).astype(BF16)
    vt_ref[...] = pa[:, B_WIDTH + B_KV:].T.astype(BF16)

    pg = jnp.dot(u, w_ref[:, 2 * A_QK + A_WIDTH:2 * A_QK + 2 * A_WIDTH], preferred_element_type=F32)
    gate_ref[...] = (_silu(pg) * hng_ref[...]).astype(BF16)
    pi = jnp.dot(u, w_ref[:, 2 * A_QK:2 * A_QK + A_WIDTH], preferred_element_type=F32)
    hv_ref[...] = pi.astype(BF16)


def _in_proj(x2d, ln1_g, w_in, lb_logits, hgrn_norm_g, w_out, w_gate, w_up, w_down):
    t = x2d.shape[0]
    tm = TM_PROJ
    n_steps = t // tm
    row_blk = lambda width: pl.BlockSpec((tm, width), lambda i: (i, 0))
    const = lambda shape, **kw: pl.BlockSpec(shape, lambda i: (0, 0), **kw)
    bf = lambda shape: jax.ShapeDtypeStruct(shape, BF16)

    def slab(w, rows):
        n_slabs = w.shape[0] // rows
        assert w.shape[0] % rows == 0 and n_slabs <= n_steps, (w.shape, rows)
        return pl.BlockSpec((rows, w.shape[1]), lambda i: (jnp.minimum(i, n_slabs - 1), 0))

    cast_weights = (w_out, w_gate, w_up, w_down)
    even_rows = D_MODEL // n_steps
    slab_rows = (even_rows, even_rows, even_rows, CAST_ROWS)
    slab_specs = lambda: [slab(w, rows) for w, rows in zip(cast_weights, slab_rows)]
    return pl.pallas_call(
        functools.partial(_in_proj_kernel, n_down_slabs=D_FF // CAST_ROWS),
        grid=(n_steps,),
        in_specs=[
            row_blk(D_MODEL),
            const((1, D_MODEL)),
            const((D_MODEL, IN_COLS), pipeline_mode=pl.Buffered(1)),
            const((2, A_QK)),
            const((1, A_WIDTH)),
        ] + slab_specs(),
        out_specs=[
            row_blk(A_QK), pl.BlockSpec((A_QK, tm), lambda i: (0, i)), row_blk(A_QK),
            row_blk(A_WIDTH), row_blk(A_WIDTH),
            pl.BlockSpec((tm // CHUNK, A_QK), lambda i: (i, 0)),
            row_blk(B_WIDTH),
            pl.BlockSpec((KEY_VARIANTS, tm, LANES), lambda i: (0, i, 0)),
            pl.BlockSpec((LANES, tm), lambda i: (0, i)),
        ] + slab_specs(),
        out_shape=[
            bf((t, A_QK)), bf((A_QK, t)), bf((t, A_QK)), bf((t, A_WIDTH)), bf((t, A_WIDTH)),
            jax.ShapeDtypeStruct((t // CHUNK, A_QK), F32),
            bf((t, B_WIDTH)),
            bf((KEY_VARIANTS, t, LANES)),
            bf((LANES, t)),
        ] + [bf(w.shape) for w in cast_weights],
        scratch_shapes=[pltpu.VMEM((D_MODEL, IN_COLS), BF16)],
        compiler_params=pltpu.CompilerParams(
            dimension_semantics=("arbitrary",), vmem_limit_bytes=V7X_VMEM_LIMIT_BYTES),
        name="in_proj",
    )(x2d, ln1_g, w_in, lb_logits, hgrn_norm_g, *cast_weights)


def _bias_table_init(bias_ref):
    kj = lax.broadcasted_iota(jnp.int32, (2 * BLOCK, BLOCK), 0)
    qi = lax.broadcasted_iota(jnp.int32, (2 * BLOCK, BLOCK), 1)
    dist = qi + BLOCK - kj
    valid = (dist >= 0) & (dist < WINDOW)
    valid_first = valid & (kj >= BLOCK)
    distf = dist.astype(F32)
    for h in range(B_HEADS):
        slope = 2.0 ** (-(8.0 / B_HEADS) * (h + 1))
        sc = (-slope * LOG2E) * distf
        bias_ref[0, h] = jnp.where(valid, sc, NEG_BIG)
        bias_ref[1, h] = jnp.where(valid_first, sc, NEG_BIG)


def _mixer_stages(first, sinks_ref, qd_ref, kit_ref, ks_ref, hv_ref, gate_ref, dec_ref, bq_ref,
                  kp_ref, kph_ref, vt_ref, vth_ref, ang_ref, y_ref,
                  state_ref, upd_ref, sprev_ref, bias_ref):
    ts = qd_ref.shape[0]
    n_chunk = ts // CHUNK
    n_blk = ts // BLOCK
    head_cols = [slice(h * A_DK, (h + 1) * A_DK) for h in range(A_HEADS)]

    def chunk_units(c_lo, c_hi):
        return [(ci, slice(ci * CHUNK, (ci + 1) * CHUNK), head_cols[h], h)
                for ci in range(c_lo, c_hi) for h in range(A_HEADS)]

    def hgrn_updates(c_lo, c_hi):
        units = chunk_units(c_lo, c_hi)
        upds = [lax.dot_general(hv_ref[rows, cs], ks_ref[rows, cs], TN_DIMS,
                                preferred_element_type=F32) for (_, rows, cs, _) in units]
        for upd, (ci, _, _, h) in zip(upds, units):
            upd_ref[ci, h] = upd

    def hgrn_scan(h):
        st = state_ref[h]
        for ci in range(n_chunk):
            sprev_ref[ci, h] = st.T.astype(BF16)
            st = st * dec_ref[ci:ci + 1, head_cols[h]] + upd_ref[ci, h]
        state_ref[h] = st

    def hgrn_scores(p, box):
        rows = slice(p * PAIR, (p + 1) * PAIR)
        box["a"] = [jnp.dot(qd_ref[rows, cs], kit_ref[cs, rows], preferred_element_type=F32)
                    for cs in head_cols]
        box["o_inter"] = [
            jnp.concatenate(
                [jnp.dot(qd_ref[ci * CHUNK:(ci + 1) * CHUNK, cs], sprev_ref[ci, h],
                         preferred_element_type=F32)
                 for ci in range(p * PAIR // CHUNK, (p + 1) * PAIR // CHUNK)], axis=0)
            for h, cs in enumerate(head_cols)]

    def hgrn_values(p, box):
        rows = slice(p * PAIR, (p + 1) * PAIR)
        r = lax.broadcasted_iota(jnp.int32, (PAIR, PAIR), 0)
        c = lax.broadcasted_iota(jnp.int32, (PAIR, PAIR), 1)
        same_chunk_causal = (c <= r) & (c >= (r // CHUNK) * CHUNK)
        a_s = [jnp.where(same_chunk_causal, a, 0.0).astype(BF16) for a in box["a"]]
        box["o"] = [jnp.dot(a, hv_ref[rows, cs], preferred_element_type=F32) + oi
                    for a, oi, cs in zip(a_s, box["o_inter"], head_cols)]

    def hgrn_store(p, box):
        rows = slice(p * PAIR, (p + 1) * PAIR)
        for o, cs in zip(box["o"], head_cols):
            gate = gate_ref[rows, cs].astype(F32)
            y_ref[rows, cs] = (o * _rms_scale(o) * gate).astype(y_ref.dtype)

    swa_units = [(kvh, par) for kvh in range(B_KV_HEADS) for par in range(2)]

    def swa_scores(j, box):
        s_ts = []
        for kvh, par in swa_units:
            rows = slice(j * BLOCK, (j + 1) * BLOCK)
            q_pairs = jnp.concatenate(
                [bq_ref[rows, (2 * kvh) * LANES:(2 * kvh + 1) * LANES],
                 bq_ref[rows, (2 * kvh + 1) * LANES:(2 * kvh + 2) * LANES]], axis=0)
            idx = 2 * kvh + par
            if j == 0:
                kk = jnp.concatenate([kph_ref[idx], kp_ref[idx, 0:BLOCK, :]], axis=0)
            else:
                kk = kp_ref[idx, (j - 1) * BLOCK:(j + 1) * BLOCK, :]
            s_ts.append(lax.dot_general(kk, q_pairs, NT_DIMS, preferred_element_type=F32))
        box["s"] = s_ts

    def swa_softmax(j, box):
        p_ts, inv_ls = [], []
        for (kvh, par), s_t in zip(swa_units, box["s"]):
            p_halves = []
            for half, h in enumerate((4 * kvh + par, 4 * kvh + 2 + par)):
                bias = bias_ref[first, h] if j == 0 else bias_ref[0, h]
                s = s_t[:, half * BLOCK:(half + 1) * BLOCK] + bias
                sink = sinks_ref[h] * LOG2E
                m = jnp.maximum(jnp.max(s, axis=0, keepdims=True), sink)
                p = jnp.exp2(s - m)
                inv_ls.append(1.0 / (jnp.sum(p, axis=0, keepdims=True) + jnp.exp2(sink - m)))
                p_halves.append(p.astype(BF16))
            p_ts.append(jnp.concatenate(p_halves, axis=1))
        box["p"], box["inv_l"] = p_ts, inv_ls

    def swa_values(j, box):
        o_t = [None] * B_HEADS
        for ui, ((kvh, par), p_t) in enumerate(zip(swa_units, box["p"])):
            drow = slice(kvh * B_HEAD_DIM, (kvh + 1) * B_HEAD_DIM)
            if j == 0:
                vt_h = jnp.concatenate([vth_ref[drow, :], vt_ref[drow, 0:BLOCK]], axis=1)
            else:
                vt_h = vt_ref[drow, (j - 1) * BLOCK:(j + 1) * BLOCK]
            pv = jnp.dot(vt_h, p_t, preferred_element_type=F32)
            for half, h in enumerate((4 * kvh + par, 4 * kvh + 2 + par)):
                o_t[h] = pv[:, half * BLOCK:(half + 1) * BLOCK] * box["inv_l"][2 * ui + half]
        box["o_t"] = o_t

    def swa_store(j, box):
        o_all_t = jnp.concatenate(box["o_t"], axis=0)
        scale = lax.rsqrt(jnp.sum(o_all_t * o_all_t, axis=0, keepdims=True) * (1.0 / B_WIDTH) + EPS)
        o_all = (o_all_t * scale).T
        y_ref[j * BLOCK:(j + 1) * BLOCK, A_WIDTH:] = (o_all * ang_ref[...]).astype(y_ref.dtype)

    group = HGRN_GROUP_CHUNKS
    hgrn = [functools.partial(hgrn_updates, c, c + group) for c in range(0, n_chunk, group)]
    hgrn += [functools.partial(hgrn_scan, h) for h in range(A_HEADS)]
    for p in range(ts // PAIR):
        box = {}
        hgrn += [functools.partial(f, p, box) for f in (hgrn_scores, hgrn_values, hgrn_store)]
    swa = []
    for j in range(n_blk):
        box = {}
        swa += [functools.partial(f, j, box) for f in (swa_scores, swa_softmax, swa_values, swa_store)]
    stages = []
    for k in range(max(len(hgrn), len(swa))):
        stages += hgrn[k:k + 1] + swa[k:k + 1]
    return stages


def _ffn_stages(x_ref, y_ref, wo_ref, g2_ref, wg_ref, wu_ref, wd_ref, h_ref, act_ref, h2_ref,
                before_down=()):
    box = {}

    def head():
        h = x_ref[...] + jnp.dot(y_ref[...], wo_ref[...], preferred_element_type=F32)
        box["u"] = (h * _rms_scale(h) * g2_ref[...]).astype(BF16)
        h_ref[...] = h

    def ff_up(c0):
        cs = slice(c0, c0 + FF_CHUNK)
        box["gate"] = jnp.dot(box["u"], wg_ref[:, cs], preferred_element_type=F32)
        box["up"] = jnp.dot(box["u"], wu_ref[:, cs], preferred_element_type=F32)

    def ff_act(c0):
        act_ref[:, c0:c0 + FF_CHUNK] = (_silu(box["gate"]) * box["up"]).astype(BF16)

    def down():
        h2_ref[...] = h_ref[...] + jnp.dot(act_ref[...], wd_ref[...], preferred_element_type=F32)

    stages = [head]
    for c0 in range(0, D_FF, FF_CHUNK):
        stages += [functools.partial(ff_up, c0), functools.partial(ff_act, c0)]
    return stages + list(before_down) + [down]


def _final_norm(h2_ref, gf_ref, o_ref):
    h2 = h2_ref[...]
    o_ref[...] = h2 * _rms_scale(h2) * gf_ref[...]


def _run_interleaved(major, minor):
    span = max(len(major) - 2, 1)
    done = 0
    for k, stage in enumerate(major):
        stage()
        want = min(len(minor), -(-(k + 1) * len(minor) // span))
        while done < want:
            minor[done]()
            done += 1


def _mixer_ffn_kernel(sinks_ref, qd_ref, kit_ref, ks_ref, hv_ref, gate_ref, dec_ref, bq_ref,
                      kp_ref, kph_ref, vt_ref, vth_ref, ang_ref,
                      x_ref, wo_hbm, g2_ref, wg_hbm, wu_hbm, wd_hbm, gf_ref, o_ref,
                      state_ref, upd_ref, sprev_ref, bias_ref, y_cur, y_prev, h_ref, act_ref, h2_ref,
                      wo_ref, wg_ref, wu_ref, wd_ref, w_sems, *, n_tiles, n_s):
    i = pl.program_id(0)
    seq_start = lax.rem(i, n_s) == 0
    weight_copies = [
        pltpu.make_async_copy(src, dst, w_sems.at[k])
        for k, (src, dst) in enumerate(((wo_hbm, wo_ref), (wg_hbm, wg_ref),
                                        (wu_hbm, wu_ref), (wd_hbm, wd_ref)))]

    @pl.when(i == 0)
    def _():
        for copy in weight_copies:
            copy.start()
        _bias_table_init(bias_ref)

    @pl.when(i == 1)
    def _():
        for copy in weight_copies:
            copy.wait()

    @pl.when(seq_start)
    def _():
        state_ref[...] = jnp.zeros_like(state_ref)

    first = jnp.where(seq_start, 1, 0)
    mixer_args = (first, sinks_ref, qd_ref, kit_ref, ks_ref, hv_ref, gate_ref, dec_ref, bq_ref,
                  kp_ref, kph_ref, vt_ref, vth_ref, ang_ref, y_cur,
                  state_ref, upd_ref, sprev_ref, bias_ref)
    ffn_args = (x_ref, y_prev, wo_ref, g2_ref, wg_ref, wu_ref, wd_ref, h_ref, act_ref, h2_ref)
    final_norm = functools.partial(_final_norm, h2_ref, gf_ref, o_ref)

    @pl.when(i == 0)
    def _():
        _run_interleaved(_mixer_stages(*mixer_args), [])
        y_prev[...] = y_cur[...]

    @pl.when(i == 1)
    def _():
        _run_interleaved(_ffn_stages(*ffn_args), _mixer_stages(*mixer_args))
        y_prev[...] = y_cur[...]

    @pl.when((i > 1) & (i < n_tiles))
    def _():
        _run_interleaved(_ffn_stages(*ffn_args, before_down=[final_norm]),
                         _mixer_stages(*mixer_args))
        y_prev[...] = y_cur[...]

    @pl.when(i == n_tiles)
    def _():
        _run_interleaved(_ffn_stages(*ffn_args, before_down=[final_norm]), [])

    @pl.when(i == n_tiles + 1)
    def _():
        final_norm()


def _mixer_ffn(prep, sinks, attn_norm_g, x2d, w_out, ln2_g, w_gate, w_up, w_down, final_g, seq):
    qd, kit, ks, hv, gate, dec, bq, kp, vt = prep
    t = x2d.shape[0]
    ts = TS_MIX
    n_tiles = t // ts
    n_s = seq // ts
    n_chunk = ts // CHUNK
    n_blk = ts // BLOCK
    cur = lambda i: jnp.minimum(i, n_tiles - 1)
    prev = lambda i: jnp.clip(i - 1, 0, n_tiles - 1)
    last = lambda i: jnp.clip(i - 2, 0, n_tiles - 1)
    halo = lambda i: jnp.maximum(cur(i) * n_blk - 1, 0)
    row_blk = lambda width: pl.BlockSpec((ts, width), lambda i: (cur(i), 0))
    const = lambda shape: pl.BlockSpec(shape, lambda i: (0, 0), pipeline_mode=pl.Buffered(1))
    in_hbm = pl.BlockSpec(memory_space=pl.ANY)
    assert n_tiles >= 2
    return pl.pallas_call(
        functools.partial(_mixer_ffn_kernel, n_tiles=n_tiles, n_s=n_s),
        grid=(n_tiles + 2,),
        in_specs=[
            pl.BlockSpec(memory_space=pltpu.SMEM),
            row_blk(A_QK), pl.BlockSpec((A_QK, ts), lambda i: (0, cur(i))), row_blk(A_QK),
            row_blk(A_WIDTH), row_blk(A_WIDTH),
            pl.BlockSpec((n_chunk, A_QK), lambda i: (cur(i), 0)),
            row_blk(B_WIDTH),
            pl.BlockSpec((KEY_VARIANTS, ts, LANES), lambda i: (0, cur(i), 0)),
            pl.BlockSpec((KEY_VARIANTS, BLOCK, LANES), lambda i: (0, halo(i), 0)),
            pl.BlockSpec((LANES, ts), lambda i: (0, cur(i))),
            pl.BlockSpec((LANES, BLOCK), lambda i: (0, halo(i))),
            pl.BlockSpec((1, B_WIDTH), lambda i: (0, 0)),
            pl.BlockSpec((ts, D_MODEL), lambda i: (prev(i), 0)),
            in_hbm,
            const((1, D_MODEL)),
            in_hbm, in_hbm, in_hbm,
            const((1, D_MODEL)),
        ],
        out_specs=pl.BlockSpec((ts, D_MODEL), lambda i: (last(i), 0)),
        out_shape=jax.ShapeDtypeStruct((t, D_MODEL), F32),
        scratch_shapes=[
            pltpu.VMEM((A_HEADS, A_DV, A_DK), F32),
            pltpu.VMEM((n_chunk, A_HEADS, A_DV, A_DK), F32),
            pltpu.VMEM((n_chunk, A_HEADS, A_DV, A_DK), BF16),
            pltpu.VMEM((2, B_HEADS, 2 * BLOCK, BLOCK), F32),
            pltpu.VMEM((ts, MIX_WIDTH), BF16),
            pltpu.VMEM((ts, MIX_WIDTH), BF16),
            pltpu.VMEM((ts, D_MODEL), F32),
            pltpu.VMEM((ts, D_FF), BF16),
            pltpu.VMEM((ts, D_MODEL), F32),
            pltpu.VMEM((MIX_WIDTH, D_MODEL), BF16),
            pltpu.VMEM((D_MODEL, D_FF), BF16),
            pltpu.VMEM((D_MODEL, D_FF), BF16),
            pltpu.VMEM((D_FF, D_MODEL), BF16),
            pltpu.SemaphoreType.DMA((N_FFN_WEIGHTS,)),
        ],
        compiler_params=pltpu.CompilerParams(
            dimension_semantics=("arbitrary",), vmem_limit_bytes=V7X_VMEM_LIMIT_BYTES),
        name="mixer_ffn",
    )(sinks, qd, kit, ks, hv, gate, dec, bq, kp, kp, vt, vt, attn_norm_g,
      x2d, w_out, ln2_g, w_gate, w_up, w_down, final_g)


def kernel(x, ln1_g, w_in, lb_logits, hgrn_norm_g, attn_sinks, attn_norm_g, w_out, ln2_g,
           w_gate, w_up, w_down, final_g):
    bsz, seq, d = x.shape
    x2d = x.reshape(bsz * seq, d)
    *prep, wo16, wg16, wu16, wd16 = _in_proj(x2d, ln1_g[0:1], w_in[0], lb_logits, hgrn_norm_g[0:1],
                                             w_out[0], w_gate[0], w_up[0], w_down[0])
    out = _mixer_ffn(prep, attn_sinks[0], attn_norm_g[0:1], x2d, wo16, ln2_g[0:1],
                     wg16, wu16, wd16, final_g.reshape(1, d), seq)
    return out.reshape(bsz, seq, d)
```

```python
import functools

import jax
import jax.numpy as jnp
from jax import lax
from jax.experimental import pallas as pl
from jax.experimental.pallas import tpu as pltpu

D_MODEL = 1024
A_HEADS = 4
A_DK = 128
A_DV = 128
A_QK = A_HEADS * A_DK
A_WIDTH = A_HEADS * A_DV
CHUNK = 64
B_HEADS = 8
B_KV_HEADS = 2
B_HEAD_DIM = 64
B_WIDTH = B_HEADS * B_HEAD_DIM
B_KV = B_KV_HEADS * B_HEAD_DIM
WINDOW = 128
BLOCK = 128
MIX_WIDTH = A_WIDTH + B_WIDTH
D_FF = 2816
IN_COLS = 4 * A_QK + B_WIDTH + 2 * B_KV
EPS = 1e-6
NEG_BIG = -1e30
LANES = 128
KEY_VARIANTS = 2 * B_KV_HEADS
N_FFN_WEIGHTS = 4
ROW_QD, ROW_KS, ROW_HV, ROW_GATE, ROW_BQ, ROW_KP = (k * 512 for k in range(6))
ROW_COLS = ROW_KP + KEY_VARIANTS * LANES
COL_KIT, COL_VT = 0, A_QK
COL_ROWS = A_QK + B_KV

F32 = jnp.float32
BF16 = jnp.bfloat16

V7X_VMEM_LIMIT_BYTES = 56 * 1024 * 1024

TM_PROJ = 512
TS_MIX = 512
FF_CHUNK = 256
HGRN_GROUP_CHUNKS = 2
PAIR = 2 * CHUNK
CAST_ROWS = 128
LOG2E = 1.4426950408889634

NT_DIMS = (((1,), (1,)), ((), ()))
TN_DIMS = (((0,), (0,)), ((), ()))


def _rms_scale(v):
    return lax.rsqrt(jnp.mean(v * v, axis=-1, keepdims=True) + EPS)


def _silu(v):
    return v * (1.0 / (1.0 + jnp.exp(-v)))


def _sigmoid(v):
    return 1.0 / (1.0 + jnp.exp(-v))


def _in_proj_kernel(x_ref, g_ref, w32_ref, lbl_ref, hng_ref, wo32_ref, wg32_ref, wu32_ref, wd32_ref,
                    rows_ref, cols_ref, dec_ref,
                    wo16_ref, wg16_ref, wu16_ref, wd16_ref, w_ref, *, n_down_slabs):
    tm = x_ref.shape[0]
    qd_ref, ks_ref, hv_ref, gate_ref, bq_ref = (
        rows_ref.at[:, pl.ds(c, 512)] for c in (ROW_QD, ROW_KS, ROW_HV, ROW_GATE, ROW_BQ))
    kp_refs = [rows_ref.at[:, pl.ds(ROW_KP + k * LANES, LANES)] for k in range(KEY_VARIANTS)]
    kit_ref = cols_ref.at[pl.ds(COL_KIT, A_QK), :]
    vt_ref = cols_ref.at[pl.ds(COL_VT, B_KV), :]

    @pl.when(pl.program_id(0) == 0)
    def _():
        for r0 in range(0, D_MODEL, CAST_ROWS):
            w_ref[r0:r0 + CAST_ROWS, :] = w32_ref[r0:r0 + CAST_ROWS, :].astype(BF16)

    @pl.when(pl.program_id(0) < n_down_slabs)
    def _():
        wd16_ref[...] = wd32_ref[...].astype(BF16)

    x = x_ref[...]
    u = (x * _rms_scale(x) * g_ref[...]).astype(BF16)

    lbl = lbl_ref[...]
    lmax = jnp.max(lbl, axis=0, keepdims=True)
    le = jnp.exp(lbl - lmax)
    lb = le[0:1, :] / jnp.sum(le, axis=0, keepdims=True)

    r = lax.broadcasted_iota(jnp.int32, (CHUNK, CHUNK), 0)
    c = lax.broadcasted_iota(jnp.int32, (CHUNK, CHUNK), 1)
    tri = jnp.where(r >= c, 1.0, 0.0).astype(BF16)

    pf = jnp.dot(u, w_ref[:, A_QK:2 * A_QK], preferred_element_type=F32)
    pq = jnp.dot(u, w_ref[:, 0:A_QK], preferred_element_type=F32)
    k_pair = []
    for ci in range(tm // CHUNK):
        rows = slice(ci * CHUNK, (ci + 1) * CHUNK)
        forget = lb + (1.0 - lb) * _sigmoid(pf[rows])
        key = 1.0 - forget
        logf = jnp.log(forget)
        p0 = logf.astype(BF16)
        r1 = logf - p0.astype(F32)
        p1 = r1.astype(BF16)
        p2 = (r1 - p1.astype(F32)).astype(BF16)
        bc3 = jnp.dot(tri, jnp.concatenate([p0, p1, p2], axis=-1), preferred_element_type=F32)
        bcum = bc3[:, :A_QK] + bc3[:, A_QK:2 * A_QK] + bc3[:, 2 * A_QK:]
        decay = jnp.exp(bcum[CHUNK - 1:CHUNK, :])
        k_intra = key * jnp.exp(-bcum)
        qd_ref[rows, :] = (_silu(pq[rows]) * jnp.exp(bcum)).astype(BF16)
        ks_ref[rows, :] = (k_intra * decay).astype(BF16)
        dec_ref[ci:ci + 1, :] = decay
        k_pair.append(k_intra)
        if len(k_pair) == PAIR // CHUNK:
            p0_row = (ci + 1) * CHUNK - PAIR
            kit_ref[:, p0_row:p0_row + PAIR] = jnp.concatenate(k_pair, axis=0).T.astype(BF16)
            k_pair = []

    wo16_ref[...] = wo32_ref[...].astype(BF16)
    wg16_ref[...] = wg32_ref[...].astype(BF16)
    wu16_ref[...] = wu32_ref[...].astype(BF16)

    c0 = 2 * A_QK + 2 * A_WIDTH
    pa = jnp.dot(u, w_ref[:, c0:], preferred_element_type=F32)
    bq_ref[...] = (pa[:, :B_WIDTH] * (B_HEAD_DIM ** -0.5 * LOG2E)).astype(BF16)
    lo = lax.broadcasted_iota(jnp.int32, (tm, LANES), 1) < B_HEAD_DIM
    k_all = pa[:, B_WIDTH:B_WIDTH + B_KV]
    k_rot = pltpu.roll(k_all, B_HEAD_DIM, axis=1)
    kp_refs[0][...] = jnp.where(lo, k_all, 0.0).astype(BF16)
    kp_refs[1][...] = jnp.where(lo, 0.0, k_rot).astype(BF16)
    kp_refs[2][...] = jnp.where(lo, k_rot, 0.0).astype(BF16)
    kp_refs[3][...] = jnp.where(lo, 0.0, k_all).astype(BF16)
    vt_ref[...] = pa[:, B_WIDTH + B_KV:].T.astype(BF16)

    pg = jnp.dot(u, w_ref[:, 2 * A_QK + A_WIDTH:2 * A_QK + 2 * A_WIDTH], preferred_element_type=F32)
    gate_ref[...] = (_silu(pg) * hng_ref[...]).astype(BF16)
    pi = jnp.dot(u, w_ref[:, 2 * A_QK:2 * A_QK + A_WIDTH], preferred_element_type=F32)
    hv_ref[...] = pi.astype(BF16)


def _in_proj(x2d, ln1_g, w_in, lb_logits, hgrn_norm_g, w_out, w_gate, w_up, w_down):
    t = x2d.shape[0]
    tm = TM_PROJ
    n_steps = t // tm
    row_blk = lambda width: pl.BlockSpec((tm, width), lambda i: (i, 0))
    const = lambda shape, **kw: pl.BlockSpec(shape, lambda i: (0, 0), **kw)
    bf = lambda shape: jax.ShapeDtypeStruct(shape, BF16)

    def slab(w, rows):
        n_slabs = w.shape[0] // rows
        assert w.shape[0] % rows == 0 and n_slabs <= n_steps, (w.shape, rows)
        return pl.BlockSpec((rows, w.shape[1]), lambda i: (jnp.minimum(i, n_slabs - 1), 0))

    cast_weights = (w_out, w_gate, w_up, w_down)
    even_rows = D_MODEL // n_steps
    slab_rows = (even_rows, even_rows, even_rows, CAST_ROWS)
    slab_specs = lambda: [slab(w, rows) for w, rows in zip(cast_weights, slab_rows)]
    return pl.pallas_call(
        functools.partial(_in_proj_kernel, n_down_slabs=D_FF // CAST_ROWS),
        grid=(n_steps,),
        in_specs=[
            row_blk(D_MODEL),
            const((1, D_MODEL)),
            const((D_MODEL, IN_COLS), pipeline_mode=pl.Buffered(1)),
            const((2, A_QK)),
            const((1, A_WIDTH)),
        ] + slab_specs(),
        out_specs=[
            row_blk(ROW_COLS),
            pl.BlockSpec((COL_ROWS, tm), lambda i: (0, i)),
            pl.BlockSpec((tm // CHUNK, A_QK), lambda i: (i, 0)),
        ] + slab_specs(),
        out_shape=[
            bf((t, ROW_COLS)),
            bf((COL_ROWS, t)),
            jax.ShapeDtypeStruct((t // CHUNK, A_QK), F32),
        ] + [bf(w.shape) for w in cast_weights],
        scratch_shapes=[pltpu.VMEM((D_MODEL, IN_COLS), BF16)],
        compiler_params=pltpu.CompilerParams(
            dimension_semantics=("arbitrary",), vmem_limit_bytes=V7X_VMEM_LIMIT_BYTES),
        name="in_proj",
    )(x2d, ln1_g, w_in, lb_logits, hgrn_norm_g, *cast_weights)


def _bias_table_init(bias_ref):
    kj = lax.broadcasted_iota(jnp.int32, (2 * BLOCK, BLOCK), 0)
    qi = lax.broadcasted_iota(jnp.int32, (2 * BLOCK, BLOCK), 1)
    dist = qi + BLOCK - kj
    valid = (dist >= 0) & (dist < WINDOW)
    valid_first = valid & (kj >= BLOCK)
    distf = dist.astype(F32)
    for h in range(B_HEADS):
        slope = 2.0 ** (-(8.0 / B_HEADS) * (h + 1))
        sc = (-slope * LOG2E) * distf
        bias_ref[0, h] = jnp.where(valid, sc, NEG_BIG)
        bias_ref[1, h] = jnp.where(valid_first, sc, NEG_BIG)


def _mixer_stages(first, sinks_ref, qd_ref, kit_ref, ks_ref, hv_ref, gate_ref, dec_ref, bq_ref,
                  kp_refs, kph_refs, vt_ref, vth_ref, ang_ref, y_ref,
                  state_ref, upd_ref, sprev_ref, bias_ref):
    ts = qd_ref.shape[0]
    n_chunk = ts // CHUNK
    n_blk = ts // BLOCK
    head_cols = [slice(h * A_DK, (h + 1) * A_DK) for h in range(A_HEADS)]

    def chunk_units(c_lo, c_hi):
        return [(ci, slice(ci * CHUNK, (ci + 1) * CHUNK), head_cols[h], h)
                for ci in range(c_lo, c_hi) for h in range(A_HEADS)]

    def hgrn_updates(c_lo, c_hi):
        units = chunk_units(c_lo, c_hi)
        upds = [lax.dot_general(hv_ref[rows, cs], ks_ref[rows, cs], TN_DIMS,
                                preferred_element_type=F32) for (_, rows, cs, _) in units]
        for upd, (ci, _, _, h) in zip(upds, units):
            upd_ref[ci, h] = upd

    def hgrn_scan(h):
        st = state_ref[h]
        for ci in range(n_chunk):
            sprev_ref[ci, h] = st.T.astype(BF16)
            st = st * dec_ref[ci:ci + 1, head_cols[h]] + upd_ref[ci, h]
        state_ref[h] = st

    def hgrn_scores(p, box):
        rows = slice(p * PAIR, (p + 1) * PAIR)
        box["a"] = [jnp.dot(qd_ref[rows, cs], kit_ref[cs, rows], preferred_element_type=F32)
                    for cs in head_cols]
        box["o_inter"] = [
            jnp.concatenate(
                [jnp.dot(qd_ref[ci * CHUNK:(ci + 1) * CHUNK, cs], sprev_ref[ci, h],
                         preferred_element_type=F32)
                 for ci in range(p * PAIR // CHUNK, (p + 1) * PAIR // CHUNK)], axis=0)
            for h, cs in enumerate(head_cols)]

    def hgrn_values(p, box):
        rows = slice(p * PAIR, (p + 1) * PAIR)
        r = lax.broadcasted_iota(jnp.int32, (PAIR, PAIR), 0)
        c = lax.broadcasted_iota(jnp.int32, (PAIR, PAIR), 1)
        same_chunk_causal = (c <= r) & (c >= (r // CHUNK) * CHUNK)
        a_s = [jnp.where(same_chunk_causal, a, 0.0).astype(BF16) for a in box["a"]]
        box["o"] = [jnp.dot(a, hv_ref[rows, cs], preferred_element_type=F32) + oi
                    for a, oi, cs in zip(a_s, box["o_inter"], head_cols)]

    def hgrn_store(p, box):
        rows = slice(p * PAIR, (p + 1) * PAIR)
        for o, cs in zip(box["o"], head_cols):
            gate = gate_ref[rows, cs].astype(F32)
            y_ref[rows, cs] = (o * _rms_scale(o) * gate).astype(y_ref.dtype)

    swa_units = [(kvh, par) for kvh in range(B_KV_HEADS) for par in range(2)]

    def swa_scores(j, box):
        s_ts = []
        for kvh, par in swa_units:
            rows = slice(j * BLOCK, (j + 1) * BLOCK)
            q_pairs = jnp.concatenate(
                [bq_ref[rows, (2 * kvh) * LANES:(2 * kvh + 1) * LANES],
                 bq_ref[rows, (2 * kvh + 1) * LANES:(2 * kvh + 2) * LANES]], axis=0)
            idx = 2 * kvh + par
            if j == 0:
                kk = jnp.concatenate([kph_refs[idx][...], kp_refs[idx][0:BLOCK, :]], axis=0)
            else:
                kk = kp_refs[idx][(j - 1) * BLOCK:(j + 1) * BLOCK, :]
            s_ts.append(lax.dot_general(kk, q_pairs, NT_DIMS, preferred_element_type=F32))
        box["s"] = s_ts

    def swa_softmax(j, box):
        p_ts, inv_ls = [], []
        for (kvh, par), s_t in zip(swa_units, box["s"]):
            p_halves = []
            for half, h in enumerate((4 * kvh + par, 4 * kvh + 2 + par)):
                bias = bias_ref[first, h] if j == 0 else bias_ref[0, h]
                s = s_t[:, half * BLOCK:(half + 1) * BLOCK] + bias
                sink = sinks_ref[h] * LOG2E
                m = jnp.maximum(jnp.max(s, axis=0, keepdims=True), sink)
                p = jnp.exp2(s - m)
                inv_ls.append(1.0 / (jnp.sum(p, axis=0, keepdims=True) + jnp.exp2(sink - m)))
                p_halves.append(p.astype(BF16))
            p_ts.append(jnp.concatenate(p_halves, axis=1))
        box["p"], box["inv_l"] = p_ts, inv_ls

    def swa_values(j, box):
        o_t = [None] * B_HEADS
        for ui, ((kvh, par), p_t) in enumerate(zip(swa_units, box["p"])):
            drow = slice(kvh * B_HEAD_DIM, (kvh + 1) * B_HEAD_DIM)
            if j == 0:
                vt_h = jnp.concatenate([vth_ref[drow, :], vt_ref[drow, 0:BLOCK]], axis=1)
            else:
                vt_h = vt_ref[drow, (j - 1) * BLOCK:(j + 1) * BLOCK]
            pv = jnp.dot(vt_h, p_t, preferred_element_type=F32)
            for half, h in enumerate((4 * kvh + par, 4 * kvh + 2 + par)):
                o_t[h] = pv[:, half * BLOCK:(half + 1) * BLOCK] * box["inv_l"][2 * ui + half]
        box["o_t"] = o_t

    def swa_store(j, box):
        o_all_t = jnp.concatenate(box["o_t"], axis=0)
        scale = lax.rsqrt(jnp.sum(o_all_t * o_all_t, axis=0, keepdims=True) * (1.0 / B_WIDTH) + EPS)
        o_all = (o_all_t * scale).T
        y_ref[j * BLOCK:(j + 1) * BLOCK, A_WIDTH:] = (o_all * ang_ref[...]).astype(y_ref.dtype)

    group = HGRN_GROUP_CHUNKS
    hgrn = [functools.partial(hgrn_updates, c, c + group) for c in range(0, n_chunk, group)]
    hgrn += [functools.partial(hgrn_scan, h) for h in range(A_HEADS)]
    for p in range(ts // PAIR):
        box = {}
        hgrn += [functools.partial(f, p, box) for f in (hgrn_scores, hgrn_values, hgrn_store)]
    swa = []
    for j in range(n_blk):
        box = {}
        swa += [functools.partial(f, j, box) for f in (swa_scores, swa_softmax, swa_values, swa_store)]
    stages = []
    for k in range(max(len(hgrn), len(swa))):
        stages += hgrn[k:k + 1] + swa[k:k + 1]
    return stages


def _ffn_stages(x_ref, y_ref, wo_ref, g2_ref, wg_ref, wu_ref, wd_ref, h_ref, act_ref, h2_ref,
                before_down=()):
    box = {}

    def head():
        h = x_ref[...] + jnp.dot(y_ref[...], wo_ref[...], preferred_element_type=F32)
        box["u"] = (h * _rms_scale(h) * g2_ref[...]).astype(BF16)
        h_ref[...] = h

    def ff_up(c0):
        cs = slice(c0, c0 + FF_CHUNK)
        box["gate"] = jnp.dot(box["u"], wg_ref[:, cs], preferred_element_type=F32)
        box["up"] = jnp.dot(box["u"], wu_ref[:, cs], preferred_element_type=F32)

    def ff_act(c0):
        act_ref[:, c0:c0 + FF_CHUNK] = (_silu(box["gate"]) * box["up"]).astype(BF16)

    def down():
        h2_ref[...] = h_ref[...] + jnp.dot(act_ref[...], wd_ref[...], preferred_element_type=F32)

    stages = [head]
    for c0 in range(0, D_FF, FF_CHUNK):
        stages += [functools.partial(ff_up, c0), functools.partial(ff_act, c0)]
    return stages + list(before_down) + [down]


def _final_norm(h2_ref, gf_ref, o_ref):
    h2 = h2_ref[...]
    o_ref[...] = h2 * _rms_scale(h2) * gf_ref[...]


def _run_interleaved(major, minor):
    span = max(len(major) - 2, 1)
    done = 0
    for k, stage in enumerate(major):
        stage()
        want = min(len(minor), -(-(k + 1) * len(minor) // span))
        while done < want:
            minor[done]()
            done += 1


def _mixer_ffn_kernel(sinks_ref, rows_ref, kph_ref, cols_ref, vth_ref, dec_ref, ang_ref,
                      x_ref, wo_hbm, g2_ref, wg_hbm, wu_hbm, wd_hbm, gf_ref, o_ref,
                      state_ref, upd_ref, sprev_ref, bias_ref, y_cur, y_prev, h_ref, act_ref, h2_ref,
                      wo_ref, wg_ref, wu_ref, wd_ref, w_sems, *, n_tiles, n_s):
    i = pl.program_id(0)
    seq_start = lax.rem(i, n_s) == 0
    weight_copies = [
        pltpu.make_async_copy(src, dst, w_sems.at[k])
        for k, (src, dst) in enumerate(((wo_hbm, wo_ref), (wg_hbm, wg_ref),
                                        (wu_hbm, wu_ref), (wd_hbm, wd_ref)))]

    @pl.when(i == 0)
    def _():
        for copy in weight_copies:
            copy.start()
        _bias_table_init(bias_ref)

    @pl.when(i == 1)
    def _():
        for copy in weight_copies:
            copy.wait()

    @pl.when(seq_start)
    def _():
        state_ref[...] = jnp.zeros_like(state_ref)

    first = jnp.where(seq_start, 1, 0)
    qd_ref, ks_ref, hv_ref, gate_ref, bq_ref = (
        rows_ref.at[:, pl.ds(c, 512)] for c in (ROW_QD, ROW_KS, ROW_HV, ROW_GATE, ROW_BQ))
    kp_refs = [rows_ref.at[:, pl.ds(ROW_KP + k * LANES, LANES)] for k in range(KEY_VARIANTS)]
    kph_refs = [kph_ref.at[:, pl.ds(k * LANES, LANES)] for k in range(KEY_VARIANTS)]
    kit_ref = cols_ref.at[pl.ds(COL_KIT, A_QK), :]
    vt_ref = cols_ref.at[pl.ds(COL_VT, B_KV), :]
    mixer_args = (first, sinks_ref, qd_ref, kit_ref, ks_ref, hv_ref, gate_ref, dec_ref, bq_ref,
                  kp_refs, kph_refs, vt_ref, vth_ref, ang_ref, y_cur,
                  state_ref, upd_ref, sprev_ref, bias_ref)
    ffn_args = (x_ref, y_prev, wo_ref, g2_ref, wg_ref, wu_ref, wd_ref, h_ref, act_ref, h2_ref)
    final_norm = functools.partial(_final_norm, h2_ref, gf_ref, o_ref)

    @pl.when(i == 0)
    def _():
        _run_interleaved(_mixer_stages(*mixer_args), [])
        y_prev[...] = y_cur[...]

    @pl.when(i == 1)
    def _():
        _run_interleaved(_ffn_stages(*ffn_args), _mixer_stages(*mixer_args))
        y_prev[...] = y_cur[...]

    @pl.when((i > 1) & (i < n_tiles))
    def _():
        _run_interleaved(_ffn_stages(*ffn_args, before_down=[final_norm]),
                         _mixer_stages(*mixer_args))
        y_prev[...] = y_cur[...]

    @pl.when(i == n_tiles)
    def _():
        _run_interleaved(_ffn_stages(*ffn_args, before_down=[final_norm]), [])

    @pl.when(i == n_tiles + 1)
    def _():
        final_norm()


def _mixer_ffn(prep, sinks, attn_norm_g, x2d, w_out, ln2_g, w_gate, w_up, w_down, final_g, seq):
    row_pack, col_pack, dec = prep
    t = x2d.shape[0]
    ts = TS_MIX
    n_tiles = t // ts
    n_s = seq // ts
    n_chunk = ts // CHUNK
    n_blk = ts // BLOCK
    cur = lambda i: jnp.minimum(i, n_tiles - 1)
    prev = lambda i: jnp.clip(i - 1, 0, n_tiles - 1)
    last = lambda i: jnp.clip(i - 2, 0, n_tiles - 1)
    halo = lambda i: jnp.maximum(cur(i) * n_blk - 1, 0)
    row_blk = lambda width: pl.BlockSpec((ts, width), lambda i: (cur(i), 0))
    const = lambda shape: pl.BlockSpec(shape, lambda i: (0, 0), pipeline_mode=pl.Buffered(1))
    in_hbm = pl.BlockSpec(memory_space=pl.ANY)
    assert n_tiles >= 2
    return pl.pallas_call(
        functools.partial(_mixer_ffn_kernel, n_tiles=n_tiles, n_s=n_s),
        grid=(n_tiles + 2,),
        in_specs=[
            pl.BlockSpec(memory_space=pltpu.SMEM),
            row_blk(ROW_COLS),
            pl.BlockSpec((BLOCK, KEY_VARIANTS * LANES),
                         lambda i: (halo(i), ROW_KP // (KEY_VARIANTS * LANES))),
            pl.BlockSpec((COL_ROWS, ts), lambda i: (0, cur(i))),
            pl.BlockSpec((B_KV, BLOCK), lambda i: (COL_VT // B_KV, halo(i))),
            pl.BlockSpec((n_chunk, A_QK), lambda i: (cur(i), 0)),
            pl.BlockSpec((1, B_WIDTH), lambda i: (0, 0)),
            pl.BlockSpec((ts, D_MODEL), lambda i: (prev(i), 0)),
            in_hbm,
            const((1, D_MODEL)),
            in_hbm, in_hbm, in_hbm,
            const((1, D_MODEL)),
        ],
        out_specs=pl.BlockSpec((ts, D_MODEL), lambda i: (last(i), 0)),
        out_shape=jax.ShapeDtypeStruct((t, D_MODEL), F32),
        scratch_shapes=[
            pltpu.VMEM((A_HEADS, A_DV, A_DK), F32),
            pltpu.VMEM((n_chunk, A_HEADS, A_DV, A_DK), F32),
            pltpu.VMEM((n_chunk, A_HEADS, A_DV, A_DK), BF16),
            pltpu.VMEM((2, B_HEADS, 2 * BLOCK, BLOCK), F32),
            pltpu.VMEM((ts, MIX_WIDTH), BF16),
            pltpu.VMEM((ts, MIX_WIDTH), BF16),
            pltpu.VMEM((ts, D_MODEL), F32),
            pltpu.VMEM((ts, D_FF), BF16),
            pltpu.VMEM((ts, D_MODEL), F32),
            pltpu.VMEM((MIX_WIDTH, D_MODEL), BF16),
            pltpu.VMEM((D_MODEL, D_FF), BF16),
            pltpu.VMEM((D_MODEL, D_FF), BF16),
            pltpu.VMEM((D_FF, D_MODEL), BF16),
            pltpu.SemaphoreType.DMA((N_FFN_WEIGHTS,)),
        ],
        compiler_params=pltpu.CompilerParams(
            dimension_semantics=("arbitrary",), vmem_limit_bytes=V7X_VMEM_LIMIT_BYTES),
        name="mixer_ffn",
    )(sinks, row_pack, row_pack, col_pack, col_pack, dec, attn_norm_g,
      x2d, w_out, ln2_g, w_gate, w_up, w_down, final_g)


def kernel(x, ln1_g, w_in, lb_logits, hgrn_norm_g, attn_sinks, attn_norm_g, w_out, ln2_g,
           w_gate, w_up, w_down, final_g):
    bsz, seq, d = x.shape
    x2d = x.reshape(bsz * seq, d)
    *prep, wo16, wg16, wu16, wd16 = _in_proj(x2d, ln1_g[0:1], w_in[0], lb_logits, hgrn_norm_g[0:1],
                                             w_out[0], w_gate[0], w_up[0], w_down[0])
    out = _mixer_ffn(prep, attn_sinks[0], attn_norm_g[0:1], x2d, wo16, ln2_g[0:1],
                     wg16, wu16, wd16, final_g.reshape(1, d), seq)
    return out.reshape(bsz, seq, d)
```

```python
import functools

import jax
import jax.numpy as jnp
from jax import lax
from jax.experimental import pallas as pl
from jax.experimental.pallas import tpu as pltpu

D_MODEL = 1024
A_HEADS = 4
A_DK = 128
A_DV = 128
A_QK = A_HEADS * A_DK
A_WIDTH = A_HEADS * A_DV
CHUNK = 64
B_HEADS = 8
B_KV_HEADS = 2
B_HEAD_DIM = 64
B_WIDTH = B_HEADS * B_HEAD_DIM
B_KV = B_KV_HEADS * B_HEAD_DIM
WINDOW = 128
BLOCK = 128
MIX_WIDTH = A_WIDTH + B_WIDTH
D_FF = 2816
IN_COLS = 4 * A_QK + B_WIDTH + 2 * B_KV
EPS = 1e-6
NEG_BIG = -1e30
LANES = 128
KEY_VARIANTS = 2 * B_KV_HEADS
N_FFN_WEIGHTS = 4

F32 = jnp.float32
BF16 = jnp.bfloat16

V7X_VMEM_LIMIT_BYTES = 56 * 1024 * 1024

TM_PROJ = 512
TS_MIX = 512
FF_CHUNK = 256
HGRN_GROUP_CHUNKS = 2
PAIR = 2 * CHUNK
CAST_ROWS = 128
LOG2E = 1.4426950408889634

NT_DIMS = (((1,), (1,)), ((), ()))
TN_DIMS = (((0,), (0,)), ((), ()))


def _rms_scale(v):
    return lax.rsqrt(jnp.mean(v * v, axis=-1, keepdims=True) + EPS)


def _silu(v):
    return v * (1.0 / (1.0 + jnp.exp(-v)))


def _sigmoid(v):
    return 1.0 / (1.0 + jnp.exp(-v))


def _in_proj_step(fetch, cast_down_slab, x_ref, g_ref, lbl_ref, hng_ref, wo32_ref, wg32_ref, wu32_ref,
                  qd_ref, kit_ref, ks_ref, hv_ref, gate_ref, dec_ref, bq_ref, kp_ref, vt_ref,
                  wo16_ref, wg16_ref, wu16_ref, w_ref):
    tm = x_ref.shape[0]
    cast_down_slab()

    x = x_ref[...]
    u = (x * _rms_scale(x) * g_ref[...]).astype(BF16)

    lbl = lbl_ref[...]
    lmax = jnp.max(lbl, axis=0, keepdims=True)
    le = jnp.exp(lbl - lmax)
    lb = le[0:1, :] / jnp.sum(le, axis=0, keepdims=True)

    r = lax.broadcasted_iota(jnp.int32, (CHUNK, CHUNK), 0)
    c = lax.broadcasted_iota(jnp.int32, (CHUNK, CHUNK), 1)
    tri = jnp.where(r >= c, 1.0, 0.0).astype(BF16)

    fetch(0)
    pf = jnp.dot(u, w_ref[:, A_QK:2 * A_QK], preferred_element_type=F32)
    fetch(1)
    pq = jnp.dot(u, w_ref[:, 0:A_QK], preferred_element_type=F32)
    k_pair = []
    for ci in range(tm // CHUNK):
        rows = slice(ci * CHUNK, (ci + 1) * CHUNK)
        forget = lb + (1.0 - lb) * _sigmoid(pf[rows])
        key = 1.0 - forget
        logf = jnp.log(forget)
        p0 = logf.astype(BF16)
        r1 = logf - p0.astype(F32)
        p1 = r1.astype(BF16)
        p2 = (r1 - p1.astype(F32)).astype(BF16)
        bc3 = jnp.dot(tri, jnp.concatenate([p0, p1, p2], axis=-1), preferred_element_type=F32)
        bcum = bc3[:, :A_QK] + bc3[:, A_QK:2 * A_QK] + bc3[:, 2 * A_QK:]
        decay = jnp.exp(bcum[CHUNK - 1:CHUNK, :])
        k_intra = key * jnp.exp(-bcum)
        qd_ref[rows, :] = (_silu(pq[rows]) * jnp.exp(bcum)).astype(BF16)
        ks_ref[rows, :] = (k_intra * decay).astype(BF16)
        dec_ref[ci:ci + 1, :] = decay
        k_pair.append(k_intra)
        if len(k_pair) == PAIR // CHUNK:
            p0_row = (ci + 1) * CHUNK - PAIR
            kit_ref[:, p0_row:p0_row + PAIR] = jnp.concatenate(k_pair, axis=0).T.astype(BF16)
            k_pair = []

    wo16_ref[...] = wo32_ref[...].astype(BF16)
    wg16_ref[...] = wg32_ref[...].astype(BF16)
    wu16_ref[...] = wu32_ref[...].astype(BF16)

    c0 = 2 * A_QK + 2 * A_WIDTH
    fetch(2)
    pa = jnp.dot(u, w_ref[:, c0:], preferred_element_type=F32)
    bq_ref[...] = (pa[:, :B_WIDTH] * (B_HEAD_DIM ** -0.5 * LOG2E)).astype(BF16)
    lo = lax.broadcasted_iota(jnp.int32, (tm, LANES), 1) < B_HEAD_DIM
    k_all = pa[:, B_WIDTH:B_WIDTH + B_KV]
    k_rot = pltpu.roll(k_all, B_HEAD_DIM, axis=1)
    kp_ref[0] = jnp.where(lo, k_all, 0.0).astype(BF16)
    kp_ref[1] = jnp.where(lo, 0.0, k_rot).astype(BF16)
    kp_ref[2] = jnp.where(lo, k_rot, 0.0).astype(BF16)
    kp_ref[3] = jnp.where(lo, 0.0, k_all).astype(BF16)
    vt_ref[...] = pa[:, B_WIDTH + B_KV:].T.astype(BF16)

    fetch(3)
    pg = jnp.dot(u, w_ref[:, 2 * A_QK + A_WIDTH:2 * A_QK + 2 * A_WIDTH], preferred_element_type=F32)
    gate_ref[...] = (_silu(pg) * hng_ref[...]).astype(BF16)
    fetch(4)
    pi = jnp.dot(u, w_ref[:, 2 * A_QK:2 * A_QK + A_WIDTH], preferred_element_type=F32)
    hv_ref[...] = pi.astype(BF16)


W_IN_GROUPS = ((A_QK, A_QK), (0, A_QK), (2 * A_QK + 2 * A_WIDTH, B_WIDTH + 2 * B_KV),
               (2 * A_QK + A_WIDTH, A_WIDTH), (2 * A_QK, A_WIDTH))


def _in_proj_kernel(x_ref, g_ref, w32_hbm, lbl_ref, hng_ref, wo32_ref, wg32_ref, wu32_ref, wd32_ref,
                    qd_ref, kit_ref, ks_ref, hv_ref, gate_ref, dec_ref, bq_ref, kp_ref, vt_ref,
                    wo16_ref, wg16_ref, wu16_ref, wd16_ref, w_ref, w32_ref, w_sems, *, n_down_slabs):
    step = pl.program_id(0)
    copies = [pltpu.make_async_copy(w32_hbm.at[:, pl.ds(c0, n)], w32_ref.at[:, pl.ds(c0, n)],
                                    w_sems.at[k]) for k, (c0, n) in enumerate(W_IN_GROUPS)]

    def fetch(k):
        c0, n = W_IN_GROUPS[k]
        copies[k].wait()
        for r0 in range(0, D_MODEL, CAST_ROWS):
            w_ref[r0:r0 + CAST_ROWS, c0:c0 + n] = w32_ref[r0:r0 + CAST_ROWS, c0:c0 + n].astype(BF16)

    def cast_down_slab():
        wd16_ref[...] = wd32_ref[...].astype(BF16)

    refs = (x_ref, g_ref, lbl_ref, hng_ref, wo32_ref, wg32_ref, wu32_ref,
            qd_ref, kit_ref, ks_ref, hv_ref, gate_ref, dec_ref, bq_ref, kp_ref, vt_ref,
            wo16_ref, wg16_ref, wu16_ref, w_ref)

    @pl.when(step == 0)
    def _():
        for copy in copies:
            copy.start()
        _in_proj_step(fetch, cast_down_slab, *refs)

    @pl.when(step != 0)
    def _():
        _in_proj_step(lambda k: None, lambda: pl.when(step < n_down_slabs)(cast_down_slab), *refs)


def _in_proj(x2d, ln1_g, w_in, lb_logits, hgrn_norm_g, w_out, w_gate, w_up, w_down):
    t = x2d.shape[0]
    tm = TM_PROJ
    n_steps = t // tm
    row_blk = lambda width: pl.BlockSpec((tm, width), lambda i: (i, 0))
    const = lambda shape, **kw: pl.BlockSpec(shape, lambda i: (0, 0), **kw)
    bf = lambda shape: jax.ShapeDtypeStruct(shape, BF16)

    def slab(w, rows):
        n_slabs = w.shape[0] // rows
        assert w.shape[0] % rows == 0 and n_slabs <= n_steps, (w.shape, rows)
        return pl.BlockSpec((rows, w.shape[1]), lambda i: (jnp.minimum(i, n_slabs - 1), 0))

    cast_weights = (w_out, w_gate, w_up, w_down)
    even_rows = D_MODEL // n_steps
    slab_rows = (even_rows, even_rows, even_rows, CAST_ROWS)
    slab_specs = lambda: [slab(w, rows) for w, rows in zip(cast_weights, slab_rows)]
    return pl.pallas_call(
        functools.partial(_in_proj_kernel, n_down_slabs=D_FF // CAST_ROWS),
        grid=(n_steps,),
        in_specs=[
            row_blk(D_MODEL),
            const((1, D_MODEL)),
            pl.BlockSpec(memory_space=pl.ANY),
            const((2, A_QK)),
            const((1, A_WIDTH)),
        ] + slab_specs(),
        out_specs=[
            row_blk(A_QK), pl.BlockSpec((A_QK, tm), lambda i: (0, i)), row_blk(A_QK),
            row_blk(A_WIDTH), row_blk(A_WIDTH),
            pl.BlockSpec((tm // CHUNK, A_QK), lambda i: (i, 0)),
            row_blk(B_WIDTH),
            pl.BlockSpec((KEY_VARIANTS, tm, LANES), lambda i: (0, i, 0)),
            pl.BlockSpec((LANES, tm), lambda i: (0, i)),
        ] + slab_specs(),
        out_shape=[
            bf((t, A_QK)), bf((A_QK, t)), bf((t, A_QK)), bf((t, A_WIDTH)), bf((t, A_WIDTH)),
            jax.ShapeDtypeStruct((t // CHUNK, A_QK), F32),
            bf((t, B_WIDTH)),
            bf((KEY_VARIANTS, t, LANES)),
            bf((LANES, t)),
        ] + [bf(w.shape) for w in cast_weights],
        scratch_shapes=[pltpu.VMEM((D_MODEL, IN_COLS), BF16),
                        pltpu.VMEM((D_MODEL, IN_COLS), F32),
                        pltpu.SemaphoreType.DMA((len(W_IN_GROUPS),))],
        compiler_params=pltpu.CompilerParams(
            dimension_semantics=("arbitrary",), vmem_limit_bytes=V7X_VMEM_LIMIT_BYTES),
        name="in_proj",
    )(x2d, ln1_g, w_in, lb_logits, hgrn_norm_g, *cast_weights)


def _bias_table_init(bias_ref):
    kj = lax.broadcasted_iota(jnp.int32, (2 * BLOCK, BLOCK), 0)
    qi = lax.broadcasted_iota(jnp.int32, (2 * BLOCK, BLOCK), 1)
    dist = qi + BLOCK - kj
    valid = (dist >= 0) & (dist < WINDOW)
    valid_first = valid & (kj >= BLOCK)
    distf = dist.astype(F32)
    for h in range(B_HEADS):
        slope = 2.0 ** (-(8.0 / B_HEADS) * (h + 1))
        sc = (-slope * LOG2E) * distf
        bias_ref[0, h] = jnp.where(valid, sc, NEG_BIG)
        bias_ref[1, h] = jnp.where(valid_first, sc, NEG_BIG)


def _mixer_stages(first, sinks_ref, qd_ref, kit_ref, ks_ref, hv_ref, gate_ref, dec_ref, bq_ref,
                  kp_ref, kph_ref, vt_ref, vth_ref, ang_ref, y_ref,
                  state_ref, upd_ref, sprev_ref, bias_ref):
    ts = qd_ref.shape[0]
    n_chunk = ts // CHUNK
    n_blk = ts // BLOCK
    head_cols = [slice(h * A_DK, (h + 1) * A_DK) for h in range(A_HEADS)]

    def chunk_units(c_lo, c_hi):
        return [(ci, slice(ci * CHUNK, (ci + 1) * CHUNK), head_cols[h], h)
                for ci in range(c_lo, c_hi) for h in range(A_HEADS)]

    def hgrn_updates(c_lo, c_hi):
        units = chunk_units(c_lo, c_hi)
        upds = [lax.dot_general(hv_ref[rows, cs], ks_ref[rows, cs], TN_DIMS,
                                preferred_element_type=F32) for (_, rows, cs, _) in units]
        for upd, (ci, _, _, h) in zip(upds, units):
            upd_ref[ci, h] = upd

    def hgrn_scan(h):
        st = state_ref[h]
        for ci in range(n_chunk):
            sprev_ref[ci, h] = st.T.astype(BF16)
            st = st * dec_ref[ci:ci + 1, head_cols[h]] + upd_ref[ci, h]
        state_ref[h] = st

    def hgrn_scores(p, box):
        rows = slice(p * PAIR, (p + 1) * PAIR)
        box["a"] = [jnp.dot(qd_ref[rows, cs], kit_ref[cs, rows], preferred_element_type=F32)
                    for cs in head_cols]
        box["o_inter"] = [
            jnp.concatenate(
                [jnp.dot(qd_ref[ci * CHUNK:(ci + 1) * CHUNK, cs], sprev_ref[ci, h],
                         preferred_element_type=F32)
                 for ci in range(p * PAIR // CHUNK, (p + 1) * PAIR // CHUNK)], axis=0)
            for h, cs in enumerate(head_cols)]

    def hgrn_values(p, box):
        rows = slice(p * PAIR, (p + 1) * PAIR)
        r = lax.broadcasted_iota(jnp.int32, (PAIR, PAIR), 0)
        c = lax.broadcasted_iota(jnp.int32, (PAIR, PAIR), 1)
        same_chunk_causal = (c <= r) & (c >= (r // CHUNK) * CHUNK)
        a_s = [jnp.where(same_chunk_causal, a, 0.0).astype(BF16) for a in box["a"]]
        box["o"] = [jnp.dot(a, hv_ref[rows, cs], preferred_element_type=F32) + oi
                    for a, oi, cs in zip(a_s, box["o_inter"], head_cols)]

    def hgrn_store(p, box):
        rows = slice(p * PAIR, (p + 1) * PAIR)
        for o, cs in zip(box["o"], head_cols):
            gate = gate_ref[rows, cs].astype(F32)
            y_ref[rows, cs] = (o * _rms_scale(o) * gate).astype(y_ref.dtype)

    swa_units = [(kvh, par) for kvh in range(B_KV_HEADS) for par in range(2)]

    def swa_scores(j, box):
        s_ts = []
        for kvh, par in swa_units:
            rows = slice(j * BLOCK, (j + 1) * BLOCK)
            q_pairs = jnp.concatenate(
                [bq_ref[rows, (2 * kvh) * LANES:(2 * kvh + 1) * LANES],
                 bq_ref[rows, (2 * kvh + 1) * LANES:(2 * kvh + 2) * LANES]], axis=0)
            idx = 2 * kvh + par
            if j == 0:
                kk = jnp.concatenate([kph_ref[idx], kp_ref[idx, 0:BLOCK, :]], axis=0)
            else:
                kk = kp_ref[idx, (j - 1) * BLOCK:(j + 1) * BLOCK, :]
            s_ts.append(lax.dot_general(kk, q_pairs, NT_DIMS, preferred_element_type=F32))
        box["s"] = s_ts

    def swa_softmax(j, box):
        p_ts, inv_ls = [], []
        for (kvh, par), s_t in zip(swa_units, box["s"]):
            p_halves = []
            for half, h in enumerate((4 * kvh + par, 4 * kvh + 2 + par)):
                bias = bias_ref[first, h] if j == 0 else bias_ref[0, h]
                s = s_t[:, half * BLOCK:(half + 1) * BLOCK] + bias
                sink = sinks_ref[h] * LOG2E
                m = jnp.maximum(jnp.max(s, axis=0, keepdims=True), sink)
                p = jnp.exp2(s - m)
                inv_ls.append(1.0 / (jnp.sum(p, axis=0, keepdims=True) + jnp.exp2(sink - m)))
                p_halves.append(p.astype(BF16))
            p_ts.append(jnp.concatenate(p_halves, axis=1))
        box["p"], box["inv_l"] = p_ts, inv_ls

    def swa_values(j, box):
        o_t = [None] * B_HEADS
        for ui, ((kvh, par), p_t) in enumerate(zip(swa_units, box["p"])):
            drow = slice(kvh * B_HEAD_DIM, (kvh + 1) * B_HEAD_DIM)
            if j == 0:
                vt_h = jnp.concatenate([vth_ref[drow, :], vt_ref[drow, 0:BLOCK]], axis=1)
            else:
                vt_h = vt_ref[drow, (j - 1) * BLOCK:(j + 1) * BLOCK]
            pv = jnp.dot(vt_h, p_t, preferred_element_type=F32)
            for half, h in enumerate((4 * kvh + par, 4 * kvh + 2 + par)):
                o_t[h] = pv[:, half * BLOCK:(half + 1) * BLOCK] * box["inv_l"][2 * ui + half]
        box["o_t"] = o_t

    def swa_store(j, box):
        o_all_t = jnp.concatenate(box["o_t"], axis=0)
        scale = lax.rsqrt(jnp.sum(o_all_t * o_all_t, axis=0, keepdims=True) * (1.0 / B_WIDTH) + EPS)
        o_all = (o_all_t * scale).T
        y_ref[j * BLOCK:(j + 1) * BLOCK, A_WIDTH:] = (o_all * ang_ref[...]).astype(y_ref.dtype)

    group = HGRN_GROUP_CHUNKS
    hgrn = [functools.partial(hgrn_updates, c, c + group) for c in range(0, n_chunk, group)]
    hgrn += [functools.partial(hgrn_scan, h) for h in range(A_HEADS)]
    for p in range(ts // PAIR):
        box = {}
        hgrn += [functools.partial(f, p, box) for f in (hgrn_scores, hgrn_values, hgrn_store)]
    swa = []
    for j in range(n_blk):
        box = {}
        swa += [functools.partial(f, j, box) for f in (swa_scores, swa_softmax, swa_values, swa_store)]
    stages = []
    for k in range(max(len(hgrn), len(swa))):
        stages += hgrn[k:k + 1] + swa[k:k + 1]
    return stages


def _ffn_stages(x_ref, y_ref, wo_ref, g2_ref, wg_ref, wu_ref, wd_ref, h_ref, act_ref, h2_ref,
                before_up=(), before_down=()):
    box = {}

    def head():
        h = x_ref[...] + jnp.dot(y_ref[...], wo_ref[...], preferred_element_type=F32)
        box["u"] = (h * _rms_scale(h) * g2_ref[...]).astype(BF16)
        h_ref[...] = h

    def ff_up(c0):
        cs = slice(c0, c0 + FF_CHUNK)
        box["gate"] = jnp.dot(box["u"], wg_ref[:, cs], preferred_element_type=F32)
        box["up"] = jnp.dot(box["u"], wu_ref[:, cs], preferred_element_type=F32)

    def ff_act(c0):
        act_ref[:, c0:c0 + FF_CHUNK] = (_silu(box["gate"]) * box["up"]).astype(BF16)

    def down():
        h2_ref[...] = h_ref[...] + jnp.dot(act_ref[...], wd_ref[...], preferred_element_type=F32)

    stages = [head] + list(before_up)
    for c0 in range(0, D_FF, FF_CHUNK):
        stages += [functools.partial(ff_up, c0), functools.partial(ff_act, c0)]
    return stages + list(before_down) + [down]


def _final_norm(h2_ref, gf_ref, o_ref):
    h2 = h2_ref[...]
    o_ref[...] = h2 * _rms_scale(h2) * gf_ref[...]


def _run_interleaved(major, minor):
    span = max(len(major) - 2, 1)
    done = 0
    for k, stage in enumerate(major):
        stage()
        want = min(len(minor), -(-(k + 1) * len(minor) // span))
        while done < want:
            minor[done]()
            done += 1


def _mixer_ffn_kernel(sinks_ref, qd_ref, kit_ref, ks_ref, hv_ref, gate_ref, dec_ref, bq_ref,
                      kp_ref, kph_ref, vt_ref, vth_ref, ang_ref,
                      x_ref, wo_hbm, g2_ref, wg_hbm, wu_hbm, wd_hbm, gf_ref, o_ref,
                      state_ref, upd_ref, sprev_ref, bias_ref, y_cur, y_prev, h_ref, act_ref, h2_ref,
                      wo_ref, wg_ref, wu_ref, wd_ref, w_sems, *, n_tiles, n_s):
    i = pl.program_id(0)
    seq_start = lax.rem(i, n_s) == 0
    weight_copies = [
        pltpu.make_async_copy(src, dst, w_sems.at[k])
        for k, (src, dst) in enumerate(((wo_hbm, wo_ref), (wg_hbm, wg_ref),
                                        (wu_hbm, wu_ref), (wd_hbm, wd_ref)))]

    @pl.when(i == 0)
    def _():
        for copy in weight_copies:
            copy.start()
        _bias_table_init(bias_ref)

    @pl.when(seq_start)
    def _():
        state_ref[...] = jnp.zeros_like(state_ref)

    first = jnp.where(seq_start, 1, 0)
    mixer_args = (first, sinks_ref, qd_ref, kit_ref, ks_ref, hv_ref, gate_ref, dec_ref, bq_ref,
                  kp_ref, kph_ref, vt_ref, vth_ref, ang_ref, y_cur,
                  state_ref, upd_ref, sprev_ref, bias_ref)
    ffn_args = (x_ref, y_prev, wo_ref, g2_ref, wg_ref, wu_ref, wd_ref, h_ref, act_ref, h2_ref)
    final_norm = functools.partial(_final_norm, h2_ref, gf_ref, o_ref)

    @pl.when(i == 0)
    def _():
        _run_interleaved(_mixer_stages(*mixer_args), [])
        y_prev[...] = y_cur[...]

    @pl.when(i == 1)
    def _():
        wo_copy, wg_copy, wu_copy, wd_copy = weight_copies
        wo_copy.wait()
        _run_interleaved(_ffn_stages(*ffn_args, before_up=[wg_copy.wait, wu_copy.wait],
                                     before_down=[wd_copy.wait]),
                         _mixer_stages(*mixer_args))
        y_prev[...] = y_cur[...]

    @pl.when((i > 1) & (i < n_tiles))
    def _():
        _run_interleaved(_ffn_stages(*ffn_args, before_down=[final_norm]),
                         _mixer_stages(*mixer_args))
        y_prev[...] = y_cur[...]

    @pl.when(i == n_tiles)
    def _():
        _run_interleaved(_ffn_stages(*ffn_args, before_down=[final_norm]), [])

    @pl.when(i == n_tiles + 1)
    def _():
        final_norm()


def _mixer_ffn(prep, sinks, attn_norm_g, x2d, w_out, ln2_g, w_gate, w_up, w_down, final_g, seq):
    qd, kit, ks, hv, gate, dec, bq, kp, vt = prep
    t = x2d.shape[0]
    ts = TS_MIX
    n_tiles = t // ts
    n_s = seq // ts
    n_chunk = ts // CHUNK
    n_blk = ts // BLOCK
    cur = lambda i: jnp.minimum(i, n_tiles - 1)
    prev = lambda i: jnp.clip(i - 1, 0, n_tiles - 1)
    last = lambda i: jnp.clip(i - 2, 0, n_tiles - 1)
    halo = lambda i: jnp.maximum(cur(i) * n_blk - 1, 0)
    row_blk = lambda width: pl.BlockSpec((ts, width), lambda i: (cur(i), 0))
    const = lambda shape: pl.BlockSpec(shape, lambda i: (0, 0), pipeline_mode=pl.Buffered(1))
    in_hbm = pl.BlockSpec(memory_space=pl.ANY)
    assert n_tiles >= 2
    return pl.pallas_call(
        functools.partial(_mixer_ffn_kernel, n_tiles=n_tiles, n_s=n_s),
        grid=(n_tiles + 2,),
        in_specs=[
            pl.BlockSpec(memory_space=pltpu.SMEM),
            row_blk(A_QK), pl.BlockSpec((A_QK, ts), lambda i: (0, cur(i))), row_blk(A_QK),
            row_blk(A_WIDTH), row_blk(A_WIDTH),
            pl.BlockSpec((n_chunk, A_QK), lambda i: (cur(i), 0)),
            row_blk(B_WIDTH),
            pl.BlockSpec((KEY_VARIANTS, ts, LANES), lambda i: (0, cur(i), 0)),
            pl.BlockSpec((KEY_VARIANTS, BLOCK, LANES), lambda i: (0, halo(i), 0)),
            pl.BlockSpec((LANES, ts), lambda i: (0, cur(i))),
            pl.BlockSpec((LANES, BLOCK), lambda i: (0, halo(i))),
            pl.BlockSpec((1, B_WIDTH), lambda i: (0, 0)),
            pl.BlockSpec((ts, D_MODEL), lambda i: (prev(i), 0)),
            in_hbm,
            const((1, D_MODEL)),
            in_hbm, in_hbm, in_hbm,
            const((1, D_MODEL)),
        ],
        out_specs=pl.BlockSpec((ts, D_MODEL), lambda i: (last(i), 0)),
        out_shape=jax.ShapeDtypeStruct((t, D_MODEL), F32),
        scratch_shapes=[
            pltpu.VMEM((A_HEADS, A_DV, A_DK), F32),
            pltpu.VMEM((n_chunk, A_HEADS, A_DV, A_DK), F32),
            pltpu.VMEM((n_chunk, A_HEADS, A_DV, A_DK), BF16),
            pltpu.VMEM((2, B_HEADS, 2 * BLOCK, BLOCK), F32),
            pltpu.VMEM((ts, MIX_WIDTH), BF16),
            pltpu.VMEM((ts, MIX_WIDTH), BF16),
            pltpu.VMEM((ts, D_MODEL), F32),
            pltpu.VMEM((ts, D_FF), BF16),
            pltpu.VMEM((ts, D_MODEL), F32),
            pltpu.VMEM((MIX_WIDTH, D_MODEL), BF16),
            pltpu.VMEM((D_MODEL, D_FF), BF16),
            pltpu.VMEM((D_MODEL, D_FF), BF16),
            pltpu.VMEM((D_FF, D_MODEL), BF16),
            pltpu.SemaphoreType.DMA((N_FFN_WEIGHTS,)),
        ],
        compiler_params=pltpu.CompilerParams(
            dimension_semantics=("arbitrary",), vmem_limit_bytes=V7X_VMEM_LIMIT_BYTES),
        name="mixer_ffn",
    )(sinks, qd, kit, ks, hv, gate, dec, bq, kp, kp, vt, vt, attn_norm_g,
      x2d, w_out, ln2_g, w_gate, w_up, w_down, final_g)


def kernel(x, ln1_g, w_in, lb_logits, hgrn_norm_g, attn_sinks, attn_norm_g, w_out, ln2_g,
           w_gate, w_up, w_down, final_g):
    bsz, seq, d = x.shape
    x2d = x.reshape(bsz * seq, d)
    *prep, wo16, wg16, wu16, wd16 = _in_proj(x2d, ln1_g[0:1], w_in[0], lb_logits, hgrn_norm_g[0:1],
                                             w_out[0], w_gate[0], w_up[0], w_down[0])
    out = _mixer_ffn(prep, attn_sinks[0], attn_norm_g[0:1], x2d, wo16, ln2_g[0:1],
                     wg16, wu16, wd16, final_g.reshape(1, d), seq)
    return out.reshape(bsz, seq, d)
```

```python
import functools

import jax
import jax.numpy as jnp
from jax import lax
from jax.experimental import pallas as pl
from jax.experimental.pallas import tpu as pltpu

D_MODEL = 1024
A_HEADS = 4
A_DK = 128
A_DV = 128
A_QK = A_HEADS * A_DK
A_WIDTH = A_HEADS * A_DV
CHUNK = 64
B_HEADS = 8
B_KV_HEADS = 2
B_HEAD_DIM = 64
B_WIDTH = B_HEADS * B_HEAD_DIM
B_KV = B_KV_HEADS * B_HEAD_DIM
WINDOW = 128
BLOCK = 128
MIX_WIDTH = A_WIDTH + B_WIDTH
D_FF = 2816
IN_COLS = 4 * A_QK + B_WIDTH + 2 * B_KV
EPS = 1e-6
NEG_BIG = -1e30
LANES = 128
KEY_VARIANTS = 2 * B_KV_HEADS
N_FFN_WEIGHTS = 4

F32 = jnp.float32
BF16 = jnp.bfloat16

V7X_VMEM_LIMIT_BYTES = 56 * 1024 * 1024

TM_PROJ = 512
TS_MIX = 512
FF_CHUNK = 256
HGRN_GROUP_CHUNKS = 2
PAIR = 2 * CHUNK
CAST_ROWS = 128
LOG2E = 1.4426950408889634

NT_DIMS = (((1,), (1,)), ((), ()))
TN_DIMS = (((0,), (0,)), ((), ()))


def _rms_scale(v):
    return lax.rsqrt(jnp.mean(v * v, axis=-1, keepdims=True) + EPS)


def _silu(v):
    return v * (1.0 / (1.0 + jnp.exp(-v)))


def _sigmoid(v):
    return 1.0 / (1.0 + jnp.exp(-v))


def _in_proj_kernel(x_ref, g_ref, w32_ref, lbl_ref, hng_ref, wo32_ref, wg32_ref, wu32_ref, wd32_ref,
                    qd_ref, kit_ref, ks_ref, hv_ref, gate_ref, dec_ref, bq_ref, kp_ref, vt_ref,
                    wo16_ref, wg16_ref, wu16_ref, wd16_ref, w_ref, *, n_down_slabs):
    tm = x_ref.shape[0]

    @pl.when(pl.program_id(0) == 0)
    def _():
        for r0 in range(0, D_MODEL, CAST_ROWS):
            w_ref[r0:r0 + CAST_ROWS, :] = w32_ref[r0:r0 + CAST_ROWS, :].astype(BF16)

    @pl.when(pl.program_id(0) < n_down_slabs)
    def _():
        wd16_ref[...] = wd32_ref[...].astype(BF16)

    x = x_ref[...]
    u = (x * _rms_scale(x) * g_ref[...]).astype(BF16)

    lbl = lbl_ref[...]
    lmax = jnp.max(lbl, axis=0, keepdims=True)
    le = jnp.exp(lbl - lmax)
    lb = le[0:1, :] / jnp.sum(le, axis=0, keepdims=True)

    r = lax.broadcasted_iota(jnp.int32, (CHUNK, CHUNK), 0)
    c = lax.broadcasted_iota(jnp.int32, (CHUNK, CHUNK), 1)
    tri = jnp.where(r >= c, 1.0, 0.0).astype(BF16)

    pf = jnp.dot(u, w_ref[:, A_QK:2 * A_QK], preferred_element_type=F32)
    pq = jnp.dot(u, w_ref[:, 0:A_QK], preferred_element_type=F32)
    k_pair = []
    for ci in range(tm // CHUNK):
        rows = slice(ci * CHUNK, (ci + 1) * CHUNK)
        forget = lb + (1.0 - lb) * _sigmoid(pf[rows])
        key = 1.0 - forget
        logf = jnp.log(forget)
        p0 = logf.astype(BF16)
        r1 = logf - p0.astype(F32)
        p1 = r1.astype(BF16)
        p2 = (r1 - p1.astype(F32)).astype(BF16)
        bc3 = jnp.dot(tri, jnp.concatenate([p0, p1, p2], axis=-1), preferred_element_type=F32)
        bcum = bc3[:, :A_QK] + bc3[:, A_QK:2 * A_QK] + bc3[:, 2 * A_QK:]
        decay = jnp.exp(bcum[CHUNK - 1:CHUNK, :])
        k_intra = key * jnp.exp(-bcum)
        qd_ref[rows, :] = (_silu(pq[rows]) * jnp.exp(bcum)).astype(BF16)
        ks_ref[rows, :] = (k_intra * decay).astype(BF16)
        dec_ref[ci:ci + 1, :] = decay
        k_pair.append(k_intra)
        if len(k_pair) == PAIR // CHUNK:
            p0_row = (ci + 1) * CHUNK - PAIR
            kit_ref[:, p0_row:p0_row + PAIR] = jnp.concatenate(k_pair, axis=0).T.astype(BF16)
            k_pair = []

    wo16_ref[...] = wo32_ref[...].astype(BF16)
    wg16_ref[...] = wg32_ref[...].astype(BF16)
    wu16_ref[...] = wu32_ref[...].astype(BF16)

    c0 = 2 * A_QK + 2 * A_WIDTH
    pa = jnp.dot(u, w_ref[:, c0:], preferred_element_type=F32)
    bq_ref[...] = (pa[:, :B_WIDTH] * (B_HEAD_DIM ** -0.5 * LOG2E)).astype(BF16)
    lo = lax.broadcasted_iota(jnp.int32, (tm, LANES), 1) < B_HEAD_DIM
    k_all = pa[:, B_WIDTH:B_WIDTH + B_KV]
    k_rot = pltpu.roll(k_all, B_HEAD_DIM, axis=1)
    kp_ref[0] = jnp.where(lo, k_all, 0.0).astype(BF16)
    kp_ref[1] = jnp.where(lo, 0.0, k_rot).astype(BF16)
    kp_ref[2] = jnp.where(lo, k_rot, 0.0).astype(BF16)
    kp_ref[3] = jnp.where(lo, 0.0, k_all).astype(BF16)
    vt_ref[...] = pa[:, B_WIDTH + B_KV:].T.astype(BF16)

    pg = jnp.dot(u, w_ref[:, 2 * A_QK + A_WIDTH:2 * A_QK + 2 * A_WIDTH], preferred_element_type=F32)
    gate_ref[...] = (_silu(pg) * hng_ref[...]).astype(BF16)
    pi = jnp.dot(u, w_ref[:, 2 * A_QK:2 * A_QK + A_WIDTH], preferred_element_type=F32)
    hv_ref[...] = pi.astype(BF16)


def _in_proj(x2d, ln1_g, w_in, lb_logits, hgrn_norm_g, w_out, w_gate, w_up, w_down):
    t = x2d.shape[0]
    tm = TM_PROJ
    n_steps = t // tm
    row_blk = lambda width: pl.BlockSpec((tm, width), lambda i: (i, 0))
    const = lambda shape, **kw: pl.BlockSpec(shape, lambda i: (0, 0), **kw)
    bf = lambda shape: jax.ShapeDtypeStruct(shape, BF16)

    def slab(w, rows):
        n_slabs = w.shape[0] // rows
        assert w.shape[0] % rows == 0 and n_slabs <= n_steps, (w.shape, rows)
        return pl.BlockSpec((rows, w.shape[1]), lambda i: (jnp.minimum(i, n_slabs - 1), 0))

    cast_weights = (w_out, w_gate, w_up, w_down)
    even_rows = D_MODEL // n_steps
    slab_rows = (even_rows, even_rows, even_rows, CAST_ROWS)
    slab_specs = lambda: [slab(w, rows) for w, rows in zip(cast_weights, slab_rows)]
    return pl.pallas_call(
        functools.partial(_in_proj_kernel, n_down_slabs=D_FF // CAST_ROWS),
        grid=(n_steps,),
        in_specs=[
            row_blk(D_MODEL),
            const((1, D_MODEL)),
            const((D_MODEL, IN_COLS), pipeline_mode=pl.Buffered(1)),
            const((2, A_QK)),
            const((1, A_WIDTH)),
        ] + slab_specs(),
        out_specs=[
            row_blk(A_QK), pl.BlockSpec((A_QK, tm), lambda i: (0, i)), row_blk(A_QK),
            row_blk(A_WIDTH), row_blk(A_WIDTH),
            pl.BlockSpec((tm // CHUNK, A_QK), lambda i: (i, 0)),
            row_blk(B_WIDTH),
            pl.BlockSpec((KEY_VARIANTS, tm, LANES), lambda i: (0, i, 0)),
            pl.BlockSpec((LANES, tm), lambda i: (0, i)),
        ] + slab_specs(),
        out_shape=[
            bf((t, A_QK)), bf((A_QK, t)), bf((t, A_QK)), bf((t, A_WIDTH)), bf((t, A_WIDTH)),
            jax.ShapeDtypeStruct((t // CHUNK, A_QK), F32),
            bf((t, B_WIDTH)),
            bf((KEY_VARIANTS, t, LANES)),
            bf((LANES, t)),
        ] + [bf(w.shape) for w in cast_weights],
        scratch_shapes=[pltpu.VMEM((D_MODEL, IN_COLS), BF16)],
        compiler_params=pltpu.CompilerParams(
            dimension_semantics=("arbitrary",), vmem_limit_bytes=V7X_VMEM_LIMIT_BYTES),
        name="in_proj",
    )(x2d, ln1_g, w_in, lb_logits, hgrn_norm_g, *cast_weights)


def _bias_table_init(bias_ref):
    kj = lax.broadcasted_iota(jnp.int32, (2 * BLOCK, BLOCK), 0)
    qi = lax.broadcasted_iota(jnp.int32, (2 * BLOCK, BLOCK), 1)
    dist = qi + BLOCK - kj
    valid = (dist >= 0) & (dist < WINDOW)
    valid_first = valid & (kj >= BLOCK)
    distf = dist.astype(F32)
    for h in range(B_HEADS):
        slope = 2.0 ** (-(8.0 / B_HEADS) * (h + 1))
        sc = (-slope * LOG2E) * distf
        bias_ref[0, h] = jnp.where(valid, sc, NEG_BIG)
        bias_ref[1, h] = jnp.where(valid_first, sc, NEG_BIG)


def _mixer_stages(first, sinks_ref, qd_ref, kit_ref, ks_ref, hv_ref, gate_ref, dec_ref, bq_ref,
                  kp_ref, kph_ref, vt_ref, vth_ref, ang_ref, y_ref,
                  state_ref, upd_ref, sprev_ref, bias_ref):
    ts = qd_ref.shape[0]
    n_chunk = ts // CHUNK
    n_blk = ts // BLOCK
    head_cols = [slice(h * A_DK, (h + 1) * A_DK) for h in range(A_HEADS)]

    def chunk_units(c_lo, c_hi):
        return [(ci, slice(ci * CHUNK, (ci + 1) * CHUNK), head_cols[h], h)
                for ci in range(c_lo, c_hi) for h in range(A_HEADS)]

    def hgrn_updates(c_lo, c_hi):
        units = chunk_units(c_lo, c_hi)
        upds = [lax.dot_general(hv_ref[rows, cs], ks_ref[rows, cs], TN_DIMS,
                                preferred_element_type=F32) for (_, rows, cs, _) in units]
        for upd, (ci, _, _, h) in zip(upds, units):
            upd_ref[ci, h] = upd

    def hgrn_scan(h):
        st = state_ref[h]
        for ci in range(n_chunk):
            sprev_ref[ci, h] = st.T.astype(BF16)
            st = st * dec_ref[ci:ci + 1, head_cols[h]] + upd_ref[ci, h]
        state_ref[h] = st

    def hgrn_scores(p, box):
        rows = slice(p * PAIR, (p + 1) * PAIR)
        box["a"] = [jnp.dot(qd_ref[rows, cs], kit_ref[cs, rows], preferred_element_type=F32)
                    for cs in head_cols]
        box["o_inter"] = [
            jnp.concatenate(
                [jnp.dot(qd_ref[ci * CHUNK:(ci + 1) * CHUNK, cs], sprev_ref[ci, h],
                         preferred_element_type=F32)
                 for ci in range(p * PAIR // CHUNK, (p + 1) * PAIR // CHUNK)], axis=0)
            for h, cs in enumerate(head_cols)]

    def hgrn_values(p, box):
        rows = slice(p * PAIR, (p + 1) * PAIR)
        r = lax.broadcasted_iota(jnp.int32, (PAIR, PAIR), 0)
        c = lax.broadcasted_iota(jnp.int32, (PAIR, PAIR), 1)
        same_chunk_causal = (c <= r) & (c >= (r // CHUNK) * CHUNK)
        a_s = [jnp.where(same_chunk_causal, a, 0.0).astype(BF16) for a in box["a"]]
        box["o"] = [jnp.dot(a, hv_ref[rows, cs], preferred_element_type=F32) + oi
                    for a, oi, cs in zip(a_s, box["o_inter"], head_cols)]

    def hgrn_store(p, box):
        rows = slice(p * PAIR, (p + 1) * PAIR)
        for o, cs in zip(box["o"], head_cols):
            gate = gate_ref[rows, cs].astype(F32)
            y_ref[rows, cs] = (o * _rms_scale(o) * gate).astype(y_ref.dtype)

    swa_units = [(kvh, par) for kvh in range(B_KV_HEADS) for par in range(2)]

    def swa_scores(j, box):
        s_ts = []
        for kvh, par in swa_units:
            rows = slice(j * BLOCK, (j + 1) * BLOCK)
            q_pairs = jnp.concatenate(
                [bq_ref[rows, (2 * kvh) * LANES:(2 * kvh + 1) * LANES],
                 bq_ref[rows, (2 * kvh + 1) * LANES:(2 * kvh + 2) * LANES]], axis=0)
            idx = 2 * kvh + par
            if j == 0:
                kk = jnp.concatenate([kph_ref[idx], kp_ref[idx, 0:BLOCK, :]], axis=0)
            else:
                kk = kp_ref[idx, (j - 1) * BLOCK:(j + 1) * BLOCK, :]
            s_ts.append(lax.dot_general(kk, q_pairs, NT_DIMS, preferred_element_type=F32))
        box["s"] = s_ts

    def swa_softmax(j, box):
        p_ts, inv_ls = [], []
        for (kvh, par), s_t in zip(swa_units, box["s"]):
            p_halves = []
            for half, h in enumerate((4 * kvh + par, 4 * kvh + 2 + par)):
                bias = bias_ref[first, h] if j == 0 else bias_ref[0, h]
                s = s_t[:, half * BLOCK:(half + 1) * BLOCK] + bias
                sink = sinks_ref[h] * LOG2E
                m = jnp.maximum(jnp.max(s, axis=0, keepdims=True), sink)
                p = jnp.exp2(s - m)
                inv_ls.append(1.0 / (jnp.sum(p, axis=0, keepdims=True) + jnp.exp2(sink - m)))
                p_halves.append(p.astype(BF16))
            p_ts.append(jnp.concatenate(p_halves, axis=1))
        box["p"], box["inv_l"] = p_ts, inv_ls

    def swa_values(j, box):
        o_t = [None] * B_HEADS
        for ui, ((kvh, par), p_t) in enumerate(zip(swa_units, box["p"])):
            drow = slice(kvh * B_HEAD_DIM, (kvh + 1) * B_HEAD_DIM)
            if j == 0:
                vt_h = jnp.concatenate([vth_ref[drow, :], vt_ref[drow, 0:BLOCK]], axis=1)
            else:
                vt_h = vt_ref[drow, (j - 1) * BLOCK:(j + 1) * BLOCK]
            pv = jnp.dot(vt_h, p_t, preferred_element_type=F32)
            for half, h in enumerate((4 * kvh + par, 4 * kvh + 2 + par)):
                o_t[h] = pv[:, half * BLOCK:(half + 1) * BLOCK] * box["inv_l"][2 * ui + half]
        box["o_t"] = o_t

    def swa_store(j, box):
        o_all_t = jnp.concatenate(box["o_t"], axis=0)
        scale = lax.rsqrt(jnp.sum(o_all_t * o_all_t, axis=0, keepdims=True) * (1.0 / B_WIDTH) + EPS)
        o_all = (o_all_t * scale).T
        y_ref[j * BLOCK:(j + 1) * BLOCK, A_WIDTH:] = (o_all * ang_ref[...]).astype(y_ref.dtype)

    group = HGRN_GROUP_CHUNKS
    hgrn = [functools.partial(hgrn_updates, c, c + group) for c in range(0, n_chunk, group)]
    hgrn += [functools.partial(hgrn_scan, h) for h in range(A_HEADS)]
    for p in range(ts // PAIR):
        box = {}
        hgrn += [functools.partial(f, p, box) for f in (hgrn_scores, hgrn_values, hgrn_store)]
    swa = []
    for j in range(n_blk):
        box = {}
        swa += [functools.partial(f, j, box) for f in (swa_scores, swa_softmax, swa_values, swa_store)]
    stages = []
    for k in range(max(len(hgrn), len(swa))):
        stages += hgrn[k:k + 1] + swa[k:k + 1]
    return stages


def _ffn_stages(x_ref, y_ref, wo_ref, g2_ref, wg_ref, wu_ref, wd_ref, h_ref, act_ref, h2_ref,
                before_down=()):
    box = {}

    def head():
        h = x_ref[...] + jnp.dot(y_ref[...], wo_ref[...], preferred_element_type=F32)
        box["u"] = (h * _rms_scale(h) * g2_ref[...]).astype(BF16)
        h_ref[...] = h

    def ff_up(c0):
        cs = slice(c0, c0 + FF_CHUNK)
        box["gate"] = jnp.dot(box["u"], wg_ref[:, cs], preferred_element_type=F32)
        box["up"] = jnp.dot(box["u"], wu_ref[:, cs], preferred_element_type=F32)

    def ff_act(c0):
        act_ref[:, c0:c0 + FF_CHUNK] = (_silu(box["gate"]) * box["up"]).astype(BF16)

    def down():
        h2_ref[...] = h_ref[...] + jnp.dot(act_ref[...], wd_ref[...], preferred_element_type=F32)

    stages = [head]
    for c0 in range(0, D_FF, FF_CHUNK):
        stages += [functools.partial(ff_up, c0), functools.partial(ff_act, c0)]
    return stages + list(before_down) + [down]


def _final_norm(h2_ref, gf_ref, o_ref):
    h2 = h2_ref[...]
    o_ref[...] = h2 * _rms_scale(h2) * gf_ref[...]


def _run_interleaved(major, minor):
    span = max(len(major) - 2, 1)
    done = 0
    for k, stage in enumerate(major):
        stage()
        want = min(len(minor), -(-(k + 1) * len(minor) // span))
        while done < want:
            minor[done]()
            done += 1


def _mixer_ffn_kernel(sinks_ref, qd_ref, kit_ref, ks_ref, hv_ref, gate_ref, dec_ref, bq_ref,
                      kp_ref, kph_ref, vt_ref, vth_ref, ang_ref,
                      x_ref, wo_hbm, g2_ref, wg_hbm, wu_hbm, wd_hbm, gf_ref, o_ref,
                      state_ref, upd_ref, sprev_ref, bias_ref, y_cur, y_prev, h_ref, act_ref, h2_ref,
                      wo_ref, wg_ref, wu_ref, wd_ref, w_sems, *, n_tiles, n_s):
    i = pl.program_id(0)
    seq_start = lax.rem(i, n_s) == 0
    weight_copies = [
        pltpu.make_async_copy(src, dst, w_sems.at[k])
        for k, (src, dst) in enumerate(((wo_hbm, wo_ref), (wg_hbm, wg_ref),
                                        (wu_hbm, wu_ref), (wd_hbm, wd_ref)))]

    @pl.when(i == 0)
    def _():
        for copy in weight_copies:
            copy.start()
        _bias_table_init(bias_ref)

    @pl.when(i == 1)
    def _():
        for copy in weight_copies:
            copy.wait()

    @pl.when(seq_start)
    def _():
        state_ref[...] = jnp.zeros_like(state_ref)

    first = jnp.where(seq_start, 1, 0)
    mixer_args = (first, sinks_ref, qd_ref, kit_ref, ks_ref, hv_ref, gate_ref, dec_ref, bq_ref,
                  kp_ref, kph_ref, vt_ref, vth_ref, ang_ref, y_cur,
                  state_ref, upd_ref, sprev_ref, bias_ref)
    ffn_args = (x_ref, y_prev, wo_ref, g2_ref, wg_ref, wu_ref, wd_ref, h_ref, act_ref, h2_ref)
    final_norm = functools.partial(_final_norm, h2_ref, gf_ref, o_ref)

    @pl.when(i == 0)
    def _():
        _run_interleaved(_mixer_stages(*mixer_args), [])
        y_prev[...] = y_cur[...]
        h2_ref[...] = jnp.zeros_like(h2_ref)

    @pl.when((i >= 1) & (i < n_tiles))
    def _():
        _run_interleaved(_ffn_stages(*ffn_args, before_down=[final_norm]),
                         _mixer_stages(*mixer_args))
        y_prev[...] = y_cur[...]

    @pl.when(i == n_tiles)
    def _():
        _run_interleaved(_ffn_stages(*ffn_args, before_down=[final_norm]), [])

    @pl.when(i == n_tiles + 1)
    def _():
        final_norm()


def _mixer_ffn(prep, sinks, attn_norm_g, x2d, w_out, ln2_g, w_gate, w_up, w_down, final_g, seq):
    qd, kit, ks, hv, gate, dec, bq, kp, vt = prep
    t = x2d.shape[0]
    ts = TS_MIX
    n_tiles = t // ts
    n_s = seq // ts
    n_chunk = ts // CHUNK
    n_blk = ts // BLOCK
    cur = lambda i: jnp.minimum(i, n_tiles - 1)
    prev = lambda i: jnp.clip(i - 1, 0, n_tiles - 1)
    last = lambda i: jnp.clip(i - 2, 0, n_tiles - 1)
    halo = lambda i: jnp.maximum(cur(i) * n_blk - 1, 0)
    row_blk = lambda width: pl.BlockSpec((ts, width), lambda i: (cur(i), 0))
    const = lambda shape: pl.BlockSpec(shape, lambda i: (0, 0), pipeline_mode=pl.Buffered(1))
    in_hbm = pl.BlockSpec(memory_space=pl.ANY)
    assert n_tiles >= 2
    return pl.pallas_call(
        functools.partial(_mixer_ffn_kernel, n_tiles=n_tiles, n_s=n_s),
        grid=(n_tiles + 2,),
        in_specs=[
            pl.BlockSpec(memory_space=pltpu.SMEM),
            row_blk(A_QK), pl.BlockSpec((A_QK, ts), lambda i: (0, cur(i))), row_blk(A_QK),
            row_blk(A_WIDTH), row_blk(A_WIDTH),
            pl.BlockSpec((n_chunk, A_QK), lambda i: (cur(i), 0)),
            row_blk(B_WIDTH),
            pl.BlockSpec((KEY_VARIANTS, ts, LANES), lambda i: (0, cur(i), 0)),
            pl.BlockSpec((KEY_VARIANTS, BLOCK, LANES), lambda i: (0, halo(i), 0)),
            pl.BlockSpec((LANES, ts), lambda i: (0, cur(i))),
            pl.BlockSpec((LANES, BLOCK), lambda i: (0, halo(i))),
            pl.BlockSpec((1, B_WIDTH), lambda i: (0, 0)),
            pl.BlockSpec((ts, D_MODEL), lambda i: (prev(i), 0)),
            in_hbm,
            const((1, D_MODEL)),
            in_hbm, in_hbm, in_hbm,
            const((1, D_MODEL)),
        ],
        out_specs=pl.BlockSpec((ts, D_MODEL), lambda i: (last(i), 0)),
        out_shape=jax.ShapeDtypeStruct((t, D_MODEL), F32),
        scratch_shapes=[
            pltpu.VMEM((A_HEADS, A_DV, A_DK), F32),
            pltpu.VMEM((n_chunk, A_HEADS, A_DV, A_DK), F32),
            pltpu.VMEM((n_chunk, A_HEADS, A_DV, A_DK), BF16),
            pltpu.VMEM((2, B_HEADS, 2 * BLOCK, BLOCK), F32),
            pltpu.VMEM((ts, MIX_WIDTH), BF16),
            pltpu.VMEM((ts, MIX_WIDTH), BF16),
            pltpu.VMEM((ts, D_MODEL), F32),
            pltpu.VMEM((ts, D_FF), BF16),
            pltpu.VMEM((ts, D_MODEL), F32),
            pltpu.VMEM((MIX_WIDTH, D_MODEL), BF16),
            pltpu.VMEM((D_MODEL, D_FF), BF16),
            pltpu.VMEM((D_MODEL, D_FF), BF16),
            pltpu.VMEM((D_FF, D_MODEL), BF16),
            pltpu.SemaphoreType.DMA((N_FFN_WEIGHTS,)),
        ],
        compiler_params=pltpu.CompilerParams(
            dimension_semantics=("arbitrary",), vmem_limit_bytes=V7X_VMEM_LIMIT_BYTES),
        name="mixer_ffn",
    )(sinks, qd, kit, ks, hv, gate, dec, bq, kp, kp, vt, vt, attn_norm_g,
      x2d, w_out, ln2_g, w_gate, w_up, w_down, final_g)


def kernel(x, ln1_g, w_in, lb_logits, hgrn_norm_g, attn_sinks, attn_norm_g, w_out, ln2_g,
           w_gate, w_up, w_down, final_g):
    bsz, seq, d = x.shape
    x2d = x.reshape(bsz * seq, d)
    *prep, wo16, wg16, wu16, wd16 = _in_proj(x2d, ln1_g[0:1], w_in[0], lb_logits, hgrn_norm_g[0:1],
                                             w_out[0], w_gate[0], w_up[0], w_down[0])
    out = _mixer_ffn(prep, attn_sinks[0], attn_norm_g[0:1], x2d, wo16, ln2_g[0:1],
                     wg16, wu16, wd16, final_g.reshape(1, d), seq)
    return out.reshape(bsz, seq, d)
```

```python
import functools

import jax
import jax.numpy as jnp
from jax import lax
from jax.experimental import pallas as pl
from jax.experimental.pallas import tpu as pltpu

D_MODEL = 1024
A_HEADS = 4
A_DK = 128
A_DV = 128
A_QK = A_HEADS * A_DK
A_WIDTH = A_HEADS * A_DV
CHUNK = 64
B_HEADS = 8
B_KV_HEADS = 2
B_HEAD_DIM = 64
B_WIDTH = B_HEADS * B_HEAD_DIM
B_KV = B_KV_HEADS * B_HEAD_DIM
WINDOW = 128
BLOCK = 128
MIX_WIDTH = A_WIDTH + B_WIDTH
D_FF = 2816
IN_COLS = 4 * A_QK + B_WIDTH + 2 * B_KV
EPS = 1e-6
NEG_BIG = -1e30
LANES = 128
KEY_VARIANTS = 2 * B_KV_HEADS
N_FFN_WEIGHTS = 4

F32 = jnp.float32
BF16 = jnp.bfloat16

V7X_VMEM_LIMIT_BYTES = 56 * 1024 * 1024

TM_PROJ = 512
TS_MIX = 512
FF_CHUNK = 256
HGRN_GROUP_CHUNKS = 2
PAIR = 2 * CHUNK
CAST_ROWS = 128
LOG2E = 1.4426950408889634

NT_DIMS = (((1,), (1,)), ((), ()))
TN_DIMS = (((0,), (0,)), ((), ()))


def _rms_scale(v):
    return lax.rsqrt(jnp.mean(v * v, axis=-1, keepdims=True) + EPS)


def _silu(v):
    return v * (1.0 / (1.0 + jnp.exp(-v)))


def _sigmoid(v):
    return 1.0 / (1.0 + jnp.exp(-v))


def _in_proj_kernel(x_ref, g_ref, w32_ref, lbl_ref, hng_ref, wo32_ref, wg32_ref, wu32_ref, wd32_ref,
                    qd_ref, kit_ref, ks_ref, hv_ref, gate_ref, dec_ref, bq_ref, kp_ref, vt_ref,
                    wo16_ref, wg16_ref, wu16_ref, wd16_ref, w_ref, *, n_down_slabs):
    tm = x_ref.shape[0]

    @pl.when(pl.program_id(0) == 0)
    def _():
        for r0 in range(0, D_MODEL, CAST_ROWS):
            w_ref[r0:r0 + CAST_ROWS, :] = w32_ref[r0:r0 + CAST_ROWS, :].astype(BF16)

    @pl.when(pl.program_id(0) < n_down_slabs)
    def _():
        wd16_ref[...] = wd32_ref[...].astype(BF16)

    u_parts, pf_parts = [], []
    for r0 in (0, tm // 2):
        xh = x_ref[r0:r0 + tm // 2, :]
        uh = (xh * _rms_scale(xh) * g_ref[...]).astype(BF16)
        u_parts.append(uh)
        pf_parts.append(jnp.dot(uh, w_ref[:, A_QK:2 * A_QK], preferred_element_type=F32))
    u = jnp.concatenate(u_parts, axis=0)
    pf = jnp.concatenate(pf_parts, axis=0)

    lbl = lbl_ref[...]
    lmax = jnp.max(lbl, axis=0, keepdims=True)
    le = jnp.exp(lbl - lmax)
    lb = le[0:1, :] / jnp.sum(le, axis=0, keepdims=True)

    r = lax.broadcasted_iota(jnp.int32, (CHUNK, CHUNK), 0)
    c = lax.broadcasted_iota(jnp.int32, (CHUNK, CHUNK), 1)
    tri = jnp.where(r >= c, 1.0, 0.0).astype(BF16)

    pq = jnp.dot(u, w_ref[:, 0:A_QK], preferred_element_type=F32)
    def attention_part():
        wo16_ref[...] = wo32_ref[...].astype(BF16)
        wg16_ref[...] = wg32_ref[...].astype(BF16)
        wu16_ref[...] = wu32_ref[...].astype(BF16)

        c0 = 2 * A_QK + 2 * A_WIDTH
        pa = jnp.dot(u, w_ref[:, c0:], preferred_element_type=F32)
        bq_ref[...] = (pa[:, :B_WIDTH] * (B_HEAD_DIM ** -0.5 * LOG2E)).astype(BF16)
        lo = lax.broadcasted_iota(jnp.int32, (tm, LANES), 1) < B_HEAD_DIM
        k_all = pa[:, B_WIDTH:B_WIDTH + B_KV]
        k_rot = pltpu.roll(k_all, B_HEAD_DIM, axis=1)
        kp_ref[0] = jnp.where(lo, k_all, 0.0).astype(BF16)
        kp_ref[1] = jnp.where(lo, 0.0, k_rot).astype(BF16)
        kp_ref[2] = jnp.where(lo, k_rot, 0.0).astype(BF16)
        kp_ref[3] = jnp.where(lo, 0.0, k_all).astype(BF16)
        vt_ref[...] = pa[:, B_WIDTH + B_KV:].T.astype(BF16)

    def gate_part():
        pg = jnp.dot(u, w_ref[:, 2 * A_QK + A_WIDTH:2 * A_QK + 2 * A_WIDTH], preferred_element_type=F32)
        gate_ref[...] = (_silu(pg) * hng_ref[...]).astype(BF16)

    def value_part():
        pi = jnp.dot(u, w_ref[:, 2 * A_QK:2 * A_QK + A_WIDTH], preferred_element_type=F32)
        hv_ref[...] = pi.astype(BF16)

    k_pair = []
    for ci in range(tm // CHUNK):
        rows = slice(ci * CHUNK, (ci + 1) * CHUNK)
        forget = lb + (1.0 - lb) * _sigmoid(pf[rows])
        key = 1.0 - forget
        logf = jnp.log(forget)
        p0 = logf.astype(BF16)
        r1 = logf - p0.astype(F32)
        p1 = r1.astype(BF16)
        p2 = (r1 - p1.astype(F32)).astype(BF16)
        bc3 = jnp.dot(tri, jnp.concatenate([p0, p1, p2], axis=-1), preferred_element_type=F32)
        bcum = bc3[:, :A_QK] + bc3[:, A_QK:2 * A_QK] + bc3[:, 2 * A_QK:]
        decay = jnp.exp(bcum[CHUNK - 1:CHUNK, :])
        k_intra = key * jnp.exp(-bcum)
        qd_ref[rows, :] = (_silu(pq[rows]) * jnp.exp(bcum)).astype(BF16)
        ks_ref[rows, :] = (k_intra * decay).astype(BF16)
        dec_ref[ci:ci + 1, :] = decay
        k_pair.append(k_intra)
        if len(k_pair) == PAIR // CHUNK:
            p0_row = (ci + 1) * CHUNK - PAIR
            kit_ref[:, p0_row:p0_row + PAIR] = jnp.concatenate(k_pair, axis=0).T.astype(BF16)
            k_pair = []

    attention_part()
    gate_part()
    value_part()


def _in_proj(x2d, ln1_g, w_in, lb_logits, hgrn_norm_g, w_out, w_gate, w_up, w_down):
    t = x2d.shape[0]
    tm = TM_PROJ
    n_steps = t // tm
    row_blk = lambda width: pl.BlockSpec((tm, width), lambda i: (i, 0))
    const = lambda shape, **kw: pl.BlockSpec(shape, lambda i: (0, 0), **kw)
    bf = lambda shape: jax.ShapeDtypeStruct(shape, BF16)

    def slab(w, rows):
        n_slabs = w.shape[0] // rows
        assert w.shape[0] % rows == 0 and n_slabs <= n_steps, (w.shape, rows)
        return pl.BlockSpec((rows, w.shape[1]), lambda i: (jnp.minimum(i, n_slabs - 1), 0))

    cast_weights = (w_out, w_gate, w_up, w_down)
    even_rows = D_MODEL // n_steps
    slab_rows = (even_rows, even_rows, even_rows, CAST_ROWS)
    slab_specs = lambda: [slab(w, rows) for w, rows in zip(cast_weights, slab_rows)]
    return pl.pallas_call(
        functools.partial(_in_proj_kernel, n_down_slabs=D_FF // CAST_ROWS),
        grid=(n_steps,),
        in_specs=[
            row_blk(D_MODEL),
            const((1, D_MODEL)),
            const((D_MODEL, IN_COLS), pipeline_mode=pl.Buffered(1)),
            const((2, A_QK)),
            const((1, A_WIDTH)),
        ] + slab_specs(),
        out_specs=[
            row_blk(A_QK), pl.BlockSpec((A_QK, tm), lambda i: (0, i)), row_blk(A_QK),
            row_blk(A_WIDTH), row_blk(A_WIDTH),
            pl.BlockSpec((tm // CHUNK, A_QK), lambda i: (i, 0)),
            row_blk(B_WIDTH),
            pl.BlockSpec((KEY_VARIANTS, tm, LANES), lambda i: (0, i, 0)),
            pl.BlockSpec((LANES, tm), lambda i: (0, i)),
        ] + slab_specs(),
        out_shape=[
            bf((t, A_QK)), bf((A_QK, t)), bf((t, A_QK)), bf((t, A_WIDTH)), bf((t, A_WIDTH)),
            jax.ShapeDtypeStruct((t // CHUNK, A_QK), F32),
            bf((t, B_WIDTH)),
            bf((KEY_VARIANTS, t, LANES)),
            bf((LANES, t)),
        ] + [bf(w.shape) for w in cast_weights],
        scratch_shapes=[pltpu.VMEM((D_MODEL, IN_COLS), BF16)],
        compiler_params=pltpu.CompilerParams(
            dimension_semantics=("arbitrary",), vmem_limit_bytes=V7X_VMEM_LIMIT_BYTES),
        name="in_proj",
    )(x2d, ln1_g, w_in, lb_logits, hgrn_norm_g, *cast_weights)


def _bias_table_init(bias_ref):
    kj = lax.broadcasted_iota(jnp.int32, (2 * BLOCK, BLOCK), 0)
    qi = lax.broadcasted_iota(jnp.int32, (2 * BLOCK, BLOCK), 1)
    dist = qi + BLOCK - kj
    valid = (dist >= 0) & (dist < WINDOW)
    valid_first = valid & (kj >= BLOCK)
    distf = dist.astype(F32)
    for h in range(B_HEADS):
        slope = 2.0 ** (-(8.0 / B_HEADS) * (h + 1))
        sc = (-slope * LOG2E) * distf
        bias_ref[0, h] = jnp.where(valid, sc, NEG_BIG)
        bias_ref[1, h] = jnp.where(valid_first, sc, NEG_BIG)


def _mixer_stages(first, sinks_ref, qd_ref, kit_ref, ks_ref, hv_ref, gate_ref, dec_ref, bq_ref,
                  kp_ref, kph_ref, vt_ref, vth_ref, ang_ref, y_ref,
                  state_ref, upd_ref, sprev_ref, bias_ref):
    ts = qd_ref.shape[0]
    n_chunk = ts // CHUNK
    n_blk = ts // BLOCK
    head_cols = [slice(h * A_DK, (h + 1) * A_DK) for h in range(A_HEADS)]

    def chunk_units(c_lo, c_hi):
        return [(ci, slice(ci * CHUNK, (ci + 1) * CHUNK), head_cols[h], h)
                for ci in range(c_lo, c_hi) for h in range(A_HEADS)]

    def hgrn_updates(c_lo, c_hi):
        units = chunk_units(c_lo, c_hi)
        upds = [lax.dot_general(hv_ref[rows, cs], ks_ref[rows, cs], TN_DIMS,
                                preferred_element_type=F32) for (_, rows, cs, _) in units]
        for upd, (ci, _, _, h) in zip(upds, units):
            upd_ref[ci, h] = upd

    def hgrn_scan(h):
        st = state_ref[h]
        for ci in range(n_chunk):
            sprev_ref[ci, h] = st.T.astype(BF16)
            st = st * dec_ref[ci:ci + 1, head_cols[h]] + upd_ref[ci, h]
        state_ref[h] = st

    def hgrn_scores(p, box):
        rows = slice(p * PAIR, (p + 1) * PAIR)
        box["a"] = [jnp.dot(qd_ref[rows, cs], kit_ref[cs, rows], preferred_element_type=F32)
                    for cs in head_cols]
        box["o_inter"] = [
            jnp.concatenate(
                [jnp.dot(qd_ref[ci * CHUNK:(ci + 1) * CHUNK, cs], sprev_ref[ci, h],
                         preferred_element_type=F32)
                 for ci in range(p * PAIR // CHUNK, (p + 1) * PAIR // CHUNK)], axis=0)
            for h, cs in enumerate(head_cols)]

    def hgrn_values(p, box):
        rows = slice(p * PAIR, (p + 1) * PAIR)
        r = lax.broadcasted_iota(jnp.int32, (PAIR, PAIR), 0)
        c = lax.broadcasted_iota(jnp.int32, (PAIR, PAIR), 1)
        same_chunk_causal = (c <= r) & (c >= (r // CHUNK) * CHUNK)
        a_s = [jnp.where(same_chunk_causal, a, 0.0).astype(BF16) for a in box["a"]]
        box["o"] = [jnp.dot(a, hv_ref[rows, cs], preferred_element_type=F32) + oi
                    for a, oi, cs in zip(a_s, box["o_inter"], head_cols)]

    def hgrn_store(p, box):
        rows = slice(p * PAIR, (p + 1) * PAIR)
        for o, cs in zip(box["o"], head_cols):
            gate = gate_ref[rows, cs].astype(F32)
            y_ref[rows, cs] = (o * _rms_scale(o) * gate).astype(y_ref.dtype)

    swa_units = [(kvh, par) for kvh in range(B_KV_HEADS) for par in range(2)]

    def swa_scores(j, box):
        s_ts = []
        for kvh, par in swa_units:
            rows = slice(j * BLOCK, (j + 1) * BLOCK)
            q_pairs = jnp.concatenate(
                [bq_ref[rows, (2 * kvh) * LANES:(2 * kvh + 1) * LANES],
                 bq_ref[rows, (2 * kvh + 1) * LANES:(2 * kvh + 2) * LANES]], axis=0)
            idx = 2 * kvh + par
            if j == 0:
                kk = jnp.concatenate([kph_ref[idx], kp_ref[idx, 0:BLOCK, :]], axis=0)
            else:
                kk = kp_ref[idx, (j - 1) * BLOCK:(j + 1) * BLOCK, :]
            s_ts.append(lax.dot_general(kk, q_pairs, NT_DIMS, preferred_element_type=F32))
        box["s"] = s_ts

    def swa_softmax(j, box):
        p_ts, inv_ls = [], []
        for (kvh, par), s_t in zip(swa_units, box["s"]):
            p_halves = []
            for half, h in enumerate((4 * kvh + par, 4 * kvh + 2 + par)):
                bias = bias_ref[first, h] if j == 0 else bias_ref[0, h]
                s = s_t[:, half * BLOCK:(half + 1) * BLOCK] + bias
                sink = sinks_ref[h] * LOG2E
                m = jnp.maximum(jnp.max(s, axis=0, keepdims=True), sink)
                p = jnp.exp2(s - m)
                inv_ls.append(1.0 / (jnp.sum(p, axis=0, keepdims=True) + jnp.exp2(sink - m)))
                p_halves.append(p.astype(BF16))
            p_ts.append(jnp.concatenate(p_halves, axis=1))
        box["p"], box["inv_l"] = p_ts, inv_ls

    def swa_values(j, box):
        o_t = [None] * B_HEADS
        for ui, ((kvh, par), p_t) in enumerate(zip(swa_units, box["p"])):
            drow = slice(kvh * B_HEAD_DIM, (kvh + 1) * B_HEAD_DIM)
            if j == 0:
                vt_h = jnp.concatenate([vth_ref[drow, :], vt_ref[drow, 0:BLOCK]], axis=1)
            else:
                vt_h = vt_ref[drow, (j - 1) * BLOCK:(j + 1) * BLOCK]
            pv = jnp.dot(vt_h, p_t, preferred_element_type=F32)
            for half, h in enumerate((4 * kvh + par, 4 * kvh + 2 + par)):
                o_t[h] = pv[:, half * BLOCK:(half + 1) * BLOCK] * box["inv_l"][2 * ui + half]
        box["o_t"] = o_t

    def swa_store(j, box):
        o_all_t = jnp.concatenate(box["o_t"], axis=0)
        scale = lax.rsqrt(jnp.sum(o_all_t * o_all_t, axis=0, keepdims=True) * (1.0 / B_WIDTH) + EPS)
        o_all = (o_all_t * scale).T
        y_ref[j * BLOCK:(j + 1) * BLOCK, A_WIDTH:] = (o_all * ang_ref[...]).astype(y_ref.dtype)

    group = HGRN_GROUP_CHUNKS
    hgrn = [functools.partial(hgrn_updates, c, c + group) for c in range(0, n_chunk, group)]
    hgrn += [functools.partial(hgrn_scan, h) for h in range(A_HEADS)]
    for p in range(ts // PAIR):
        box = {}
        hgrn += [functools.partial(f, p, box) for f in (hgrn_scores, hgrn_values, hgrn_store)]
    swa = []
    for j in range(n_blk):
        box = {}
        swa += [functools.partial(f, j, box) for f in (swa_scores, swa_softmax, swa_values, swa_store)]
    stages = []
    for k in range(max(len(hgrn), len(swa))):
        stages += hgrn[k:k + 1] + swa[k:k + 1]
    return stages


def _ffn_stages(x_ref, y_ref, wo_ref, g2_ref, wg_ref, wu_ref, wd_ref, h_ref, act_ref, h2_ref,
                before_down=()):
    box = {}

    def head():
        h = x_ref[...] + jnp.dot(y_ref[...], wo_ref[...], preferred_element_type=F32)
        box["u"] = (h * _rms_scale(h) * g2_ref[...]).astype(BF16)
        h_ref[...] = h

    def ff_up(c0):
        cs = slice(c0, c0 + FF_CHUNK)
        box["gate"] = jnp.dot(box["u"], wg_ref[:, cs], preferred_element_type=F32)
        box["up"] = jnp.dot(box["u"], wu_ref[:, cs], preferred_element_type=F32)

    def ff_act(c0):
        act_ref[:, c0:c0 + FF_CHUNK] = (_silu(box["gate"]) * box["up"]).astype(BF16)

    def down():
        h2_ref[...] = h_ref[...] + jnp.dot(act_ref[...], wd_ref[...], preferred_element_type=F32)

    stages = [head]
    for c0 in range(0, D_FF, FF_CHUNK):
        stages += [functools.partial(ff_up, c0), functools.partial(ff_act, c0)]
    return stages + list(before_down) + [down]


def _final_norm(h2_ref, gf_ref, o_ref):
    h2 = h2_ref[...]
    o_ref[...] = h2 * _rms_scale(h2) * gf_ref[...]


def _run_interleaved(major, minor):
    span = max(len(major) - 2, 1)
    done = 0
    for k, stage in enumerate(major):
        stage()
        want = min(len(minor), -(-(k + 1) * len(minor) // span))
        while done < want:
            minor[done]()
            done += 1


def _mixer_ffn_kernel(sinks_ref, qd_ref, kit_ref, ks_ref, hv_ref, gate_ref, dec_ref, bq_ref,
                      kp_ref, kph_ref, vt_ref, vth_ref, ang_ref,
                      x_ref, wo_hbm, g2_ref, wg_hbm, wu_hbm, wd_hbm, gf_ref, o_ref,
                      state_ref, upd_ref, sprev_ref, bias_ref, y_cur, y_prev, h_ref, act_ref, h2_ref,
                      wo_ref, wg_ref, wu_ref, wd_ref, w_sems, *, n_tiles, n_s):
    i = pl.program_id(0)
    seq_start = lax.rem(i, n_s) == 0
    weight_copies = [
        pltpu.make_async_copy(src, dst, w_sems.at[k])
        for k, (src, dst) in enumerate(((wo_hbm, wo_ref), (wg_hbm, wg_ref),
                                        (wu_hbm, wu_ref), (wd_hbm, wd_ref)))]

    @pl.when(i == 0)
    def _():
        for copy in weight_copies:
            copy.start()
        _bias_table_init(bias_ref)

    @pl.when(i == 1)
    def _():
        for copy in weight_copies:
            copy.wait()

    @pl.when(seq_start)
    def _():
        state_ref[...] = jnp.zeros_like(state_ref)

    first = jnp.where(seq_start, 1, 0)
    mixer_args = (first, sinks_ref, qd_ref, kit_ref, ks_ref, hv_ref, gate_ref, dec_ref, bq_ref,
                  kp_ref, kph_ref, vt_ref, vth_ref, ang_ref, y_cur,
                  state_ref, upd_ref, sprev_ref, bias_ref)
    ffn_args = (x_ref, y_prev, wo_ref, g2_ref, wg_ref, wu_ref, wd_ref, h_ref, act_ref, h2_ref)
    final_norm = functools.partial(_final_norm, h2_ref, gf_ref, o_ref)

    @pl.when(i == 0)
    def _():
        _run_interleaved(_mixer_stages(*mixer_args), [])
        y_prev[...] = y_cur[...]

    @pl.when(i == 1)
    def _():
        _run_interleaved(_ffn_stages(*ffn_args), _mixer_stages(*mixer_args))
        y_prev[...] = y_cur[...]

    @pl.when((i > 1) & (i < n_tiles))
    def _():
        _run_interleaved(_ffn_stages(*ffn_args, before_down=[final_norm]),
                         _mixer_stages(*mixer_args))
        y_prev[...] = y_cur[...]

    @pl.when(i == n_tiles)
    def _():
        _run_interleaved(_ffn_stages(*ffn_args, before_down=[final_norm]), [])

    @pl.when(i == n_tiles + 1)
    def _():
        final_norm()


def _mixer_ffn(prep, sinks, attn_norm_g, x2d, w_out, ln2_g, w_gate, w_up, w_down, final_g, seq):
    qd, kit, ks, hv, gate, dec, bq, kp, vt = prep
    t = x2d.shape[0]
    ts = TS_MIX
    n_tiles = t // ts
    n_s = seq // ts
    n_chunk = ts // CHUNK
    n_blk = ts // BLOCK
    cur = lambda i: jnp.minimum(i, n_tiles - 1)
    prev = lambda i: jnp.clip(i - 1, 0, n_tiles - 1)
    last = lambda i: jnp.clip(i - 2, 0, n_tiles - 1)
    halo = lambda i: jnp.maximum(cur(i) * n_blk - 1, 0)
    row_blk = lambda width: pl.BlockSpec((ts, width), lambda i: (cur(i), 0))
    const = lambda shape: pl.BlockSpec(shape, lambda i: (0, 0), pipeline_mode=pl.Buffered(1))
    in_hbm = pl.BlockSpec(memory_space=pl.ANY)
    assert n_tiles >= 2
    return pl.pallas_call(
        functools.partial(_mixer_ffn_kernel, n_tiles=n_tiles, n_s=n_s),
        grid=(n_tiles + 2,),
        in_specs=[
            pl.BlockSpec(memory_space=pltpu.SMEM),
            row_blk(A_QK), pl.BlockSpec((A_QK, ts), lambda i: (0, cur(i))), row_blk(A_QK),
            row_blk(A_WIDTH), row_blk(A_WIDTH),
            pl.BlockSpec((n_chunk, A_QK), lambda i: (cur(i), 0)),
            row_blk(B_WIDTH),
            pl.BlockSpec((KEY_VARIANTS, ts, LANES), lambda i: (0, cur(i), 0)),
            pl.BlockSpec((KEY_VARIANTS, BLOCK, LANES), lambda i: (0, halo(i), 0)),
            pl.BlockSpec((LANES, ts), lambda i: (0, cur(i))),
            pl.BlockSpec((LANES, BLOCK), lambda i: (0, halo(i))),
            pl.BlockSpec((1, B_WIDTH), lambda i: (0, 0)),
            pl.BlockSpec((ts, D_MODEL), lambda i: (prev(i), 0)),
            in_hbm,
            const((1, D_MODEL)),
            in_hbm, in_hbm, in_hbm,
            const((1, D_MODEL)),
        ],
        out_specs=pl.BlockSpec((ts, D_MODEL), lambda i: (last(i), 0)),
        out_shape=jax.ShapeDtypeStruct((t, D_MODEL), F32),
        scratch_shapes=[
            pltpu.VMEM((A_HEADS, A_DV, A_DK), F32),
            pltpu.VMEM((n_chunk, A_HEADS, A_DV, A_DK), F32),
            pltpu.VMEM((n_chunk, A_HEADS, A_DV, A_DK), BF16),
            pltpu.VMEM((2, B_HEADS, 2 * BLOCK, BLOCK), F32),
            pltpu.VMEM((ts, MIX_WIDTH), BF16),
            pltpu.VMEM((ts, MIX_WIDTH), BF16),
            pltpu.VMEM((ts, D_MODEL), F32),
            pltpu.VMEM((ts, D_FF), BF16),
            pltpu.VMEM((ts, D_MODEL), F32),
            pltpu.VMEM((MIX_WIDTH, D_MODEL), BF16),
            pltpu.VMEM((D_MODEL, D_FF), BF16),
            pltpu.VMEM((D_MODEL, D_FF), BF16),
            pltpu.VMEM((D_FF, D_MODEL), BF16),
            pltpu.SemaphoreType.DMA((N_FFN_WEIGHTS,)),
        ],
        compiler_params=pltpu.CompilerParams(
            dimension_semantics=("arbitrary",), vmem_limit_bytes=V7X_VMEM_LIMIT_BYTES),
        name="mixer_ffn",
    )(sinks, qd, kit, ks, hv, gate, dec, bq, kp, kp, vt, vt, attn_norm_g,
      x2d, w_out, ln2_g, w_gate, w_up, w_down, final_g)


def kernel(x, ln1_g, w_in, lb_logits, hgrn_norm_g, attn_sinks, attn_norm_g, w_out, ln2_g,
           w_gate, w_up, w_down, final_g):
    bsz, seq, d = x.shape
    x2d = x.reshape(bsz * seq, d)
    *prep, wo16, wg16, wu16, wd16 = _in_proj(x2d, ln1_g[0:1], w_in[0], lb_logits, hgrn_norm_g[0:1],
                                             w_out[0], w_gate[0], w_up[0], w_down[0])
    out = _mixer_ffn(prep, attn_sinks[0], attn_norm_g[0:1], x2d, wo16, ln2_g[0:1],
                     wg16, wu16, wd16, final_g.reshape(1, d), seq)
    return out.reshape(bsz, seq, d)
```

```python
import functools

import jax
import jax.numpy as jnp
from jax import lax
from jax.experimental import pallas as pl
from jax.experimental.pallas import tpu as pltpu

D_MODEL = 1024
A_HEADS = 4
A_DK = 128
A_DV = 128
A_QK = A_HEADS * A_DK
A_WIDTH = A_HEADS * A_DV
CHUNK = 64
B_HEADS = 8
B_KV_HEADS = 2
B_HEAD_DIM = 64
B_WIDTH = B_HEADS * B_HEAD_DIM
B_KV = B_KV_HEADS * B_HEAD_DIM
WINDOW = 128
BLOCK = 128
MIX_WIDTH = A_WIDTH + B_WIDTH
D_FF = 2816
IN_COLS = 4 * A_QK + B_WIDTH + 2 * B_KV
EPS = 1e-6
NEG_BIG = -1e30
LANES = 128
KEY_VARIANTS = 2 * B_KV_HEADS
N_FFN_WEIGHTS = 4

F32 = jnp.float32
BF16 = jnp.bfloat16

V7X_VMEM_LIMIT_BYTES = 56 * 1024 * 1024

TM_PROJ = 512
TS_MIX = 512
FF_CHUNK = 256
HGRN_GROUP_CHUNKS = 2
PAIR = 2 * CHUNK
CAST_ROWS = 128
LOG2E = 1.4426950408889634

NT_DIMS = (((1,), (1,)), ((), ()))
TN_DIMS = (((0,), (0,)), ((), ()))


def _rms_scale(v):
    return lax.rsqrt(jnp.mean(v * v, axis=-1, keepdims=True) + EPS)


def _silu(v):
    return v * (1.0 / (1.0 + jnp.exp(-v)))


def _sigmoid(v):
    return 1.0 / (1.0 + jnp.exp(-v))


def _in_proj_kernel(x_ref, g_ref, w32_ref, lbl_ref, hng_ref, wo32_ref, wg32_ref, wu32_ref, wd32_ref,
                    qd_ref, kit_ref, ks_ref, hv_ref, gate_ref, dec_ref, bq_ref, kp_ref, vt_ref,
                    wo16_ref, wg16_ref, wu16_ref, wd16_ref, w_ref, *, n_down_slabs):
    tm = x_ref.shape[0]

    @pl.when(pl.program_id(0) == 0)
    def _():
        for r0 in range(0, D_MODEL, CAST_ROWS):
            w_ref[r0:r0 + CAST_ROWS, :] = w32_ref[r0:r0 + CAST_ROWS, :].astype(BF16)

    @pl.when(pl.program_id(0) < n_down_slabs)
    def _():
        wd16_ref[...] = wd32_ref[...].astype(BF16)

    u_parts, pf_parts = [], []
    for r0 in (0, tm // 2):
        xh = x_ref[r0:r0 + tm // 2, :]
        uh = (xh * _rms_scale(xh) * g_ref[...]).astype(BF16)
        u_parts.append(uh)
        pf_parts.append(jnp.dot(uh, w_ref[:, A_QK:2 * A_QK], preferred_element_type=F32))
    u = jnp.concatenate(u_parts, axis=0)
    pf = jnp.concatenate(pf_parts, axis=0)

    lbl = lbl_ref[...]
    lmax = jnp.max(lbl, axis=0, keepdims=True)
    le = jnp.exp(lbl - lmax)
    lb = le[0:1, :] / jnp.sum(le, axis=0, keepdims=True)

    r = lax.broadcasted_iota(jnp.int32, (CHUNK, CHUNK), 0)
    c = lax.broadcasted_iota(jnp.int32, (CHUNK, CHUNK), 1)
    tri = jnp.where(r >= c, 1.0, 0.0).astype(BF16)

    pq = jnp.dot(u, w_ref[:, 0:A_QK], preferred_element_type=F32)
    def attention_part():
        wo16_ref[...] = wo32_ref[...].astype(BF16)
        wg16_ref[...] = wg32_ref[...].astype(BF16)
        wu16_ref[...] = wu32_ref[...].astype(BF16)

        c0 = 2 * A_QK + 2 * A_WIDTH
        pa = jnp.dot(u, w_ref[:, c0:], preferred_element_type=F32)
        bq_ref[...] = (pa[:, :B_WIDTH] * (B_HEAD_DIM ** -0.5 * LOG2E)).astype(BF16)
        lo = lax.broadcasted_iota(jnp.int32, (tm, LANES), 1) < B_HEAD_DIM
        k_all = pa[:, B_WIDTH:B_WIDTH + B_KV]
        k_rot = pltpu.roll(k_all, B_HEAD_DIM, axis=1)
        kp_ref[0] = jnp.where(lo, k_all, 0.0).astype(BF16)
        kp_ref[1] = jnp.where(lo, 0.0, k_rot).astype(BF16)
        kp_ref[2] = jnp.where(lo, k_rot, 0.0).astype(BF16)
        kp_ref[3] = jnp.where(lo, 0.0, k_all).astype(BF16)
        vt_ref[...] = pa[:, B_WIDTH + B_KV:].T.astype(BF16)

    def gate_part():
        pg = jnp.dot(u, w_ref[:, 2 * A_QK + A_WIDTH:2 * A_QK + 2 * A_WIDTH], preferred_element_type=F32)
        gate_ref[...] = (_silu(pg) * hng_ref[...]).astype(BF16)

    def value_part():
        pi = jnp.dot(u, w_ref[:, 2 * A_QK:2 * A_QK + A_WIDTH], preferred_element_type=F32)
        hv_ref[...] = pi.astype(BF16)

    k_pair = []
    for ci in range(tm // CHUNK):
        rows = slice(ci * CHUNK, (ci + 1) * CHUNK)
        forget = lb + (1.0 - lb) * _sigmoid(pf[rows])
        key = 1.0 - forget
        logf = jnp.log(forget)
        p0 = logf.astype(BF16)
        r1 = logf - p0.astype(F32)
        p1 = r1.astype(BF16)
        p2 = (r1 - p1.astype(F32)).astype(BF16)
        bc3 = jnp.dot(tri, jnp.concatenate([p0, p1, p2], axis=-1), preferred_element_type=F32)
        bcum = bc3[:, :A_QK] + bc3[:, A_QK:2 * A_QK] + bc3[:, 2 * A_QK:]
        decay = jnp.exp(bcum[CHUNK - 1:CHUNK, :])
        k_intra = key * jnp.exp(-bcum)
        qd_ref[rows, :] = (_silu(pq[rows]) * jnp.exp(bcum)).astype(BF16)
        ks_ref[rows, :] = (k_intra * decay).astype(BF16)
        dec_ref[ci:ci + 1, :] = decay
        k_pair.append(k_intra)
        if len(k_pair) == PAIR // CHUNK:
            p0_row = (ci + 1) * CHUNK - PAIR
            kit_ref[:, p0_row:p0_row + PAIR] = jnp.concatenate(k_pair, axis=0).T.astype(BF16)
            k_pair = []

    attention_part()
    gate_part()
    value_part()


def _in_proj(x2d, ln1_g, w_in, lb_logits, hgrn_norm_g, w_out, w_gate, w_up, w_down):
    t = x2d.shape[0]
    tm = TM_PROJ
    n_steps = t // tm
    row_blk = lambda width: pl.BlockSpec((tm, width), lambda i: (i, 0))
    const = lambda shape, **kw: pl.BlockSpec(shape, lambda i: (0, 0), **kw)
    bf = lambda shape: jax.ShapeDtypeStruct(shape, BF16)

    def slab(w, rows):
        n_slabs = w.shape[0] // rows
        assert w.shape[0] % rows == 0 and n_slabs <= n_steps, (w.shape, rows)
        return pl.BlockSpec((rows, w.shape[1]), lambda i: (jnp.minimum(i, n_slabs - 1), 0))

    cast_weights = (w_out, w_gate, w_up, w_down)
    even_rows = D_MODEL // n_steps
    slab_rows = (even_rows, even_rows, even_rows, CAST_ROWS)
    slab_specs = lambda: [slab(w, rows) for w, rows in zip(cast_weights, slab_rows)]
    return pl.pallas_call(
        functools.partial(_in_proj_kernel, n_down_slabs=D_FF // CAST_ROWS),
        grid=(n_steps,),
        in_specs=[
            row_blk(D_MODEL),
            const((1, D_MODEL)),
            const((D_MODEL, IN_COLS), pipeline_mode=pl.Buffered(1)),
            const((2, A_QK)),
            const((1, A_WIDTH)),
        ] + slab_specs(),
        out_specs=[
            row_blk(A_QK), pl.BlockSpec((A_QK, tm), lambda i: (0, i)), row_blk(A_QK),
            row_blk(A_WIDTH), row_blk(A_WIDTH),
            pl.BlockSpec((tm // CHUNK, A_QK), lambda i: (i, 0)),
            row_blk(B_WIDTH),
            pl.BlockSpec((KEY_VARIANTS, tm, LANES), lambda i: (0, i, 0)),
            pl.BlockSpec((LANES, tm), lambda i: (0, i)),
        ] + slab_specs(),
        out_shape=[
            bf((t, A_QK)), bf((A_QK, t)), bf((t, A_QK)), bf((t, A_WIDTH)), bf((t, A_WIDTH)),
            jax.ShapeDtypeStruct((t // CHUNK, A_QK), F32),
            bf((t, B_WIDTH)),
            bf((KEY_VARIANTS, t, LANES)),
            bf((LANES, t)),
        ] + [bf(w.shape) for w in cast_weights],
        scratch_shapes=[pltpu.VMEM((D_MODEL, IN_COLS), BF16)],
        compiler_params=pltpu.CompilerParams(
            dimension_semantics=("arbitrary",), vmem_limit_bytes=V7X_VMEM_LIMIT_BYTES),
        name="in_proj",
    )(x2d, ln1_g, w_in, lb_logits, hgrn_norm_g, *cast_weights)


def _bias_table_init(bias_ref):
    kj = lax.broadcasted_iota(jnp.int32, (2 * BLOCK, BLOCK), 0)
    qi = lax.broadcasted_iota(jnp.int32, (2 * BLOCK, BLOCK), 1)
    dist = qi + BLOCK - kj
    valid = (dist >= 0) & (dist < WINDOW)
    valid_first = valid & (kj >= BLOCK)
    distf = dist.astype(F32)
    for h in range(B_HEADS):
        slope = 2.0 ** (-(8.0 / B_HEADS) * (h + 1))
        sc = (-slope * LOG2E) * distf
        bias_ref[0, h] = jnp.where(valid, sc, NEG_BIG)
        bias_ref[1, h] = jnp.where(valid_first, sc, NEG_BIG)


def _mixer_stages(first, sinks_ref, qd_ref, kit_ref, ks_ref, hv_ref, gate_ref, dec_ref, bq_ref,
                  kp_ref, kph_ref, vt_ref, vth_ref, ang_ref, y_ref,
                  state_ref, upd_ref, sprev_ref, bias_ref):
    ts = qd_ref.shape[0]
    n_chunk = ts // CHUNK
    n_blk = ts // BLOCK
    head_cols = [slice(h * A_DK, (h + 1) * A_DK) for h in range(A_HEADS)]

    def chunk_units(c_lo, c_hi):
        return [(ci, slice(ci * CHUNK, (ci + 1) * CHUNK), head_cols[h], h)
                for ci in range(c_lo, c_hi) for h in range(A_HEADS)]

    def hgrn_updates(c_lo, c_hi):
        units = chunk_units(c_lo, c_hi)
        upds = [lax.dot_general(hv_ref[rows, cs], ks_ref[rows, cs], TN_DIMS,
                                preferred_element_type=F32) for (_, rows, cs, _) in units]
        for upd, (ci, _, _, h) in zip(upds, units):
            upd_ref[ci, h] = upd

    def hgrn_scan(h):
        st = state_ref[h]
        for ci in range(n_chunk):
            sprev_ref[ci, h] = st.T.astype(BF16)
            st = st * dec_ref[ci:ci + 1, head_cols[h]] + upd_ref[ci, h]
        state_ref[h] = st

    def hgrn_scores(p, box):
        rows = slice(p * PAIR, (p + 1) * PAIR)
        box["a"] = [jnp.dot(qd_ref[rows, cs], kit_ref[cs, rows], preferred_element_type=F32)
                    for cs in head_cols]
        box["o_inter"] = [
            jnp.concatenate(
                [jnp.dot(qd_ref[ci * CHUNK:(ci + 1) * CHUNK, cs], sprev_ref[ci, h],
                         preferred_element_type=F32)
                 for ci in range(p * PAIR // CHUNK, (p + 1) * PAIR // CHUNK)], axis=0)
            for h, cs in enumerate(head_cols)]

    def hgrn_values(p, box):
        rows = slice(p * PAIR, (p + 1) * PAIR)
        r = lax.broadcasted_iota(jnp.int32, (PAIR, PAIR), 0)
        c = lax.broadcasted_iota(jnp.int32, (PAIR, PAIR), 1)
        same_chunk_causal = (c <= r) & (c >= (r // CHUNK) * CHUNK)
        a_s = [jnp.where(same_chunk_causal, a, 0.0).astype(BF16) for a in box["a"]]
        box["o"] = [jnp.dot(a, hv_ref[rows, cs], preferred_element_type=F32) + oi
                    for a, oi, cs in zip(a_s, box["o_inter"], head_cols)]

    def hgrn_store(p, box):
        rows = slice(p * PAIR, (p + 1) * PAIR)
        for o, cs in zip(box["o"], head_cols):
            gate = gate_ref[rows, cs].astype(F32)
            y_ref[rows, cs] = (o * _rms_scale(o) * gate).astype(y_ref.dtype)

    swa_units = [(kvh, par) for kvh in range(B_KV_HEADS) for par in range(2)]

    def swa_scores(j, box):
        s_ts = []
        for kvh, par in swa_units:
            rows = slice(j * BLOCK, (j + 1) * BLOCK)
            q_pairs = jnp.concatenate(
                [bq_ref[rows, (2 * kvh) * LANES:(2 * kvh + 1) * LANES],
                 bq_ref[rows, (2 * kvh + 1) * LANES:(2 * kvh + 2) * LANES]], axis=0)
            idx = 2 * kvh + par
            if j == 0:
                kk = jnp.concatenate([kph_ref[idx], kp_ref[idx, 0:BLOCK, :]], axis=0)
            else:
                kk = kp_ref[idx, (j - 1) * BLOCK:(j + 1) * BLOCK, :]
            s_ts.append(lax.dot_general(kk, q_pairs, NT_DIMS, preferred_element_type=F32))
        box["s"] = s_ts

    def swa_softmax(j, box):
        p_ts, inv_ls = [], []
        for (kvh, par), s_t in zip(swa_units, box["s"]):
            p_halves = []
            for half, h in enumerate((4 * kvh + par, 4 * kvh + 2 + par)):
                bias = bias_ref[first, h] if j == 0 else bias_ref[0, h]
                s = s_t[:, half * BLOCK:(half + 1) * BLOCK] + bias
                sink = sinks_ref[h] * LOG2E
                m = jnp.maximum(jnp.max(s, axis=0, keepdims=True), sink)
                p = jnp.exp2(s - m)
                inv_ls.append(1.0 / (jnp.sum(p, axis=0, keepdims=True) + jnp.exp2(sink - m)))
                p_halves.append(p.astype(BF16))
            p_ts.append(jnp.concatenate(p_halves, axis=1))
        box["p"], box["inv_l"] = p_ts, inv_ls

    def swa_values(j, box):
        o_t = [None] * B_HEADS
        for ui, ((kvh, par), p_t) in enumerate(zip(swa_units, box["p"])):
            drow = slice(kvh * B_HEAD_DIM, (kvh + 1) * B_HEAD_DIM)
            if j == 0:
                vt_h = jnp.concatenate([vth_ref[drow, :], vt_ref[drow, 0:BLOCK]], axis=1)
            else:
                vt_h = vt_ref[drow, (j - 1) * BLOCK:(j + 1) * BLOCK]
            pv = jnp.dot(vt_h, p_t, preferred_element_type=F32)
            for half, h in enumerate((4 * kvh + par, 4 * kvh + 2 + par)):
                o_t[h] = pv[:, half * BLOCK:(half + 1) * BLOCK] * box["inv_l"][2 * ui + half]
        box["o_t"] = o_t

    def swa_store(j, box):
        o_all_t = jnp.concatenate(box["o_t"], axis=0)
        scale = lax.rsqrt(jnp.sum(o_all_t * o_all_t, axis=0, keepdims=True) * (1.0 / B_WIDTH) + EPS)
        o_all = (o_all_t * scale).T
        y_ref[j * BLOCK:(j + 1) * BLOCK, A_WIDTH:] = (o_all * ang_ref[...]).astype(y_ref.dtype)

    group = HGRN_GROUP_CHUNKS
    hgrn = [functools.partial(hgrn_updates, c, c + group) for c in range(0, n_chunk, group)]
    hgrn += [functools.partial(hgrn_scan, h) for h in range(A_HEADS)]
    for p in range(ts // PAIR):
        box = {}
        hgrn += [functools.partial(f, p, box) for f in (hgrn_scores, hgrn_values, hgrn_store)]
    swa = []
    for j in range(n_blk):
        box = {}
        swa += [functools.partial(f, j, box) for f in (swa_scores, swa_softmax, swa_values, swa_store)]
    stages = []
    for k in range(max(len(hgrn), len(swa))):
        stages += hgrn[k:k + 1] + swa[k:k + 1]
    return stages


def _ffn_stages(x_ref, y_ref, wo_ref, g2_ref, wg_ref, wu_ref, wd_ref, h_ref, act_ref, h2_ref,
                before_down=()):
    box = {}

    def head():
        half = x_ref.shape[0] // 2
        u_parts = []
        for r0 in (0, half):
            rows = slice(r0, r0 + half)
            h = x_ref[rows, :] + jnp.dot(y_ref[rows, :], wo_ref[...], preferred_element_type=F32)
            u_parts.append((h * _rms_scale(h) * g2_ref[...]).astype(BF16))
            h_ref[rows, :] = h
        box["u"] = jnp.concatenate(u_parts, axis=0)

    def ff_up(c0):
        cs = slice(c0, c0 + FF_CHUNK)
        box["gate"] = jnp.dot(box["u"], wg_ref[:, cs], preferred_element_type=F32)
        box["up"] = jnp.dot(box["u"], wu_ref[:, cs], preferred_element_type=F32)

    def ff_act(c0):
        act_ref[:, c0:c0 + FF_CHUNK] = (_silu(box["gate"]) * box["up"]).astype(BF16)

    def down():
        h2_ref[...] = h_ref[...] + jnp.dot(act_ref[...], wd_ref[...], preferred_element_type=F32)

    stages = [head]
    for c0 in range(0, D_FF, FF_CHUNK):
        stages += [functools.partial(ff_up, c0), functools.partial(ff_act, c0)]
    return stages + list(before_down) + [down]


def _final_norm(h2_ref, gf_ref, o_ref):
    h2 = h2_ref[...]
    o_ref[...] = h2 * _rms_scale(h2) * gf_ref[...]


def _run_interleaved(major, minor):
    span = max(len(major) - 2, 1)
    done = 0
    for k, stage in enumerate(major):
        stage()
        want = min(len(minor), -(-(k + 1) * len(minor) // span))
        while done < want:
            minor[done]()
            done += 1


def _mixer_ffn_kernel(sinks_ref, qd_ref, kit_ref, ks_ref, hv_ref, gate_ref, dec_ref, bq_ref,
                      kp_ref, kph_ref, vt_ref, vth_ref, ang_ref,
                      x_ref, wo_hbm, g2_ref, wg_hbm, wu_hbm, wd_hbm, gf_ref, o_ref,
                      state_ref, upd_ref, sprev_ref, bias_ref, y_cur, y_prev, h_ref, act_ref, h2_ref,
                      wo_ref, wg_ref, wu_ref, wd_ref, w_sems, *, n_tiles, n_s):
    i = pl.program_id(0)
    seq_start = lax.rem(i, n_s) == 0
    weight_copies = [
        pltpu.make_async_copy(src, dst, w_sems.at[k])
        for k, (src, dst) in enumerate(((wo_hbm, wo_ref), (wg_hbm, wg_ref),
                                        (wu_hbm, wu_ref), (wd_hbm, wd_ref)))]

    @pl.when(i == 0)
    def _():
        for copy in weight_copies:
            copy.start()
        _bias_table_init(bias_ref)

    @pl.when(i == 1)
    def _():
        for copy in weight_copies:
            copy.wait()

    @pl.when(seq_start)
    def _():
        state_ref[...] = jnp.zeros_like(state_ref)

    first = jnp.where(seq_start, 1, 0)
    mixer_args = (first, sinks_ref, qd_ref, kit_ref, ks_ref, hv_ref, gate_ref, dec_ref, bq_ref,
                  kp_ref, kph_ref, vt_ref, vth_ref, ang_ref, y_cur,
                  state_ref, upd_ref, sprev_ref, bias_ref)
    ffn_args = (x_ref, y_prev, wo_ref, g2_ref, wg_ref, wu_ref, wd_ref, h_ref, act_ref, h2_ref)
    final_norm = functools.partial(_final_norm, h2_ref, gf_ref, o_ref)

    @pl.when(i == 0)
    def _():
        _run_interleaved(_mixer_stages(*mixer_args), [])
        y_prev[...] = y_cur[...]

    @pl.when(i == 1)
    def _():
        _run_interleaved(_ffn_stages(*ffn_args), _mixer_stages(*mixer_args))
        y_prev[...] = y_cur[...]

    @pl.when((i > 1) & (i < n_tiles))
    def _():
        _run_interleaved(_ffn_stages(*ffn_args, before_down=[final_norm]),
                         _mixer_stages(*mixer_args))
        y_prev[...] = y_cur[...]

    @pl.when(i == n_tiles)
    def _():
        _run_interleaved(_ffn_stages(*ffn_args, before_down=[final_norm]), [])

    @pl.when(i == n_tiles + 1)
    def _():
        final_norm()


def _mixer_ffn(prep, sinks, attn_norm_g, x2d, w_out, ln2_g, w_gate, w_up, w_down, final_g, seq):
    qd, kit, ks, hv, gate, dec, bq, kp, vt = prep
    t = x2d.shape[0]
    ts = TS_MIX
    n_tiles = t // ts
    n_s = seq // ts
    n_chunk = ts // CHUNK
    n_blk = ts // BLOCK
    cur = lambda i: jnp.minimum(i, n_tiles - 1)
    prev = lambda i: jnp.clip(i - 1, 0, n_tiles - 1)
    last = lambda i: jnp.clip(i - 2, 0, n_tiles - 1)
    halo = lambda i: jnp.maximum(cur(i) * n_blk - 1, 0)
    row_blk = lambda width: pl.BlockSpec((ts, width), lambda i: (cur(i), 0))
    const = lambda shape: pl.BlockSpec(shape, lambda i: (0, 0), pipeline_mode=pl.Buffered(1))
    in_hbm = pl.BlockSpec(memory_space=pl.ANY)
    assert n_tiles >= 2
    return pl.pallas_call(
        functools.partial(_mixer_ffn_kernel, n_tiles=n_tiles, n_s=n_s),
        grid=(n_tiles + 2,),
        in_specs=[
            pl.BlockSpec(memory_space=pltpu.SMEM),
            row_blk(A_QK), pl.BlockSpec((A_QK, ts), lambda i: (0, cur(i))), row_blk(A_QK),
            row_blk(A_WIDTH), row_blk(A_WIDTH),
            pl.BlockSpec((n_chunk, A_QK), lambda i: (cur(i), 0)),
            row_blk(B_WIDTH),
            pl.BlockSpec((KEY_VARIANTS, ts, LANES), lambda i: (0, cur(i), 0)),
            pl.BlockSpec((KEY_VARIANTS, BLOCK, LANES), lambda i: (0, halo(i), 0)),
            pl.BlockSpec((LANES, ts), lambda i: (0, cur(i))),
            pl.BlockSpec((LANES, BLOCK), lambda i: (0, halo(i))),
            pl.BlockSpec((1, B_WIDTH), lambda i: (0, 0)),
            pl.BlockSpec((ts, D_MODEL), lambda i: (prev(i), 0)),
            in_hbm,
            const((1, D_MODEL)),
            in_hbm, in_hbm, in_hbm,
            const((1, D_MODEL)),
        ],
        out_specs=pl.BlockSpec((ts, D_MODEL), lambda i: (last(i), 0)),
        out_shape=jax.ShapeDtypeStruct((t, D_MODEL), F32),
        scratch_shapes=[
            pltpu.VMEM((A_HEADS, A_DV, A_DK), F32),
            pltpu.VMEM((n_chunk, A_HEADS, A_DV, A_DK), F32),
            pltpu.VMEM((n_chunk, A_HEADS, A_DV, A_DK), BF16),
            pltpu.VMEM((2, B_HEADS, 2 * BLOCK, BLOCK), F32),
            pltpu.VMEM((ts, MIX_WIDTH), BF16),
            pltpu.VMEM((ts, MIX_WIDTH), BF16),
            pltpu.VMEM((ts, D_MODEL), F32),
            pltpu.VMEM((ts, D_FF), BF16),
            pltpu.VMEM((ts, D_MODEL), F32),
            pltpu.VMEM((MIX_WIDTH, D_MODEL), BF16),
            pltpu.VMEM((D_MODEL, D_FF), BF16),
            pltpu.VMEM((D_MODEL, D_FF), BF16),
            pltpu.VMEM((D_FF, D_MODEL), BF16),
            pltpu.SemaphoreType.DMA((N_FFN_WEIGHTS,)),
        ],
        compiler_params=pltpu.CompilerParams(
            dimension_semantics=("arbitrary",), vmem_limit_bytes=V7X_VMEM_LIMIT_BYTES),
        name="mixer_ffn",
    )(sinks, qd, kit, ks, hv, gate, dec, bq, kp, kp, vt, vt, attn_norm_g,
      x2d, w_out, ln2_g, w_gate, w_up, w_down, final_g)


def kernel(x, ln1_g, w_in, lb_logits, hgrn_norm_g, attn_sinks, attn_norm_g, w_out, ln2_g,
           w_gate, w_up, w_down, final_g):
    bsz, seq, d = x.shape
    x2d = x.reshape(bsz * seq, d)
    *prep, wo16, wg16, wu16, wd16 = _in_proj(x2d, ln1_g[0:1], w_in[0], lb_logits, hgrn_norm_g[0:1],
                                             w_out[0], w_gate[0], w_up[0], w_down[0])
    out = _mixer_ffn(prep, attn_sinks[0], attn_norm_g[0:1], x2d, wo16, ln2_g[0:1],
                     wg16, wu16, wd16, final_g.reshape(1, d), seq)
    return out.reshape(bsz, seq, d)
```

```python
import functools

import jax
import jax.numpy as jnp
from jax import lax
from jax.experimental import pallas as pl
from jax.experimental.pallas import tpu as pltpu

D_MODEL = 1024
A_HEADS = 4
A_DK = 128
A_DV = 128
A_QK = A_HEADS * A_DK
A_WIDTH = A_HEADS * A_DV
CHUNK = 64
B_HEADS = 8
B_KV_HEADS = 2
B_HEAD_DIM = 64
B_WIDTH = B_HEADS * B_HEAD_DIM
B_KV = B_KV_HEADS * B_HEAD_DIM
WINDOW = 128
BLOCK = 128
MIX_WIDTH = A_WIDTH + B_WIDTH
D_FF = 2816
IN_COLS = 4 * A_QK + B_WIDTH + 2 * B_KV
EPS = 1e-6
NEG_BIG = -1e30
LANES = 128
KEY_VARIANTS = 2 * B_KV_HEADS
N_FFN_WEIGHTS = 4

F32 = jnp.float32
BF16 = jnp.bfloat16

V7X_VMEM_LIMIT_BYTES = 56 * 1024 * 1024

TM_PROJ = 512
TS_MIX = 512
FF_CHUNK = 256
HGRN_GROUP_CHUNKS = 2
PAIR = 2 * CHUNK
CAST_ROWS = 128
LOG2E = 1.4426950408889634

NT_DIMS = (((1,), (1,)), ((), ()))
TN_DIMS = (((0,), (0,)), ((), ()))


def _rms_scale(v):
    return lax.rsqrt(jnp.mean(v * v, axis=-1, keepdims=True) + EPS)


def _silu(v):
    return v * (1.0 / (1.0 + jnp.exp(-v)))


def _sigmoid(v):
    return 1.0 / (1.0 + jnp.exp(-v))


def _in_proj_kernel(x_ref, g_ref, w32_ref, lbl_ref, hng_ref, wo32_ref, wg32_ref, wu32_ref, wd32_ref,
                    qd_ref, kit_ref, ks_ref, hv_ref, gate_ref, dec_ref, bq_ref, kp_ref, vt_ref,
                    wo16_ref, wg16_ref, wu16_ref, wd16_ref, w_ref, *, n_down_slabs):
    tm = x_ref.shape[0]

    @pl.when(pl.program_id(0) == 0)
    def _():
        for r0 in range(0, D_MODEL, CAST_ROWS):
            w_ref[r0:r0 + CAST_ROWS, :] = w32_ref[r0:r0 + CAST_ROWS, :].astype(BF16)

    @pl.when(pl.program_id(0) < n_down_slabs)
    def _():
        wd16_ref[...] = wd32_ref[...].astype(BF16)

    u_parts, pf_parts = [], []
    for r0 in (0, tm // 2):
        xh = x_ref[r0:r0 + tm // 2, :]
        uh = (xh * _rms_scale(xh) * g_ref[...]).astype(BF16)
        u_parts.append(uh)
        pf_parts.append(jnp.dot(uh, w_ref[:, A_QK:2 * A_QK], preferred_element_type=F32))
    u = jnp.concatenate(u_parts, axis=0)
    pf = jnp.concatenate(pf_parts, axis=0)

    lbl = lbl_ref[...]
    lmax = jnp.max(lbl, axis=0, keepdims=True)
    le = jnp.exp(lbl - lmax)
    lb = le[0:1, :] / jnp.sum(le, axis=0, keepdims=True)

    r = lax.broadcasted_iota(jnp.int32, (CHUNK, CHUNK), 0)
    c = lax.broadcasted_iota(jnp.int32, (CHUNK, CHUNK), 1)
    tri = jnp.where(r >= c, 1.0, 0.0).astype(BF16)

    pq = jnp.dot(u, w_ref[:, 0:A_QK], preferred_element_type=F32)
    def attention_part():
        wo16_ref[...] = wo32_ref[...].astype(BF16)
        wg16_ref[...] = wg32_ref[...].astype(BF16)
        wu16_ref[...] = wu32_ref[...].astype(BF16)

        c0 = 2 * A_QK + 2 * A_WIDTH
        pa = jnp.dot(u, w_ref[:, c0:], preferred_element_type=F32)
        bq_ref[...] = (pa[:, :B_WIDTH] * (B_HEAD_DIM ** -0.5 * LOG2E)).astype(BF16)
        lo = lax.broadcasted_iota(jnp.int32, (tm, LANES), 1) < B_HEAD_DIM
        k_all = pa[:, B_WIDTH:B_WIDTH + B_KV]
        k_rot = pltpu.roll(k_all, B_HEAD_DIM, axis=1)
        kp_ref[0] = jnp.where(lo, k_all, 0.0).astype(BF16)
        kp_ref[1] = jnp.where(lo, 0.0, k_rot).astype(BF16)
        kp_ref[2] = jnp.where(lo, k_rot, 0.0).astype(BF16)
        kp_ref[3] = jnp.where(lo, 0.0, k_all).astype(BF16)
        vt_ref[...] = pa[:, B_WIDTH + B_KV:].T.astype(BF16)

    def gate_part():
        pg = jnp.dot(u, w_ref[:, 2 * A_QK + A_WIDTH:2 * A_QK + 2 * A_WIDTH], preferred_element_type=F32)
        gate_ref[...] = (_silu(pg) * hng_ref[...]).astype(BF16)

    def value_part():
        pi = jnp.dot(u, w_ref[:, 2 * A_QK:2 * A_QK + A_WIDTH], preferred_element_type=F32)
        hv_ref[...] = pi.astype(BF16)

    k_pair = []
    for ci in range(tm // CHUNK):
        rows = slice(ci * CHUNK, (ci + 1) * CHUNK)
        forget = lb + (1.0 - lb) * _sigmoid(pf[rows])
        key = 1.0 - forget
        logf = jnp.log(forget)
        p0 = logf.astype(BF16)
        r1 = logf - p0.astype(F32)
        p1 = r1.astype(BF16)
        p2 = (r1 - p1.astype(F32)).astype(BF16)
        bc3 = jnp.dot(tri, jnp.concatenate([p0, p1, p2], axis=-1), preferred_element_type=F32)
        bcum = bc3[:, :A_QK] + bc3[:, A_QK:2 * A_QK] + bc3[:, 2 * A_QK:]
        decay = jnp.exp(bcum[CHUNK - 1:CHUNK, :])
        k_intra = key * jnp.exp(-bcum)
        qd_ref[rows, :] = (_silu(pq[rows]) * jnp.exp(bcum)).astype(BF16)
        ks_ref[rows, :] = (k_intra * decay).astype(BF16)
        dec_ref[ci:ci + 1, :] = decay
        k_pair.append(k_intra)
        if len(k_pair) == PAIR // CHUNK:
            p0_row = (ci + 1) * CHUNK - PAIR
            kit_ref[:, p0_row:p0_row + PAIR] = jnp.concatenate(k_pair, axis=0).T.astype(BF16)
            k_pair = []

    attention_part()
    gate_part()
    value_part()


def _in_proj(x2d, ln1_g, w_in, lb_logits, hgrn_norm_g, w_out, w_gate, w_up, w_down):
    t = x2d.shape[0]
    tm = TM_PROJ
    n_steps = t // tm
    row_blk = lambda width: pl.BlockSpec((tm, width), lambda i: (i, 0))
    const = lambda shape, **kw: pl.BlockSpec(shape, lambda i: (0, 0), **kw)
    bf = lambda shape: jax.ShapeDtypeStruct(shape, BF16)

    def slab(w, rows):
        n_slabs = w.shape[0] // rows
        assert w.shape[0] % rows == 0 and n_slabs <= n_steps, (w.shape, rows)
        return pl.BlockSpec((rows, w.shape[1]), lambda i: (jnp.minimum(i, n_slabs - 1), 0))

    cast_weights = (w_out, w_gate, w_up, w_down)
    even_rows = D_MODEL // n_steps
    slab_rows = (even_rows, even_rows, even_rows, CAST_ROWS)
    slab_specs = lambda: [slab(w, rows) for w, rows in zip(cast_weights, slab_rows)]
    return pl.pallas_call(
        functools.partial(_in_proj_kernel, n_down_slabs=D_FF // CAST_ROWS),
        grid=(n_steps,),
        in_specs=[
            row_blk(D_MODEL),
            const((1, D_MODEL)),
            const((D_MODEL, IN_COLS), pipeline_mode=pl.Buffered(1)),
            const((2, A_QK)),
            const((1, A_WIDTH)),
        ] + slab_specs(),
        out_specs=[
            row_blk(A_QK), pl.BlockSpec((A_QK, tm), lambda i: (0, i)), row_blk(A_QK),
            row_blk(A_WIDTH), row_blk(A_WIDTH),
            pl.BlockSpec((tm // CHUNK, A_QK), lambda i: (i, 0)),
            row_blk(B_WIDTH),
            pl.BlockSpec((KEY_VARIANTS, tm, LANES), lambda i: (0, i, 0)),
            pl.BlockSpec((LANES, tm), lambda i: (0, i)),
        ] + slab_specs(),
        out_shape=[
            bf((t, A_QK)), bf((A_QK, t)), bf((t, A_QK)), bf((t, A_WIDTH)), bf((t, A_WIDTH)),
            jax.ShapeDtypeStruct((t // CHUNK, A_QK), F32),
            bf((t, B_WIDTH)),
            bf((KEY_VARIANTS, t, LANES)),
            bf((LANES, t)),
        ] + [bf(w.shape) for w in cast_weights],
        scratch_shapes=[pltpu.VMEM((D_MODEL, IN_COLS), BF16)],
        compiler_params=pltpu.CompilerParams(
            dimension_semantics=("arbitrary",), vmem_limit_bytes=V7X_VMEM_LIMIT_BYTES),
        name="in_proj",
    )(x2d, ln1_g, w_in, lb_logits, hgrn_norm_g, *cast_weights)


def _bias_table_init(bias_ref):
    kj = lax.broadcasted_iota(jnp.int32, (2 * BLOCK, BLOCK), 0)
    qi = lax.broadcasted_iota(jnp.int32, (2 * BLOCK, BLOCK), 1)
    dist = qi + BLOCK - kj
    valid = (dist >= 0) & (dist < WINDOW)
    valid_first = valid & (kj >= BLOCK)
    distf = dist.astype(F32)
    for h in range(B_HEADS):
        slope = 2.0 ** (-(8.0 / B_HEADS) * (h + 1))
        sc = (-slope * LOG2E) * distf
        bias_ref[0, h] = jnp.where(valid, sc, NEG_BIG)
        bias_ref[1, h] = jnp.where(valid_first, sc, NEG_BIG)


def _mixer_stages(first, sinks_ref, qd_ref, kit_ref, ks_ref, hv_ref, gate_ref, dec_ref, bq_ref,
                  kp_ref, kph_ref, vt_ref, vth_ref, ang_ref, y_ref,
                  state_ref, upd_ref, sprev_ref, bias_ref):
    ts = qd_ref.shape[0]
    n_chunk = ts // CHUNK
    n_blk = ts // BLOCK
    head_cols = [slice(h * A_DK, (h + 1) * A_DK) for h in range(A_HEADS)]

    def chunk_units(c_lo, c_hi):
        return [(ci, slice(ci * CHUNK, (ci + 1) * CHUNK), head_cols[h], h)
                for ci in range(c_lo, c_hi) for h in range(A_HEADS)]

    def hgrn_updates(c_lo, c_hi):
        units = chunk_units(c_lo, c_hi)
        upds = [lax.dot_general(hv_ref[rows, cs], ks_ref[rows, cs], TN_DIMS,
                                preferred_element_type=F32) for (_, rows, cs, _) in units]
        for upd, (ci, _, _, h) in zip(upds, units):
            upd_ref[ci, h] = upd

    def hgrn_scan(h):
        st = state_ref[h]
        for ci in range(n_chunk):
            sprev_ref[ci, h] = st.T.astype(BF16)
            st = st * dec_ref[ci:ci + 1, head_cols[h]] + upd_ref[ci, h]
        state_ref[h] = st

    def hgrn_scores(p, box):
        rows = slice(p * PAIR, (p + 1) * PAIR)
        box["a"] = [jnp.dot(qd_ref[rows, cs], kit_ref[cs, rows], preferred_element_type=F32)
                    for cs in head_cols]
        box["o_inter"] = [
            jnp.concatenate(
                [jnp.dot(qd_ref[ci * CHUNK:(ci + 1) * CHUNK, cs], sprev_ref[ci, h],
                         preferred_element_type=F32)
                 for ci in range(p * PAIR // CHUNK, (p + 1) * PAIR // CHUNK)], axis=0)
            for h, cs in enumerate(head_cols)]

    def hgrn_values(p, box):
        rows = slice(p * PAIR, (p + 1) * PAIR)
        r = lax.broadcasted_iota(jnp.int32, (PAIR, PAIR), 0)
        c = lax.broadcasted_iota(jnp.int32, (PAIR, PAIR), 1)
        same_chunk_causal = (c <= r) & (c >= (r // CHUNK) * CHUNK)
        a_s = [jnp.where(same_chunk_causal, a, 0.0).astype(BF16) for a in box["a"]]
        box["o"] = [jnp.dot(a, hv_ref[rows, cs], preferred_element_type=F32) + oi
                    for a, oi, cs in zip(a_s, box["o_inter"], head_cols)]

    def hgrn_store(p, box):
        rows = slice(p * PAIR, (p + 1) * PAIR)
        for o, cs in zip(box["o"], head_cols):
            gate = gate_ref[rows, cs].astype(F32)
            y_ref[rows, cs] = (o * _rms_scale(o) * gate).astype(y_ref.dtype)

    swa_units = [(kvh, par) for kvh in range(B_KV_HEADS) for par in range(2)]

    def swa_scores(j, box):
        s_ts = []
        for kvh, par in swa_units:
            rows = slice(j * BLOCK, (j + 1) * BLOCK)
            q_pairs = jnp.concatenate(
                [bq_ref[rows, (2 * kvh) * LANES:(2 * kvh + 1) * LANES],
                 bq_ref[rows, (2 * kvh + 1) * LANES:(2 * kvh + 2) * LANES]], axis=0)
            idx = 2 * kvh + par
            if j == 0:
                kk = jnp.concatenate([kph_ref[idx], kp_ref[idx, 0:BLOCK, :]], axis=0)
            else:
                kk = kp_ref[idx, (j - 1) * BLOCK:(j + 1) * BLOCK, :]
            s_ts.append(lax.dot_general(kk, q_pairs, NT_DIMS, preferred_element_type=F32))
        box["s"] = s_ts

    def swa_softmax(j, box):
        p_ts, inv_ls = [], []
        for (kvh, par), s_t in zip(swa_units, box["s"]):
            p_halves = []
            for half, h in enumerate((4 * kvh + par, 4 * kvh + 2 + par)):
                bias = bias_ref[first, h] if j == 0 else bias_ref[0, h]
                s = s_t[:, half * BLOCK:(half + 1) * BLOCK] + bias
                sink = sinks_ref[h] * LOG2E
                m = jnp.maximum(jnp.max(s, axis=0, keepdims=True), sink)
                p = jnp.exp2(s - m)
                inv_ls.append(1.0 / (jnp.sum(p, axis=0, keepdims=True) + jnp.exp2(sink - m)))
                p_halves.append(p.astype(BF16))
            p_ts.append(jnp.concatenate(p_halves, axis=1))
        box["p"], box["inv_l"] = p_ts, inv_ls

    def swa_values(j, box):
        o_t = [None] * B_HEADS
        for ui, ((kvh, par), p_t) in enumerate(zip(swa_units, box["p"])):
            drow = slice(kvh * B_HEAD_DIM, (kvh + 1) * B_HEAD_DIM)
            if j == 0:
                vt_h = jnp.concatenate([vth_ref[drow, :], vt_ref[drow, 0:BLOCK]], axis=1)
            else:
                vt_h = vt_ref[drow, (j - 1) * BLOCK:(j + 1) * BLOCK]
            pv = jnp.dot(vt_h, p_t, preferred_element_type=F32)
            for half, h in enumerate((4 * kvh + par, 4 * kvh + 2 + par)):
                o_t[h] = pv[:, half * BLOCK:(half + 1) * BLOCK] * box["inv_l"][2 * ui + half]
        box["o_t"] = o_t

    def swa_store(j, box):
        o_all_t = jnp.concatenate(box["o_t"], axis=0)
        scale = lax.rsqrt(jnp.sum(o_all_t * o_all_t, axis=0, keepdims=True) * (1.0 / B_WIDTH) + EPS)
        o_all = (o_all_t * scale).T
        y_ref[j * BLOCK:(j + 1) * BLOCK, A_WIDTH:] = (o_all * ang_ref[...]).astype(y_ref.dtype)

    group = HGRN_GROUP_CHUNKS
    hgrn = [functools.partial(hgrn_updates, c, c + group) for c in range(0, n_chunk, group)]
    hgrn += [functools.partial(hgrn_scan, h) for h in range(A_HEADS)]
    for p in range(ts // PAIR):
        box = {}
        hgrn += [functools.partial(f, p, box) for f in (hgrn_scores, hgrn_values, hgrn_store)]
    swa = []
    for j in range(n_blk):
        box = {}
        swa += [functools.partial(f, j, box) for f in (swa_scores, swa_softmax, swa_values, swa_store)]
    stages = []
    for k in range(max(len(hgrn), len(swa))):
        stages += hgrn[k:k + 1] + swa[k:k + 1]
    return stages


def _ffn_stages(x_ref, y_ref, wo_ref, g2_ref, wg_ref, wu_ref, wd_ref, h_ref, act_ref, h2_ref,
                before_down=()):
    box = {}

    def head():
        half = x_ref.shape[0] // 2
        u_parts = []
        for r0 in (0, half):
            rows = slice(r0, r0 + half)
            h = x_ref[rows, :] + jnp.dot(y_ref[rows, :], wo_ref[...], preferred_element_type=F32)
            u_parts.append((h * _rms_scale(h) * g2_ref[...]).astype(BF16))
            h_ref[rows, :] = h
        box["u"] = jnp.concatenate(u_parts, axis=0)

    def ff_up(c0):
        cs = slice(c0, c0 + FF_CHUNK)
        box["gate"] = jnp.dot(box["u"], wg_ref[:, cs], preferred_element_type=F32)
        box["up"] = jnp.dot(box["u"], wu_ref[:, cs], preferred_element_type=F32)

    def ff_act(c0):
        act_ref[:, c0:c0 + FF_CHUNK] = (_silu(box["gate"]) * box["up"]).astype(BF16)

    def down():
        h2_ref[...] = h_ref[...] + jnp.dot(act_ref[...], wd_ref[...], preferred_element_type=F32)

    stages = [head]
    for c0 in range(0, D_FF, FF_CHUNK):
        stages += [functools.partial(ff_up, c0), functools.partial(ff_act, c0)]
    return stages + list(before_down) + [down]


def _final_norm(h2_ref, gf_ref, o_ref):
    h2 = h2_ref[...]
    o_ref[...] = h2 * _rms_scale(h2) * gf_ref[...]


def _run_interleaved(major, minor):
    span = max(len(major) - 2, 1)
    done = 0
    for k, stage in enumerate(major):
        stage()
        want = min(len(minor), -(-(k + 1) * len(minor) // span))
        while done < want:
            minor[done]()
            done += 1


def _mixer_ffn_kernel(sinks_ref, qd_ref, kit_ref, ks_ref, hv_ref, gate_ref, dec_ref, bq_ref,
                      kp_ref, kph_ref, vt_ref, vth_ref, ang_ref,
                      x_ref, wo_hbm, g2_ref, wg_hbm, wu_hbm, wd_hbm, gf_ref, o_ref,
                      state_ref, upd_ref, sprev_ref, bias_ref, y_cur, y_prev, h_ref, act_ref, h2_ref,
                      wo_ref, wg_ref, wu_ref, wd_ref, w_sems, *, n_tiles, n_s):
    i = pl.program_id(0)
    seq_start = lax.rem(i, n_s) == 0
    weight_copies = [
        pltpu.make_async_copy(src, dst, w_sems.at[k])
        for k, (src, dst) in enumerate(((wo_hbm, wo_ref), (wg_hbm, wg_ref),
                                        (wu_hbm, wu_ref), (wd_hbm, wd_ref)))]

    @pl.when(i == 0)
    def _():
        for copy in weight_copies:
            copy.start()
        _bias_table_init(bias_ref)

    @pl.when(i == 1)
    def _():
        for copy in weight_copies:
            copy.wait()

    @pl.when(seq_start)
    def _():
        state_ref[...] = jnp.zeros_like(state_ref)

    first = jnp.where(seq_start, 1, 0)
    mixer_args = (first, sinks_ref, qd_ref, kit_ref, ks_ref, hv_ref, gate_ref, dec_ref, bq_ref,
                  kp_ref, kph_ref, vt_ref, vth_ref, ang_ref, y_cur,
                  state_ref, upd_ref, sprev_ref, bias_ref)
    ffn_args = (x_ref, y_prev, wo_ref, g2_ref, wg_ref, wu_ref, wd_ref, h_ref, act_ref, h2_ref)
    final_norm = functools.partial(_final_norm, h2_ref, gf_ref, o_ref)

    @pl.when(i == 0)
    def _():
        _run_interleaved(_mixer_stages(*mixer_args), [])
        y_prev[...] = y_cur[...]
        h2_ref[...] = jnp.zeros_like(h2_ref)

    @pl.when((i >= 1) & (i < n_tiles))
    def _():
        _run_interleaved(_ffn_stages(*ffn_args, before_down=[final_norm]),
                         _mixer_stages(*mixer_args))
        y_prev[...] = y_cur[...]

    @pl.when(i == n_tiles)
    def _():
        _run_interleaved(_ffn_stages(*ffn_args, before_down=[final_norm]), [])

    @pl.when(i == n_tiles + 1)
    def _():
        final_norm()


def _mixer_ffn(prep, sinks, attn_norm_g, x2d, w_out, ln2_g, w_gate, w_up, w_down, final_g, seq):
    qd, kit, ks, hv, gate, dec, bq, kp, vt = prep
    t = x2d.shape[0]
    ts = TS_MIX
    n_tiles = t // ts
    n_s = seq // ts
    n_chunk = ts // CHUNK
    n_blk = ts // BLOCK
    cur = lambda i: jnp.minimum(i, n_tiles - 1)
    prev = lambda i: jnp.clip(i - 1, 0, n_tiles - 1)
    last = lambda i: jnp.clip(i - 2, 0, n_tiles - 1)
    halo = lambda i: jnp.maximum(cur(i) * n_blk - 1, 0)
    row_blk = lambda width: pl.BlockSpec((ts, width), lambda i: (cur(i), 0))
    const = lambda shape: pl.BlockSpec(shape, lambda i: (0, 0), pipeline_mode=pl.Buffered(1))
    in_hbm = pl.BlockSpec(memory_space=pl.ANY)
    assert n_tiles >= 2
    return pl.pallas_call(
        functools.partial(_mixer_ffn_kernel, n_tiles=n_tiles, n_s=n_s),
        grid=(n_tiles + 2,),
        in_specs=[
            pl.BlockSpec(memory_space=pltpu.SMEM),
            row_blk(A_QK), pl.BlockSpec((A_QK, ts), lambda i: (0, cur(i))), row_blk(A_QK),
            row_blk(A_WIDTH), row_blk(A_WIDTH),
            pl.BlockSpec((n_chunk, A_QK), lambda i: (cur(i), 0)),
            row_blk(B_WIDTH),
            pl.BlockSpec((KEY_VARIANTS, ts, LANES), lambda i: (0, cur(i), 0)),
            pl.BlockSpec((KEY_VARIANTS, BLOCK, LANES), lambda i: (0, halo(i), 0)),
            pl.BlockSpec((LANES, ts), lambda i: (0, cur(i))),
            pl.BlockSpec((LANES, BLOCK), lambda i: (0, halo(i))),
            pl.BlockSpec((1, B_WIDTH), lambda i: (0, 0)),
            pl.BlockSpec((ts, D_MODEL), lambda i: (prev(i), 0)),
            in_hbm,
            const((1, D_MODEL)),
            in_hbm, in_hbm, in_hbm,
            const((1, D_MODEL)),
        ],
        out_specs=pl.BlockSpec((ts, D_MODEL), lambda i: (last(i), 0)),
        out_shape=jax.ShapeDtypeStruct((t, D_MODEL), F32),
        scratch_shapes=[
            pltpu.VMEM((A_HEADS, A_DV, A_DK), F32),
            pltpu.VMEM((n_chunk, A_HEADS, A_DV, A_DK), F32),
            pltpu.VMEM((n_chunk, A_HEADS, A_DV, A_DK), BF16),
            pltpu.VMEM((2, B_HEADS, 2 * BLOCK, BLOCK), F32),
            pltpu.VMEM((ts, MIX_WIDTH), BF16),
            pltpu.VMEM((ts, MIX_WIDTH), BF16),
            pltpu.VMEM((ts, D_MODEL), F32),
            pltpu.VMEM((ts, D_FF), BF16),
            pltpu.VMEM((ts, D_MODEL), F32),
            pltpu.VMEM((MIX_WIDTH, D_MODEL), BF16),
            pltpu.VMEM((D_MODEL, D_FF), BF16),
            pltpu.VMEM((D_MODEL, D_FF), BF16),
            pltpu.VMEM((D_FF, D_MODEL), BF16),
            pltpu.SemaphoreType.DMA((N_FFN_WEIGHTS,)),
        ],
        compiler_params=pltpu.CompilerParams(
            dimension_semantics=("arbitrary",), vmem_limit_bytes=V7X_VMEM_LIMIT_BYTES),
        name="mixer_ffn",
    )(sinks, qd, kit, ks, hv, gate, dec, bq, kp, kp, vt, vt, attn_norm_g,
      x2d, w_out, ln2_g, w_gate, w_up, w_down, final_g)


def kernel(x, ln1_g, w_in, lb_logits, hgrn_norm_g, attn_sinks, attn_norm_g, w_out, ln2_g,
           w_gate, w_up, w_down, final_g):
    bsz, seq, d = x.shape
    x2d = x.reshape(bsz * seq, d)
    *prep, wo16, wg16, wu16, wd16 = _in_proj(x2d, ln1_g[0:1], w_in[0], lb_logits, hgrn_norm_g[0:1],
                                             w_out[0], w_gate[0], w_up[0], w_down[0])
    out = _mixer_ffn(prep, attn_sinks[0], attn_norm_g[0:1], x2d, wo16, ln2_g[0:1],
                     wg16, wu16, wd16, final_g.reshape(1, d), seq)
    return out.reshape(bsz, seq, d)
```

```python
import functools

import jax
import jax.numpy as jnp
from jax import lax
from jax.experimental import pallas as pl
from jax.experimental.pallas import tpu as pltpu

D_MODEL = 1024
A_HEADS = 4
A_DK = 128
A_DV = 128
A_QK = A_HEADS * A_DK
A_WIDTH = A_HEADS * A_DV
CHUNK = 64
B_HEADS = 8
B_KV_HEADS = 2
B_HEAD_DIM = 64
B_WIDTH = B_HEADS * B_HEAD_DIM
B_KV = B_KV_HEADS * B_HEAD_DIM
WINDOW = 128
BLOCK = 128
MIX_WIDTH = A_WIDTH + B_WIDTH
D_FF = 2816
IN_COLS = 4 * A_QK + B_WIDTH + 2 * B_KV
EPS = 1e-6
NEG_BIG = -1e30
LANES = 128
KEY_VARIANTS = 2 * B_KV_HEADS
N_FFN_WEIGHTS = 4

F32 = jnp.float32
BF16 = jnp.bfloat16

V7X_VMEM_LIMIT_BYTES = 56 * 1024 * 1024

TM_PROJ = 512
TS_MIX = 512
FF_CHUNK = 256
HGRN_GROUP_CHUNKS = 2
PAIR = 2 * CHUNK
CAST_ROWS = 128
LOG2E = 1.4426950408889634

NT_DIMS = (((1,), (1,)), ((), ()))
TN_DIMS = (((0,), (0,)), ((), ()))


def _rms_scale(v):
    return lax.rsqrt(jnp.mean(v * v, axis=-1, keepdims=True) + EPS)


def _silu(v):
    return v * (1.0 / (1.0 + jnp.exp(-v)))


def _sigmoid(v):
    return 1.0 / (1.0 + jnp.exp(-v))


def _in_proj_kernel(x_ref, g_ref, w32_ref, lbl_ref, hng_ref, wo32_ref, wg32_ref, wu32_ref, wd32_ref,
                    qd_ref, kit_ref, ks_ref, hv_ref, gate_ref, dec_ref, bq_ref, kp_ref, vt_ref,
                    wo16_ref, wg16_ref, wu16_ref, wd16_ref, w_ref, *, n_down_slabs):
    tm = x_ref.shape[0]

    @pl.when(pl.program_id(0) == 0)
    def _():
        for r0 in range(0, D_MODEL, CAST_ROWS):
            w_ref[r0:r0 + CAST_ROWS, :] = w32_ref[r0:r0 + CAST_ROWS, :].astype(BF16)

    @pl.when(pl.program_id(0) < n_down_slabs)
    def _():
        wd16_ref[...] = wd32_ref[...].astype(BF16)

    u_parts, pf_parts = [], []
    for r0 in (0, tm // 2):
        xh = x_ref[r0:r0 + tm // 2, :]
        uh = (xh * _rms_scale(xh) * g_ref[...]).astype(BF16)
        u_parts.append(uh)
        pf_parts.append(jnp.dot(uh, w_ref[:, A_QK:2 * A_QK], preferred_element_type=F32))
    u = jnp.concatenate(u_parts, axis=0)
    pf = jnp.concatenate(pf_parts, axis=0)

    lbl = lbl_ref[...]
    lmax = jnp.max(lbl, axis=0, keepdims=True)
    le = jnp.exp(lbl - lmax)
    lb = le[0:1, :] / jnp.sum(le, axis=0, keepdims=True)

    r = lax.broadcasted_iota(jnp.int32, (CHUNK, CHUNK), 0)
    c = lax.broadcasted_iota(jnp.int32, (CHUNK, CHUNK), 1)
    tri = jnp.where(r >= c, 1.0, 0.0).astype(BF16)

    pq = jnp.dot(u, w_ref[:, 0:A_QK], preferred_element_type=F32)
    def attention_part():
        wo16_ref[...] = wo32_ref[...].astype(BF16)
        wg16_ref[...] = wg32_ref[...].astype(BF16)
        wu16_ref[...] = wu32_ref[...].astype(BF16)

        c0 = 2 * A_QK + 2 * A_WIDTH
        pa = jnp.dot(u, w_ref[:, c0:], preferred_element_type=F32)
        bq_ref[...] = (pa[:, :B_WIDTH] * (B_HEAD_DIM ** -0.5 * LOG2E)).astype(BF16)
        lo = lax.broadcasted_iota(jnp.int32, (tm, LANES), 1) < B_HEAD_DIM
        k_all = pa[:, B_WIDTH:B_WIDTH + B_KV]
        k_rot = pltpu.roll(k_all, B_HEAD_DIM, axis=1)
        kp_ref[0] = jnp.where(lo, k_all, 0.0).astype(BF16)
        kp_ref[1] = jnp.where(lo, 0.0, k_rot).astype(BF16)
        kp_ref[2] = jnp.where(lo, k_rot, 0.0).astype(BF16)
        kp_ref[3] = jnp.where(lo, 0.0, k_all).astype(BF16)
        vt_ref[...] = pa[:, B_WIDTH + B_KV:].T.astype(BF16)

    def gate_part():
        pg = jnp.dot(u, w_ref[:, 2 * A_QK + A_WIDTH:2 * A_QK + 2 * A_WIDTH], preferred_element_type=F32)
        gate_ref[...] = (_silu(pg) * hng_ref[...]).astype(BF16)

    def value_part():
        pi = jnp.dot(u, w_ref[:, 2 * A_QK:2 * A_QK + A_WIDTH], preferred_element_type=F32)
        hv_ref[...] = pi.astype(BF16)

    k_pair = []
    for ci in range(tm // CHUNK):
        rows = slice(ci * CHUNK, (ci + 1) * CHUNK)
        forget = lb + (1.0 - lb) * _sigmoid(pf[rows])
        key = 1.0 - forget
        logf = jnp.log(forget)
        p0 = logf.astype(BF16)
        r1 = logf - p0.astype(F32)
        p1 = r1.astype(BF16)
        p2 = (r1 - p1.astype(F32)).astype(BF16)
        bc3 = jnp.dot(tri, jnp.concatenate([p0, p1, p2], axis=-1), preferred_element_type=F32)
        bcum = bc3[:, :A_QK] + bc3[:, A_QK:2 * A_QK] + bc3[:, 2 * A_QK:]
        decay = jnp.exp(bcum[CHUNK - 1:CHUNK, :])
        k_intra = key * jnp.exp(-bcum)
        qd_ref[rows, :] = (_silu(pq[rows]) * jnp.exp(bcum)).astype(BF16)
        ks_ref[rows, :] = (k_intra * decay).astype(BF16)
        dec_ref[ci:ci + 1, :] = decay
        k_pair.append(k_intra)
        if len(k_pair) == PAIR // CHUNK:
            p0_row = (ci + 1) * CHUNK - PAIR
            kit_ref[:, p0_row:p0_row + PAIR] = jnp.concatenate(k_pair, axis=0).T.astype(BF16)
            k_pair = []

    attention_part()
    gate_part()
    value_part()


def _in_proj(x2d, ln1_g, w_in, lb_logits, hgrn_norm_g, w_out, w_gate, w_up, w_down):
    t = x2d.shape[0]
    tm = TM_PROJ
    n_steps = t // tm
    row_blk = lambda width: pl.BlockSpec((tm, width), lambda i: (i, 0))
    const = lambda shape, **kw: pl.BlockSpec(shape, lambda i: (0, 0), **kw)
    bf = lambda shape: jax.ShapeDtypeStruct(shape, BF16)

    def slab(w, rows):
        n_slabs = w.shape[0] // rows
        assert w.shape[0] % rows == 0 and n_slabs <= n_steps, (w.shape, rows)
        return pl.BlockSpec((rows, w.shape[1]), lambda i: (jnp.minimum(i, n_slabs - 1), 0))

    cast_weights = (w_out, w_gate, w_up, w_down)
    even_rows = D_MODEL // n_steps
    slab_rows = (even_rows, even_rows, even_rows, CAST_ROWS)
    slab_specs = lambda: [slab(w, rows) for w, rows in zip(cast_weights, slab_rows)]
    return pl.pallas_call(
        functools.partial(_in_proj_kernel, n_down_slabs=D_FF // CAST_ROWS),
        grid=(n_steps,),
        in_specs=[
            row_blk(D_MODEL),
            const((1, D_MODEL)),
            const((D_MODEL, IN_COLS), pipeline_mode=pl.Buffered(1)),
            const((2, A_QK)),
            const((1, A_WIDTH)),
        ] + slab_specs(),
        out_specs=[
            row_blk(A_QK), pl.BlockSpec((A_QK, tm), lambda i: (0, i)), row_blk(A_QK),
            row_blk(A_WIDTH), row_blk(A_WIDTH),
            pl.BlockSpec((tm // CHUNK, A_QK), lambda i: (i, 0)),
            row_blk(B_WIDTH),
            pl.BlockSpec((KEY_VARIANTS, tm, LANES), lambda i: (0, i, 0)),
            pl.BlockSpec((LANES, tm), lambda i: (0, i)),
        ] + slab_specs(),
        out_shape=[
            bf((t, A_QK)), bf((A_QK, t)), bf((t, A_QK)), bf((t, A_WIDTH)), bf((t, A_WIDTH)),
            jax.ShapeDtypeStruct((t // CHUNK, A_QK), F32),
            bf((t, B_WIDTH)),
            bf((KEY_VARIANTS, t, LANES)),
            bf((LANES, t)),
        ] + [bf(w.shape) for w in cast_weights],
        scratch_shapes=[pltpu.VMEM((D_MODEL, IN_COLS), BF16)],
        compiler_params=pltpu.CompilerParams(
            dimension_semantics=("arbitrary",), vmem_limit_bytes=V7X_VMEM_LIMIT_BYTES),
        name="in_proj",
    )(x2d, ln1_g, w_in, lb_logits, hgrn_norm_g, *cast_weights)


def _bias_table_init(bias_ref):
    kj = lax.broadcasted_iota(jnp.int32, (2 * BLOCK, BLOCK), 0)
    qi = lax.broadcasted_iota(jnp.int32, (2 * BLOCK, BLOCK), 1)
    dist = qi + BLOCK - kj
    valid = (dist >= 0) & (dist < WINDOW)
    valid_first = valid & (kj >= BLOCK)
    distf = dist.astype(F32)
    for h in range(B_HEADS):
        slope = 2.0 ** (-(8.0 / B_HEADS) * (h + 1))
        sc = (-slope * LOG2E) * distf
        bias_ref[0, h] = jnp.where(valid, sc, NEG_BIG)
        bias_ref[1, h] = jnp.where(valid_first, sc, NEG_BIG)


def _mixer_stages(first, sinks_ref, qd_ref, kit_ref, ks_ref, hv_ref, gate_ref, dec_ref, bq_ref,
                  kp_ref, kph_ref, vt_ref, vth_ref, ang_ref, y_ref,
                  state_ref, upd_ref, sprev_ref, bias_ref):
    ts = qd_ref.shape[0]
    n_chunk = ts // CHUNK
    n_blk = ts // BLOCK
    head_cols = [slice(h * A_DK, (h + 1) * A_DK) for h in range(A_HEADS)]

    def chunk_units(c_lo, c_hi):
        return [(ci, slice(ci * CHUNK, (ci + 1) * CHUNK), head_cols[h], h)
                for ci in range(c_lo, c_hi) for h in range(A_HEADS)]

    def hgrn_updates(c_lo, c_hi):
        units = chunk_units(c_lo, c_hi)
        upds = [lax.dot_general(hv_ref[rows, cs], ks_ref[rows, cs], TN_DIMS,
                                preferred_element_type=F32) for (_, rows, cs, _) in units]
        for upd, (ci, _, _, h) in zip(upds, units):
            upd_ref[ci, h] = upd

    def hgrn_scan(h):
        st = state_ref[h]
        for ci in range(n_chunk):
            sprev_ref[ci, h] = st.T.astype(BF16)
            st = st * dec_ref[ci:ci + 1, head_cols[h]] + upd_ref[ci, h]
        state_ref[h] = st

    def hgrn_scores(p, box):
        rows = slice(p * PAIR, (p + 1) * PAIR)
        box["a"] = [jnp.dot(qd_ref[rows, cs], kit_ref[cs, rows], preferred_element_type=F32)
                    for cs in head_cols]
        box["o_inter"] = [
            jnp.concatenate(
                [jnp.dot(qd_ref[ci * CHUNK:(ci + 1) * CHUNK, cs], sprev_ref[ci, h],
                         preferred_element_type=F32)
                 for ci in range(p * PAIR // CHUNK, (p + 1) * PAIR // CHUNK)], axis=0)
            for h, cs in enumerate(head_cols)]

    def hgrn_values(p, box):
        rows = slice(p * PAIR, (p + 1) * PAIR)
        r = lax.broadcasted_iota(jnp.int32, (PAIR, PAIR), 0)
        c = lax.broadcasted_iota(jnp.int32, (PAIR, PAIR), 1)
        same_chunk_causal = (c <= r) & (c >= (r // CHUNK) * CHUNK)
        a_s = [jnp.where(same_chunk_causal, a, 0.0).astype(BF16) for a in box["a"]]
        box["o"] = [jnp.dot(a, hv_ref[rows, cs], preferred_element_type=F32) + oi
                    for a, oi, cs in zip(a_s, box["o_inter"], head_cols)]

    def hgrn_store(p, box):
        rows = slice(p * PAIR, (p + 1) * PAIR)
        for o, cs in zip(box["o"], head_cols):
            gate = gate_ref[rows, cs].astype(F32)
            y_ref[rows, cs] = (o * _rms_scale(o) * gate).astype(y_ref.dtype)

    swa_units = [(kvh, par) for kvh in range(B_KV_HEADS) for par in range(2)]

    def swa_scores(j, box):
        s_ts = []
        for kvh, par in swa_units:
            rows = slice(j * BLOCK, (j + 1) * BLOCK)
            q_pairs = jnp.concatenate(
                [bq_ref[rows, (2 * kvh) * LANES:(2 * kvh + 1) * LANES],
                 bq_ref[rows, (2 * kvh + 1) * LANES:(2 * kvh + 2) * LANES]], axis=0)
            idx = 2 * kvh + par
            if j == 0:
                kk = jnp.concatenate([kph_ref[idx], kp_ref[idx, 0:BLOCK, :]], axis=0)
            else:
                kk = kp_ref[idx, (j - 1) * BLOCK:(j + 1) * BLOCK, :]
            s_ts.append(lax.dot_general(kk, q_pairs, NT_DIMS, preferred_element_type=F32))
        box["s"] = s_ts

    def swa_softmax(j, box):
        p_ts, inv_ls = [], []
        for (kvh, par), s_t in zip(swa_units, box["s"]):
            p_halves = []
            for half, h in enumerate((4 * kvh + par, 4 * kvh + 2 + par)):
                bias = bias_ref[first, h] if j == 0 else bias_ref[0, h]
                s = s_t[:, half * BLOCK:(half + 1) * BLOCK] + bias
                sink = sinks_ref[h] * LOG2E
                m = jnp.maximum(jnp.max(s, axis=0, keepdims=True), sink)
                p = jnp.exp2(s - m)
                inv_ls.append(1.0 / (jnp.sum(p, axis=0, keepdims=True) + jnp.exp2(sink - m)))
                p_halves.append(p.astype(BF16))
            p_ts.append(jnp.concatenate(p_halves, axis=1))
        box["p"], box["inv_l"] = p_ts, inv_ls

    def swa_values(j, box):
        o_t = [None] * B_HEADS
        for ui, ((kvh, par), p_t) in enumerate(zip(swa_units, box["p"])):
            drow = slice(kvh * B_HEAD_DIM, (kvh + 1) * B_HEAD_DIM)
            if j == 0:
                vt_h = jnp.concatenate([vth_ref[drow, :], vt_ref[drow, 0:BLOCK]], axis=1)
            else:
                vt_h = vt_ref[drow, (j - 1) * BLOCK:(j + 1) * BLOCK]
            pv = jnp.dot(vt_h, p_t, preferred_element_type=F32)
            for half, h in enumerate((4 * kvh + par, 4 * kvh + 2 + par)):
                o_t[h] = pv[:, half * BLOCK:(half + 1) * BLOCK] * box["inv_l"][2 * ui + half]
        box["o_t"] = o_t

    def swa_store(j, box):
        o_all_t = jnp.concatenate(box["o_t"], axis=0)
        scale = lax.rsqrt(jnp.sum(o_all_t * o_all_t, axis=0, keepdims=True) * (1.0 / B_WIDTH) + EPS)
        o_all = (o_all_t * scale).T
        y_ref[j * BLOCK:(j + 1) * BLOCK, A_WIDTH:] = (o_all * ang_ref[...]).astype(y_ref.dtype)

    group = HGRN_GROUP_CHUNKS
    hgrn = [functools.partial(hgrn_updates, c, c + group) for c in range(0, n_chunk, group)]
    hgrn += [functools.partial(hgrn_scan, h) for h in range(A_HEADS)]
    for p in range(ts // PAIR):
        box = {}
        hgrn += [functools.partial(f, p, box) for f in (hgrn_scores, hgrn_values, hgrn_store)]
    swa = []
    for j in range(n_blk):
        box = {}
        swa += [functools.partial(f, j, box) for f in (swa_scores, swa_softmax, swa_values, swa_store)]
    stages = []
    for k in range(max(len(hgrn), len(swa))):
        stages += hgrn[k:k + 1] + swa[k:k + 1]
    return stages


def _ffn_stages(x_ref, y_ref, wo_ref, g2_ref, wg_ref, wu_ref, wd_ref, h_ref, act_ref, h2_ref,
                before_down=()):
    box = {}

    def head():
        h = x_ref[...] + jnp.dot(y_ref[...], wo_ref[...], preferred_element_type=F32)
        box["u"] = (h * _rms_scale(h) * g2_ref[...]).astype(BF16)
        h_ref[...] = h

    def ff_up(c0):
        cs = slice(c0, c0 + FF_CHUNK)
        box["gate"] = jnp.dot(box["u"], wg_ref[:, cs], preferred_element_type=F32)
        box["up"] = jnp.dot(box["u"], wu_ref[:, cs], preferred_element_type=F32)

    def ff_act(c0):
        act_ref[:, c0:c0 + FF_CHUNK] = (_silu(box["gate"]) * box["up"]).astype(BF16)

    def down():
        h2_ref[...] = h_ref[...] + jnp.dot(act_ref[...], wd_ref[...], preferred_element_type=F32)

    stages = [head]
    for c0 in range(0, D_FF, FF_CHUNK):
        stages += [functools.partial(ff_up, c0), functools.partial(ff_act, c0)]
    return stages + list(before_down) + [down]


def _final_norm(h2_ref, gf_ref, o_ref):
    h2 = h2_ref[...]
    o_ref[...] = h2 * _rms_scale(h2) * gf_ref[...]


def _run_interleaved(major, minor):
    span = max(len(major) - 2, 1)
    done = 0
    for k, stage in enumerate(major):
        stage()
        want = min(len(minor), -(-(k + 1) * len(minor) // span))
        while done < want:
            minor[done]()
            done += 1


def _mixer_ffn_kernel(sinks_ref, qd_ref, kit_ref, ks_ref, hv_ref, gate_ref, dec_ref, bq_ref,
                      kp_ref, kph_ref, vt_ref, vth_ref, ang_ref,
                      x_ref, wo_hbm, g2_ref, wg_hbm, wu_hbm, wd_hbm, gf_ref, o_ref,
                      pad_ref, state_ref, upd_ref, sprev_ref, bias_ref, y_cur, y_prev, h_ref, act_ref, h2_ref,
                      wo_ref, wg_ref, wu_ref, wd_ref, w_sems, *, n_tiles, n_s):
    i = pl.program_id(0)
    seq_start = lax.rem(i, n_s) == 0
    weight_copies = [
        pltpu.make_async_copy(src, dst, w_sems.at[k])
        for k, (src, dst) in enumerate(((wo_hbm, wo_ref), (wg_hbm, wg_ref),
                                        (wu_hbm, wu_ref), (wd_hbm, wd_ref)))]

    @pl.when(i == 0)
    def _():
        for copy in weight_copies:
            copy.start()
        _bias_table_init(bias_ref)

    @pl.when(i == 1)
    def _():
        for copy in weight_copies:
            copy.wait()

    @pl.when(seq_start)
    def _():
        state_ref[...] = jnp.zeros_like(state_ref)

    first = jnp.where(seq_start, 1, 0)
    mixer_args = (first, sinks_ref, qd_ref, kit_ref, ks_ref, hv_ref, gate_ref, dec_ref, bq_ref,
                  kp_ref, kph_ref, vt_ref, vth_ref, ang_ref, y_cur,
                  state_ref, upd_ref, sprev_ref, bias_ref)
    ffn_args = (x_ref, y_prev, wo_ref, g2_ref, wg_ref, wu_ref, wd_ref, h_ref, act_ref, h2_ref)
    final_norm = functools.partial(_final_norm, h2_ref, gf_ref, o_ref)

    @pl.when(i == 0)
    def _():
        _run_interleaved(_mixer_stages(*mixer_args), [])
        y_prev[...] = y_cur[...]

    @pl.when(i == 1)
    def _():
        _run_interleaved(_ffn_stages(*ffn_args), _mixer_stages(*mixer_args))
        y_prev[...] = y_cur[...]

    @pl.when((i > 1) & (i < n_tiles))
    def _():
        _run_interleaved(_ffn_stages(*ffn_args, before_down=[final_norm]),
                         _mixer_stages(*mixer_args))
        y_prev[...] = y_cur[...]

    @pl.when(i == n_tiles)
    def _():
        _run_interleaved(_ffn_stages(*ffn_args, before_down=[final_norm]), [])

    @pl.when(i == n_tiles + 1)
    def _():
        final_norm()


def _mixer_ffn(prep, sinks, attn_norm_g, x2d, w_out, ln2_g, w_gate, w_up, w_down, final_g, seq):
    qd, kit, ks, hv, gate, dec, bq, kp, vt = prep
    t = x2d.shape[0]
    ts = TS_MIX
    n_tiles = t // ts
    n_s = seq // ts
    n_chunk = ts // CHUNK
    n_blk = ts // BLOCK
    cur = lambda i: jnp.minimum(i, n_tiles - 1)
    prev = lambda i: jnp.clip(i - 1, 0, n_tiles - 1)
    last = lambda i: jnp.clip(i - 2, 0, n_tiles - 1)
    halo = lambda i: jnp.maximum(cur(i) * n_blk - 1, 0)
    row_blk = lambda width: pl.BlockSpec((ts, width), lambda i: (cur(i), 0))
    const = lambda shape: pl.BlockSpec(shape, lambda i: (0, 0), pipeline_mode=pl.Buffered(1))
    in_hbm = pl.BlockSpec(memory_space=pl.ANY)
    assert n_tiles >= 2
    return pl.pallas_call(
        functools.partial(_mixer_ffn_kernel, n_tiles=n_tiles, n_s=n_s),
        grid=(n_tiles + 2,),
        in_specs=[
            pl.BlockSpec(memory_space=pltpu.SMEM),
            row_blk(A_QK), pl.BlockSpec((A_QK, ts), lambda i: (0, cur(i))), row_blk(A_QK),
            row_blk(A_WIDTH), row_blk(A_WIDTH),
            pl.BlockSpec((n_chunk, A_QK), lambda i: (cur(i), 0)),
            row_blk(B_WIDTH),
            pl.BlockSpec((KEY_VARIANTS, ts, LANES), lambda i: (0, cur(i), 0)),
            pl.BlockSpec((KEY_VARIANTS, BLOCK, LANES), lambda i: (0, halo(i), 0)),
            pl.BlockSpec((LANES, ts), lambda i: (0, cur(i))),
            pl.BlockSpec((LANES, BLOCK), lambda i: (0, halo(i))),
            pl.BlockSpec((1, B_WIDTH), lambda i: (0, 0)),
            pl.BlockSpec((ts, D_MODEL), lambda i: (prev(i), 0)),
            in_hbm,
            const((1, D_MODEL)),
            in_hbm, in_hbm, in_hbm,
            const((1, D_MODEL)),
        ],
        out_specs=pl.BlockSpec((ts, D_MODEL), lambda i: (last(i), 0)),
        out_shape=jax.ShapeDtypeStruct((t, D_MODEL), F32),
        scratch_shapes=[
            pltpu.VMEM((2 * CHUNK // 8, LANES), F32),
            pltpu.VMEM((A_HEADS, A_DV, A_DK), F32),
            pltpu.VMEM((n_chunk, A_HEADS, A_DV, A_DK), F32),
            pltpu.VMEM((n_chunk, A_HEADS, A_DV, A_DK), BF16),
            pltpu.VMEM((2, B_HEADS, 2 * BLOCK, BLOCK), F32),
            pltpu.VMEM((ts, MIX_WIDTH), BF16),
            pltpu.VMEM((ts, MIX_WIDTH), BF16),
            pltpu.VMEM((ts, D_MODEL), F32),
            pltpu.VMEM((ts, D_FF), BF16),
            pltpu.VMEM((ts, D_MODEL), F32),
            pltpu.VMEM((MIX_WIDTH, D_MODEL), BF16),
            pltpu.VMEM((D_MODEL, D_FF), BF16),
            pltpu.VMEM((D_MODEL, D_FF), BF16),
            pltpu.VMEM((D_FF, D_MODEL), BF16),
            pltpu.SemaphoreType.DMA((N_FFN_WEIGHTS,)),
        ],
        compiler_params=pltpu.CompilerParams(
            dimension_semantics=("arbitrary",), vmem_limit_bytes=V7X_VMEM_LIMIT_BYTES),
        name="mixer_ffn",
    )(sinks, qd, kit, ks, hv, gate, dec, bq, kp, kp, vt, vt, attn_norm_g,
      x2d, w_out, ln2_g, w_gate, w_up, w_down, final_g)


def kernel(x, ln1_g, w_in, lb_logits, hgrn_norm_g, attn_sinks, attn_norm_g, w_out, ln2_g,
           w_gate, w_up, w_down, final_g):
    bsz, seq, d = x.shape
    x2d = x.reshape(bsz * seq, d)
    *prep, wo16, wg16, wu16, wd16 = _in_proj(x2d, ln1_g[0:1], w_in[0], lb_logits, hgrn_norm_g[0:1],
                                             w_out[0], w_gate[0], w_up[0], w_down[0])
    out = _mixer_ffn(prep, attn_sinks[0], attn_norm_g[0:1], x2d, wo16, ln2_g[0:1],
                     wg16, wu16, wd16, final_g.reshape(1, d), seq)
    return out.reshape(bsz, seq, d)
```
